```python
import math
import jax, jax.numpy as jnp
from jax import lax
import numpy as np

D_MODEL = 1024
BATCH = 32
SEQ = 256
DEPTH = 4
DEC_BATCH = 8
DEC_SEQ = 1024
PAST_LEN = 256

GRID_W = 64
N_MIXERS = 2
N_MLA = (DEPTH + 1) // 2
N_HGRN = DEPTH // 2
N_DENSE = (DEPTH + 1) // 2
N_MOE = DEPTH // 2
N_MOD = 6
EPS = 1e-6
MLA_HEADS = 16
QK_NOPE = 64
QK_ROPE = 32
V_HEAD = 64
Q_LORA = 512
KV_LORA = 256
ROPE_AXIS = QK_ROPE // 2
ROPE_THETA = 10000.0
Q_BLOCK = 128
HG_EXPAND = 128
HG_HEADS = D_MODEL // HG_EXPAND
HG_DK = HG_EXPAND
HG_DV = D_MODEL // HG_HEADS
CHUNK = 16
D_FF = 2816
N_EXPERTS = 8
TOP_K = 2
E_FF = 3584

kernel_name = 'hybrid_mla_hgrn2_diffusion_step'

F32 = jnp.float32


def rmsnorm(x, g):
    xf = x.astype(F32)
    y = xf * lax.rsqrt(jnp.mean(xf * xf, axis=-1, keepdims=True) + EPS)
    return (y * g.astype(F32)).astype(x.dtype)


def ada(cond, w, b):
    m = jax.nn.silu(cond) @ w + b
    return m.reshape(cond.shape[:-1] + (N_MOD, D_MODEL))


def axial_rope_tables(n_tok):
    rows = n_tok // GRID_W
    r, col = jnp.meshgrid(jnp.arange(rows), jnp.arange(GRID_W), indexing='ij')
    r = r.reshape(-1).astype(F32)
    col = col.reshape(-1).astype(F32)
    inv = ROPE_THETA ** (-jnp.arange(0, ROPE_AXIS, 2, dtype=F32) / ROPE_AXIS)
    ang_r = r[:, None] * inv
    ang_c = col[:, None] * inv
    ang = jnp.concatenate([ang_r, ang_r, ang_c, ang_c], axis=-1)
    return jnp.cos(ang), jnp.sin(ang)


def apply_axial_rope(x, cos, sin):
    xa = x.reshape(x.shape[:-1] + (2, 2, ROPE_AXIS // 2))
    rot = jnp.stack([-xa[..., 1, :], xa[..., 0, :]], axis=-2).reshape(x.shape)
    return (x.astype(F32) * cos + rot.astype(F32) * sin).astype(x.dtype)


def mla_project(h, w_dq, q_norm, w_uq, w_dkv, kv_norm):
    cq = rmsnorm(h @ w_dq, q_norm)
    q = jnp.einsum('btc,chd->bthd', cq, w_uq)
    kv = h @ w_dkv
    ckv = rmsnorm(kv[..., :KV_LORA], kv_norm)
    return q[..., :QK_NOPE], q[..., QK_NOPE:], ckv, kv[..., KV_LORA:]


def mla_attend(q_nope, q_rope, ckv, k_rope, w_uk, w_uv, w_o):
    k_nope = jnp.einsum('bsc,chd->bshd', ckv, w_uk)
    v = jnp.einsum('bsc,chd->bshd', ckv, w_uv)
    B, T = q_nope.shape[:2]
    nb = T // Q_BLOCK
    scale = 1.0 / math.sqrt(QK_NOPE + QK_ROPE)

    def blocks(u):
        return jnp.moveaxis(u.reshape((B, nb, Q_BLOCK) + u.shape[2:]), 1, 0)

    def one(args):
        qn, qr = args
        s = jnp.einsum('bqhd,bkhd->bhqk', qn, k_nope) + jnp.einsum('bqhr,bkr->bhqk', qr, k_rope)
        p = jax.nn.softmax(s.astype(F32) * scale, axis=-1).astype(v.dtype)
        return jnp.einsum('bhqk,bkhd->bqhd', p, v)

    o = lax.map(one, (blocks(q_nope), blocks(q_rope)))
    o = jnp.moveaxis(o, 0, 1).reshape(B, T, MLA_HEADS, V_HEAD)
    return jnp.einsum('bthd,hde->bte', o, w_o)


def hgrn_lower_bounds(lb_logits):
    cum = jnp.cumsum(jax.nn.softmax(lb_logits.astype(F32), axis=0), axis=0)
    return cum - cum[0:1]


def hgrn_project(h, w_q, w_f, w_i, w_g, lb):
    B, T, _ = h.shape
    q = (h @ w_q).reshape(B, T, HG_HEADS, HG_DK)
    z = jnp.einsum('btc,ncd->btnd', h, w_f).astype(F32)
    log_f = jnp.logaddexp(jnp.log(lb), jnp.log1p(-lb) + jax.nn.log_sigmoid(z))
    k = (1.0 - lb) * jax.nn.sigmoid(-z)
    log_f = log_f.reshape(B, T, 2, HG_HEADS, HG_DK)
    k = k.reshape(B, T, 2, HG_HEADS, HG_DK)
    i = (h @ w_i).reshape(B, T, HG_HEADS, HG_DV)
    g = h @ w_g
    return q, k, log_f, i, g


def gla_chunked(q, k, v, log_f, s0):
    B, T, H, _ = q.shape
    n = T // CHUNK
    mask = jnp.tril(jnp.ones((CHUNK, CHUNK), dtype=bool))

    def to_chunks(u):
        return jnp.moveaxis(u.astype(F32).reshape(B, n, CHUNK, H, u.shape[-1]), 1, 0)

    def step(S, inp):
        qc, kc, vc, lc = inp
        b = jnp.cumsum(lc, axis=1)
        b_mid = b[:, CHUNK // 2 - 1:CHUNK // 2]
        qi = qc * jnp.exp(b - b_mid)
        ki = kc * jnp.exp(b_mid - b)
        A = jnp.einsum('bthd,bshd->bhts', qi, ki)
        A = jnp.where(mask, A, 0.0)
        o = jnp.einsum('bhts,bshv->bthv', A, vc) + jnp.einsum('bthd,bhdv->bthv', qc * jnp.exp(b), S)
        b_last = b[:, -1:]
        S = jnp.exp(b_last[:, 0])[..., None] * S + jnp.einsum('bshd,bshv->bhdv', kc * jnp.exp(b_last - b), vc)
        return S, o

    S, o = lax.scan(step, s0.astype(F32), (to_chunks(q), to_chunks(k), to_chunks(v), to_chunks(log_f)))
    o = jnp.moveaxis(o, 0, 1).reshape(B, T, H, v.shape[-1])
    return o, S


def hgrn_mix(q, k, log_f, i, g, s0, o_norm, w_o):
    B, T = q.shape[:2]
    flip = lambda u: jnp.flip(u, axis=1)
    o_f, s_f = gla_chunked(q, k[:, :, 0], i, log_f[:, :, 0], s0[:, 0])
    o_b, s_b = gla_chunked(flip(q), flip(k[:, :, 1]), flip(i), flip(log_f[:, :, 1]), s0[:, 1])
    o = o_f + flip(o_b)
    o = rmsnorm(o, o_norm) * jax.nn.silu(g.astype(F32)).reshape(o.shape)
    out = o.reshape(B, T, D_MODEL).astype(g.dtype) @ w_o
    return out, jnp.stack([s_f, s_b], axis=1)


def swiglu(h, w1, w3, w2):
    return (jax.nn.silu(h @ w1) * (h @ w3)) @ w2


def moe_swiglu(h, router, w1, w3, w2):
    probs = jax.nn.softmax((h @ router).astype(F32), axis=-1)
    top_p, top_i = lax.top_k(probs, TOP_K)
    top_p = top_p / jnp.sum(top_p, axis=-1, keepdims=True)
    combine = jnp.sum(jax.nn.one_hot(top_i, N_EXPERTS, dtype=F32) * top_p[..., None], axis=-2)
    out = jnp.zeros_like(h)
    for e in range(N_EXPERTS):
        out = out + combine[..., e:e + 1].astype(h.dtype) * swiglu(h, w1[e], w3[e], w2[e])
    return out


def setup_inputs(seed: int = 0) -> dict:
    key = jax.random.key(seed)
    ks = iter(jax.random.split(key, 48))
    D = D_MODEL

    def nrm(shape, scale=1.0):
        return jax.random.normal(next(ks), shape, F32) * scale

    def gain(shape):
        return 1.0 + nrm(shape, 0.02)

    return {
        'x_prompt': nrm((BATCH, SEQ, D)),
        'x_sample': nrm((DEC_BATCH, DEC_SEQ, D)),
        'cache_ckv': nrm((DEC_BATCH, N_MLA, PAST_LEN, KV_LORA)),
        'cache_krope': nrm((DEC_BATCH, N_MLA, PAST_LEN, QK_ROPE)),
        'state_hgrn': nrm((DEC_BATCH, N_HGRN, 2, HG_HEADS, HG_DK, HG_DV), 0.5),
        'c': nrm((DEC_BATCH, D)),
        'c_ctx': nrm((D,)),
        'w_ada': nrm((DEPTH, D, N_MOD * D), 0.5 * D ** -0.5),
        'b_ada': nrm((DEPTH, N_MOD * D), 0.02),
        'norm_mix': gain((DEPTH, D)),
        'norm_ffn': gain((DEPTH, D)),
        'mla_w_dq': nrm((N_MLA, D, Q_LORA), D ** -0.5),
        'mla_q_norm': gain((N_MLA, Q_LORA)),
        'mla_w_uq': nrm((N_MLA, Q_LORA, MLA_HEADS, QK_NOPE + QK_ROPE), Q_LORA ** -0.5),
        'mla_w_dkv': nrm((N_MLA, D, KV_LORA + QK_ROPE), D ** -0.5),
        'mla_kv_norm': gain((N_MLA, KV_LORA)),
        'mla_w_uk': nrm((N_MLA, KV_LORA, MLA_HEADS, QK_NOPE), KV_LORA ** -0.5),
        'mla_w_uv': nrm((N_MLA, KV_LORA, MLA_HEADS, V_HEAD), KV_LORA ** -0.5),
        'mla_w_o': nrm((N_MLA, MLA_HEADS, V_HEAD, D), (MLA_HEADS * V_HEAD) ** -0.5),
        'hg_w_q': nrm((N_HGRN, D, D), D ** -0.5),
        'hg_w_f': nrm((N_HGRN, 2, D, D), D ** -0.5),
        'hg_w_i': nrm((N_HGRN, D, D), D ** -0.5),
        'hg_w_g': nrm((N_HGRN, D, D), D ** -0.5),
        'hg_lb_logits': nrm((N_HGRN, 2, D)),
        'hg_o_norm': gain((N_HGRN, HG_DV)),
        'hg_w_o': nrm((N_HGRN, D, D), D ** -0.5),
        'ffn_w1': nrm((N_DENSE, D, D_FF), D ** -0.5),
        'ffn_w3': nrm((N_DENSE, D, D_FF), D ** -0.5),
        'ffn_w2': nrm((N_DENSE, D_FF, D), D_FF ** -0.5),
        'moe_router': nrm((N_MOE, D, N_EXPERTS), D ** -0.5),
        'moe_w1': nrm((N_MOE, N_EXPERTS, D, E_FF), D ** -0.5),
        'moe_w3': nrm((N_MOE, N_EXPERTS, D, E_FF), D ** -0.5),
        'moe_w2': nrm((N_MOE, N_EXPERTS, E_FF, D), E_FF ** -0.5),
        'final_norm': gain((D,)),
    }


def reference(x_prompt, x_sample, cache_ckv, cache_krope, state_hgrn, c, c_ctx, w_ada, b_ada,
              norm_mix, norm_ffn, mla_w_dq, mla_q_norm, mla_w_uq, mla_w_dkv, mla_kv_norm, mla_w_uk,
              mla_w_uv, mla_w_o, hg_w_q, hg_w_f, hg_w_i, hg_w_g, hg_lb_logits, hg_o_norm, hg_w_o,
              ffn_w1, ffn_w3, ffn_w2, moe_router, moe_w1, moe_w3, moe_w2, final_norm):
    xp, xs = x_prompt, x_sample
    cos, sin = axial_rope_tables(xs.shape[1])
    hg_lb = hgrn_lower_bounds(hg_lb_logits)
    new_ckv, new_krope, new_hgrn = [], [], []
    for l in range(DEPTH):
        j = l // N_MIXERS
        mp = ada(c_ctx, w_ada[l], b_ada[l])[None, None]
        ms = ada(c, w_ada[l], b_ada[l])[:, None]
        hp = rmsnorm(xp, norm_mix[l]) * (1.0 + mp[..., 1, :]) + mp[..., 0, :]
        hs = rmsnorm(xs, norm_mix[l]) * (1.0 + ms[..., 1, :]) + ms[..., 0, :]
        if l % N_MIXERS == 0:
            mla_w = (mla_w_dq[j], mla_q_norm[j], mla_w_uq[j], mla_w_dkv[j], mla_kv_norm[j])
            qn_p, qr_p, ckv_p, kr_p = mla_project(hp, *mla_w)
            yp = mla_attend(qn_p, qr_p, ckv_p, kr_p, mla_w_uk[j], mla_w_uv[j], mla_w_o[j])
            qn_s, qr_s, ckv_s, kr_s = mla_project(hs, *mla_w)
            qr_s = apply_axial_rope(qr_s, cos[:, None], sin[:, None])
            kr_s = apply_axial_rope(kr_s, cos, sin)
            ckv_all = jnp.concatenate([cache_ckv[:, j], ckv_s], axis=1)
            kr_all = jnp.concatenate([cache_krope[:, j], kr_s], axis=1)
            ys = mla_attend(qn_s, qr_s, ckv_all, kr_all, mla_w_uk[j], mla_w_uv[j], mla_w_o[j])
            new_ckv.append(ckv_p)
            new_krope.append(kr_p)
        else:
            hg_w = (hg_w_q[j], hg_w_f[j], hg_w_i[j], hg_w_g[j], hg_lb[j])
            q_p, k_p, lf_p, i_p, g_p = hgrn_project(hp, *hg_w)
            s_zero = jnp.zeros((xp.shape[0], 2, HG_HEADS, HG_DK, HG_DV), F32)
            yp, st_p = hgrn_mix(q_p, k_p, lf_p, i_p, g_p, s_zero, hg_o_norm[j], hg_w_o[j])
            q_s, k_s, lf_s, i_s, g_s = hgrn_project(hs, *hg_w)
            ys, _ = hgrn_mix(q_s, k_s, lf_s, i_s, g_s, state_hgrn[:, j], hg_o_norm[j], hg_w_o[j])
            new_hgrn.append(st_p.astype(xp.dtype))
        xp = xp + mp[..., 2, :] * yp
        xs = xs + ms[..., 2, :] * ys
        hp = rmsnorm(xp, norm_ffn[l]) * (1.0 + mp[..., 4, :]) + mp[..., 3, :]
        hs = rmsnorm(xs, norm_ffn[l]) * (1.0 + ms[..., 4, :]) + ms[..., 3, :]
        if l % 2 == 0:
            fp = swiglu(hp, ffn_w1[j], ffn_w3[j], ffn_w2[j])
            fs = swiglu(hs, ffn_w1[j], ffn_w3[j], ffn_w2[j])
        else:
            fp = moe_swiglu(hp, moe_router[j], moe_w1[j], moe_w3[j], moe_w2[j])
            fs = moe_swiglu(hs, moe_router[j], moe_w1[j], moe_w3[j], moe_w2[j])
        xp = xp + mp[..., 5, :] * fp
        xs = xs + ms[..., 5, :] * fs
    y_prompt = rmsnorm(xp, final_norm)
    y_sample = rmsnorm(xs, final_norm)
    new_ckv = jnp.stack(new_ckv, axis=1)
    new_krope = jnp.stack(new_krope, axis=1)
    new_hgrn = jnp.stack(new_hgrn, axis=1)
    return (y_prompt, y_sample, new_ckv, new_krope, new_hgrn)
```

```python
import functools
import math

import jax
import jax.numpy as jnp
from jax import lax
from jax.experimental import pallas as pl
from jax.experimental.pallas import tpu as pltpu

F32 = jnp.float32
BF16 = jnp.bfloat16
I32 = jnp.int32

D = 1024
BATCH, SEQ = 32, 256
DEC_BATCH, DEC_SEQ = 8, 1024
PAST = 256
DEPTH = 4
R_P = BATCH * SEQ
R_S = DEC_BATCH * DEC_SEQ
R = R_P + R_S
N_COND = DEC_BATCH + 1
COND_PAD = 16
N_MOD = 6
EPS = 1e-6
GRID_W = 64
HEADS, QK_NOPE, QK_ROPE, V_HEAD = 16, 64, 32, 64
Q_LORA, KV_LORA = 512, 256
N_PAIR = HEADS // 2
ROPE_AXIS = QK_ROPE // 2
ROPE_THETA = 10000.0
HG_H, HG_DK = 8, 128
BLK = 128
CHUNK = 16
D_FF = 2816
N_EXP, TOP_K, E_FF = 8, 2, 3584
LANES = 128
SUB = 8
ROW_TILE = D // LANES
TM = 512
TM_MOE = 512
N_SLOT = TOP_K * R + N_EXP * TM_MOE
N_TILE = N_SLOT // TM_MOE
DMA_CHUNK = 1024
MIB = 1024 * 1024
assert DEPTH % 2 == 0


def _cparams(sem, vmem_mib=40):
    return pltpu.CompilerParams(dimension_semantics=sem, vmem_limit_bytes=vmem_mib * MIB)


def _cond_idx(i, tm):
    r0 = i * tm
    return jnp.where(r0 < R_P, DEC_BATCH, (r0 - R_P) // DEC_SEQ)


def _rope_idx(i, tm):
    nb = DEC_SEQ // tm
    r0 = i * tm
    return jnp.where(r0 < R_P, nb, ((r0 - R_P) // tm) % nb)


def _normmod(x, gain, scale, shift):
    ms = jnp.mean(x * x, axis=-1, keepdims=True)
    return (x * lax.rsqrt(ms + EPS)) * gain * (1.0 + scale) + shift


def _rms(x, gain):
    ms = jnp.mean(x * x, axis=-1, keepdims=True)
    return (x * lax.rsqrt(ms + EPS)) * gain


def _silu(x):
    return x / (1.0 + jnp.exp(-x))


def _dot(a, b):
    return jnp.dot(a, b, preferred_element_type=F32)


def _dot_nt(a, b):
    return lax.dot_general(a, b, (((1,), (1,)), ((), ())), preferred_element_type=F32)


def _split3(x):
    hi = x.astype(BF16)
    r1 = x - hi.astype(F32)
    mid = r1.astype(BF16)
    lo = (r1 - mid.astype(F32)).astype(BF16)
    return hi, mid, lo


def _ada_kernel(c_ref, w_ref, b_ref, o_ref):
    o_ref[...] = _dot(_silu(c_ref[...]), w_ref[...]) + b_ref[...]


def _ada(cond, w_ada, b_ada):
    tn = 1536
    out = pl.pallas_call(
        _ada_kernel,
        grid=(DEPTH, N_MOD * D // tn),
        in_specs=[
            pl.BlockSpec((COND_PAD, D), lambda l, j: (0, 0)),
            pl.BlockSpec((None, D, tn), lambda l, j: (l, 0, j)),
            pl.BlockSpec((None, 1, tn), lambda l, j: (l, 0, j)),
        ],
        out_specs=pl.BlockSpec((None, COND_PAD, tn), lambda l, j: (l, 0, j)),
        out_shape=jax.ShapeDtypeStruct((DEPTH, COND_PAD, N_MOD * D), F32),
        compiler_params=_cparams(("arbitrary", "arbitrary")),
        name="ada",
    )(cond, w_ada, b_ada.reshape(DEPTH, 1, N_MOD * D))
    return out[:, :N_COND].reshape(DEPTH, N_COND, N_MOD, D)


def _lb_kernel(x_ref, o_ref):
    x = x_ref[...]
    e = jnp.exp(x - jnp.max(x, axis=0, keepdims=True))
    sm = e / jnp.sum(e, axis=0, keepdims=True)
    n = x.shape[0]
    cum = sm[0]
    o_ref[0] = jnp.zeros_like(cum)
    for i in range(1, n):
        cum_i = cum + sm[i]
        o_ref[i] = cum_i - sm[0]
        cum = cum_i


def _lower_bounds(lb_logits):
    return pl.pallas_call(
        _lb_kernel,
        out_shape=jax.ShapeDtypeStruct(lb_logits.shape, F32),
        name="hgrn_lower_bounds",
    )(lb_logits)


def _mla_proj_kernel(x_ref, m_ref, g_ref, w_ref, qn_ref, kvn_ref, tc_ref, ts_ref,
                     cq_ref, ckv_ref, kr_ref):
    m = m_ref[...]
    h = _normmod(x_ref[...], g_ref[...], m[1:2], m[0:1]).astype(BF16)
    y = _dot(h, w_ref[...])
    cq_ref[...] = _rms(y[:, :Q_LORA], qn_ref[...]).astype(BF16)
    ckv_ref[...] = _rms(y[:, Q_LORA:Q_LORA + KV_LORA], kvn_ref[...])
    slab = y[:, Q_LORA + KV_LORA:]
    kr_ref[...] = slab * tc_ref[...] + pltpu.roll(slab, LANES - QK_ROPE, 1) * ts_ref[...]


def _mla_proj(x, mods_l, gain, w_a, q_norm, kv_norm, tk_c, tk_s):
    n_a = w_a.shape[1]
    row = lambda i: (i, 0)
    fixed = lambda i: (0, 0)
    return pl.pallas_call(
        _mla_proj_kernel,
        grid=(R // TM,),
        in_specs=[
            pl.BlockSpec((TM, D), row),
            pl.BlockSpec((None, N_MOD, D), lambda i: (_cond_idx(i, TM), 0, 0)),
            pl.BlockSpec((1, D), fixed),
            pl.BlockSpec((D, n_a), fixed),
            pl.BlockSpec((1, Q_LORA), fixed),
            pl.BlockSpec((1, KV_LORA), fixed),
            pl.BlockSpec((TM, LANES), lambda i: (_rope_idx(i, TM), 0)),
            pl.BlockSpec((TM, LANES), lambda i: (_rope_idx(i, TM), 0)),
        ],
        out_specs=[
            pl.BlockSpec((TM, Q_LORA), row),
            pl.BlockSpec((TM, KV_LORA), row),
            pl.BlockSpec((TM, LANES), row),
        ],
        out_shape=[
            jax.ShapeDtypeStruct((R, Q_LORA), BF16),
            jax.ShapeDtypeStruct((R, KV_LORA), F32),
            jax.ShapeDtypeStruct((R, LANES), F32),
        ],
        compiler_params=_cparams(("arbitrary",)),
        name="mla_proj",
    )(x, mods_l, gain, w_a, q_norm, kv_norm, tk_c, tk_s)


def _q_up_kernel(cq_ref, w_ref, tc_ref, ts_ref, q_ref):
    y = _dot(cq_ref[...], w_ref[...])
    tc = tc_ref[...]
    ts = ts_ref[...]
    for p in range(N_PAIR):
        lo = p * 2 * LANES
        q_ref[:, lo:lo + LANES] = y[:, lo:lo + LANES].astype(BF16)
        hi = y[:, lo + LANES:lo + 2 * LANES]
        q_ref[:, lo + LANES:lo + 2 * LANES] = (hi * tc + pltpu.roll(hi, LANES // 2, 1) * ts).astype(BF16)


def _q_up(cq, w_uq2, tq_c, tq_s):
    nq = w_uq2.shape[1]
    return pl.pallas_call(
        _q_up_kernel,
        grid=(R // TM,),
        in_specs=[
            pl.BlockSpec((TM, Q_LORA), lambda i: (i, 0)),
            pl.BlockSpec((Q_LORA, nq), lambda i: (0, 0)),
            pl.BlockSpec((TM, LANES), lambda i: (_rope_idx(i, TM), 0)),
            pl.BlockSpec((TM, LANES), lambda i: (_rope_idx(i, TM), 0)),
        ],
        out_specs=pl.BlockSpec((TM, nq), lambda i: (i, 0)),
        out_shape=jax.ShapeDtypeStruct((R, nq), BF16),
        compiler_params=_cparams(("arbitrary",)),
        name="mla_q_up",
    )(cq, w_uq2, tq_c, tq_s)


def _kv_up_kernel(c_ref, r_ref, wk_ref, wkr_ref, wv_ref, k_ref, v_ref):
    c = c_ref[...].astype(BF16)
    r = r_ref[...].astype(BF16)
    k_ref[...] = (_dot(c, wk_ref[...]) + _dot(r, wkr_ref[...])).astype(BF16)
    v_ref[...] = _dot(c, wv_ref[...]).astype(BF16)


def _kv_up(ckv, kr, n_rows, wk, wkr, wv):
    nk = wk.shape[1]
    nv = wv.shape[1]
    fixed = lambda i: (0, 0)
    return pl.pallas_call(
        _kv_up_kernel,
        grid=(n_rows // TM,),
        in_specs=[
            pl.BlockSpec((TM, KV_LORA), lambda i: (i, 0)),
            pl.BlockSpec((TM, LANES), lambda i: (i, 0)),
            pl.BlockSpec((KV_LORA, nk), fixed),
            pl.BlockSpec((LANES, nk), fixed),
            pl.BlockSpec((KV_LORA, nv), fixed),
        ],
        out_specs=[
            pl.BlockSpec((TM, nk), lambda i: (i, 0)),
            pl.BlockSpec((TM, nv), lambda i: (i, 0)),
        ],
        out_shape=[
            jax.ShapeDtypeStruct((n_rows, nk), BF16),
            jax.ShapeDtypeStruct((n_rows, nv), BF16),
        ],
        compiler_params=_cparams(("arbitrary",)),
        name="mla_kv_up",
    )(ckv, kr, wk, wkr, wv)


def _attn_kernel(q_ref, k_ref, v_ref, o_ref):
    q = q_ref[...]
    k = k_ref[...]
    v = v_ref[...]
    scale = 1.0 / math.sqrt(QK_NOPE + QK_ROPE)
    lq = lax.broadcasted_iota(I32, (1, 2 * LANES), 1)
    lv = lax.broadcasted_iota(I32, (1, LANES), 1)
    sel_a = (lq < QK_NOPE) | ((lq >= LANES) & (lq < LANES + QK_ROPE))
    sel_b = ((lq >= QK_NOPE) & (lq < LANES)) | ((lq >= LANES + QK_ROPE) & (lq < LANES + 2 * QK_ROPE))
    zq = jnp.zeros_like(q)
    zv = jnp.zeros_like(v)
    out = None
    for sel, vsel in ((sel_a, lv < V_HEAD), (sel_b, lv >= V_HEAD)):
        s = _dot_nt(jnp.where(sel, q, zq), k) * scale
        e = jnp.exp(s - jnp.max(s, axis=-1, keepdims=True))
        den = jnp.sum(e, axis=-1, keepdims=True)
        o = _dot(e.astype(BF16), jnp.where(vsel, v, zv)) / den
        out = o if out is None else out + o
    o_ref[...] = out.astype(BF16)


def _attention(q2, k2, v, n_batch, t_len, s_len, q_row0, tq):
    nq = t_len // tq
    qb0 = q_row0 // tq
    return pl.pallas_call(
        _attn_kernel,
        grid=(n_batch, N_PAIR, nq),
        in_specs=[
            pl.BlockSpec((tq, 2 * LANES), lambda b, p, i: (qb0 + b * nq + i, p)),
            pl.BlockSpec((s_len, 2 * LANES), lambda b, p, i: (b, p)),
            pl.BlockSpec((s_len, LANES), lambda b, p, i: (b, p)),
        ],
        out_specs=pl.BlockSpec((tq, LANES), lambda b, p, i: (b * nq + i, p)),
        out_shape=jax.ShapeDtypeStruct((n_batch * t_len, HEADS * V_HEAD), BF16),
        compiler_params=_cparams(("arbitrary", "arbitrary", "arbitrary")),
        name="mla_attention",
    )(q2, k2, v)


def _mm_resid_kernel(a_ref, w_ref, x_ref, m_ref, o_ref, *, gate_idx):
    gate = m_ref[...][gate_idx:gate_idx + 1]
    o_ref[...] = x_ref[...] + gate * _dot(a_ref[...], w_ref[...])


def _mm_resid(a, w, x, mods_l, gate_idx):
    k = a.shape[1]
    return pl.pallas_call(
        functools.partial(_mm_resid_kernel, gate_idx=gate_idx),
        grid=(R // TM,),
        in_specs=[
            pl.BlockSpec((TM, k), lambda i: (i, 0)),
            pl.BlockSpec((k, D), lambda i: (0, 0)),
            pl.BlockSpec((TM, D), lambda i: (i, 0)),
            pl.BlockSpec((None, N_MOD, D), lambda i: (_cond_idx(i, TM), 0, 0)),
        ],
        out_specs=pl.BlockSpec((TM, D), lambda i: (i, 0)),
        out_shape=jax.ShapeDtypeStruct((R, D), F32),
        compiler_params=_cparams(("arbitrary",)),
        name="mm_resid",
    )(a, w, x, mods_l)


FFN_CHUNK = 1408


def _ffn_kernel(x_ref, m_ref, g_ref, w1_ref, w3_ref, w2_ref, o_ref):
    x = x_ref[...]
    m = m_ref[...]
    h = _normmod(x, g_ref[...], m[4:5], m[3:4]).astype(BF16)
    acc = jnp.zeros(x.shape, F32)
    for c in range(D_FF // FFN_CHUNK):
        sl = slice(c * FFN_CHUNK, (c + 1) * FFN_CHUNK)
        a = _dot(h, w1_ref[:, sl])
        b = _dot(h, w3_ref[:, sl])
        acc = acc + _dot((_silu(a) * b).astype(BF16), w2_ref[sl, :])
    o_ref[...] = x + m[5:6] * acc


def _ffn(x, mods_l, gain, w1, w3, w2):
    tm = 256
    fixed = lambda i: (0, 0)
    return pl.pallas_call(
        _ffn_kernel,
        grid=(R // tm,),
        in_specs=[
            pl.BlockSpec((tm, D), lambda i: (i, 0)),
            pl.BlockSpec((None, N_MOD, D), lambda i: (_cond_idx(i, tm), 0, 0)),
            pl.BlockSpec((1, D), fixed),
            pl.BlockSpec((D, D_FF), fixed),
            pl.BlockSpec((D, D_FF), fixed),
            pl.BlockSpec((D_FF, D), fixed),
        ],
        out_specs=pl.BlockSpec((tm, D), lambda i: (i, 0)),
        out_shape=jax.ShapeDtypeStruct((R, D), F32),
        compiler_params=_cparams(("arbitrary",), 56),
        name="dense_swiglu",
    )(x, mods_l, gain, w1, w3, w2)


def _forget_gate(z, lb):
    e = jnp.exp(-jnp.abs(z))
    log_sig = jnp.minimum(z, 0.0) - jnp.log1p(e)
    a = jnp.log(lb)
    b = jnp.log1p(-lb) + log_sig
    log_f = jnp.maximum(a, b) + jnp.log1p(jnp.exp(-jnp.abs(a - b)))
    k = (1.0 - lb) * (jnp.where(z >= 0, e, 1.0) / (1.0 + e))
    return k, log_f


def _hg_proj_kernel(x_ref, m_ref, g_ref, w_ref, lb_ref, q_ref, k_ref, lf_ref, v_ref, gg_ref, h_scr):
    j = pl.program_id(1)

    @pl.when(j == 0)
    def _():
        m = m_ref[...]
        h_scr[...] = _normmod(x_ref[...], g_ref[...], m[1:2], m[0:1]).astype(BF16)

    y = _dot(h_scr[...], w_ref[...])

    @pl.when(j == 0)
    def _():
        q_ref[...] = y

    @pl.when((j == 1) | (j == 2))
    def _():
        k, log_f = _forget_gate(y, lb_ref[...])
        k_ref[...] = k
        lf_ref[...] = log_f

    @pl.when(j == 3)
    def _():
        v_ref[...] = y.astype(BF16)

    @pl.when(j == 4)
    def _():
        gg_ref[...] = y


def _hg_proj(x, mods_l, gain, w5, lb):
    row = lambda i, j: (i, 0)
    dirs = lambda i, j: (jnp.clip(j - 1, 0, 1), i, 0)
    return pl.pallas_call(
        _hg_proj_kernel,
        grid=(R // TM, 5),
        in_specs=[
            pl.BlockSpec((TM, D), row),
            pl.BlockSpec((None, N_MOD, D), lambda i, j: (_cond_idx(i, TM), 0, 0)),
            pl.BlockSpec((1, D), lambda i, j: (0, 0)),
            pl.BlockSpec((None, D, D), lambda i, j: (j, 0, 0)),
            pl.BlockSpec((None, 1, D), lambda i, j: (jnp.clip(j - 1, 0, 1), 0, 0)),
        ],
        out_specs=[
            pl.BlockSpec((TM, D), row),
            pl.BlockSpec((None, TM, D), dirs),
            pl.BlockSpec((None, TM, D), dirs),
            pl.BlockSpec((TM, D), row),
            pl.BlockSpec((TM, D), row),
        ],
        out_shape=[
            jax.ShapeDtypeStruct((R, D), F32),
            jax.ShapeDtypeStruct((2, R, D), F32),
            jax.ShapeDtypeStruct((2, R, D), F32),
            jax.ShapeDtypeStruct((R, D), BF16),
            jax.ShapeDtypeStruct((R, D), F32),
        ],
        scratch_shapes=[pltpu.VMEM((TM, D), BF16)],
        compiler_params=_cparams(("arbitrary", "arbitrary")),
        name="hgrn_proj",
    )(x, mods_l, gain, w5, lb.reshape(2, 1, D))


LEVEL_HALVES = (64, 32, 16)


def _gla_kernel(q_ref, k_ref, lf_ref, v_ref, s0_ref, o_ref, sn_ref, st_scr, b_scr, *, nb):
    d = pl.program_id(1)
    n = pl.program_id(2)

    @pl.when(n == 0)
    def _():
        for h in range(HG_H):
            st_scr[h] = s0_ref[h].T

    row = lax.broadcasted_iota(I32, (BLK, BLK), 0)
    col = lax.broadcasted_iota(I32, (BLK, BLK), 1)
    ut = row + d * (BLK - 1 - 2 * row)
    us = col + d * (BLK - 1 - 2 * col)
    causal = us <= ut
    tri = jnp.where(causal, 1.0, 0.0).astype(BF16)

    hi, mid, lo = _split3(lf_ref[...])
    b_scr[...] = _dot(tri, hi) + _dot(tri, mid) + _dot(tri, lo)

    level_masks = []
    for hs in LEVEL_HALVES:
        sh = int(math.log2(2 * hs))
        same = (ut >> sh) == (us >> sh)
        level_masks.append(same & ((ut & (2 * hs - 1)) >= hs) & ((us & (2 * hs - 1)) < hs))
    base_mask = ((ut >> 4) == (us >> 4)) & causal

    def split_rows(sl, half):
        parts = []
        for jr in range(BLK // (2 * half)):
            r0 = jr * 2 * half + half - 1
            r = jnp.where(d == 0, b_scr[r0:r0 + 1, sl], b_scr[r0 + 1:r0 + 2, sl])
            parts.append(jnp.broadcast_to(r, (2 * half, HG_DK)))
        return parts[0] if len(parts) == 1 else jnp.concatenate(parts, axis=0)

    for h in range(HG_H):
        sl = slice(h * HG_DK, (h + 1) * HG_DK)
        bh = b_scr[:, sl]
        q = q_ref[:, sl]
        k = k_ref[:, sl]
        v = v_ref[:, sl]

        xq = bh - split_rows(sl, CHUNK // 2)
        att = jnp.where(base_mask, _dot_nt((q * jnp.exp(xq)).astype(BF16), (k * jnp.exp(-xq)).astype(BF16)), 0.0)
        for hs, msk in zip(LEVEL_HALVES, level_masks):
            e = jnp.exp(-jnp.abs(bh - split_rows(sl, hs)))
            att = jnp.where(msk, _dot_nt((q * e).astype(BF16), (k * e).astype(BF16)), att)

        b_last = jnp.where(d == 0, b_scr[BLK - 1:BLK, sl], b_scr[0:1, sl])
        q_in = (q * jnp.exp(bh)).astype(BF16)
        k_in = (k * jnp.exp(b_last - bh)).astype(BF16)
        st = st_scr[h]
        o_ref[:, sl] = _dot(att.astype(BF16), v) + _dot_nt(q_in, st.astype(BF16))
        v_t = v.astype(F32).T.astype(BF16)
        st_scr[h] = st * jnp.exp(b_last) + _dot(v_t, k_in)

    @pl.when(n == nb - 1)
    def _():
        for h in range(HG_H):
            sn_ref[h] = st_scr[h].T


def _gla(q, k, lf, v, s0, n_batch, t_len, row0):
    nb = t_len // BLK
    rb0 = row0 // BLK

    def rb(b, d, n):
        return rb0 + b * nb + n + d * (nb - 1 - 2 * n)

    return pl.pallas_call(
        functools.partial(_gla_kernel, nb=nb),
        grid=(n_batch, 2, nb),
        in_specs=[
            pl.BlockSpec((BLK, D), lambda b, d, n: (rb(b, d, n), 0)),
            pl.BlockSpec((None, BLK, D), lambda b, d, n: (d, rb(b, d, n), 0)),
            pl.BlockSpec((None, BLK, D), lambda b, d, n: (d, rb(b, d, n), 0)),
            pl.BlockSpec((BLK, D), lambda b, d, n: (rb(b, d, n), 0)),
            pl.BlockSpec((None, None, HG_H, HG_DK, HG_DK), lambda b, d, n: (b, d, 0, 0, 0)),
        ],
        out_specs=[
            pl.BlockSpec((None, BLK, D), lambda b, d, n: (d, rb(b, d, n) - rb0, 0)),
            pl.BlockSpec((None, None, HG_H, HG_DK, HG_DK), lambda b, d, n: (b, d, 0, 0, 0)),
        ],
        out_shape=[
            jax.ShapeDtypeStruct((2, n_batch * t_len, D), F32),
            jax.ShapeDtypeStruct((n_batch, 2, HG_H, HG_DK, HG_DK), F32),
        ],
        scratch_shapes=[pltpu.VMEM((HG_H, HG_DK, HG_DK), F32), pltpu.VMEM((BLK, D), F32)],
        compiler_params=_cparams(("arbitrary", "arbitrary", "arbitrary")),
        name="hgrn_recurrence",
    )(q, k, lf, v, s0)


def _hg_out_kernel(of_ref, ob_ref, gg_ref, on_ref, w_ref, x_ref, m_ref, o_ref):
    o = of_ref[...] + ob_ref[...]
    gain = on_ref[...]
    parts = []
    for h in range(HG_H):
        parts.append(_rms(o[:, h * HG_DK:(h + 1) * HG_DK], gain))
    a = (jnp.concatenate(parts, axis=1) * _silu(gg_ref[...])).astype(BF16)
    o_ref[...] = x_ref[...] + m_ref[...][2:3] * _dot(a, w_ref[...])


def _hg_out(o2, gg, o_norm, w_o, x, mods_l):
    fixed = lambda i: (0, 0)
    return pl.pallas_call(
        _hg_out_kernel,
        grid=(R // TM,),
        in_specs=[
            pl.BlockSpec((None, TM, D), lambda i: (0, i, 0)),
            pl.BlockSpec((None, TM, D), lambda i: (1, i, 0)),
            pl.BlockSpec((TM, D), lambda i: (i, 0)),
            pl.BlockSpec((1, HG_DK), fixed),
            pl.BlockSpec((D, D), fixed),
            pl.BlockSpec((TM, D), lambda i: (i, 0)),
            pl.BlockSpec((None, N_MOD, D), lambda i: (_cond_idx(i, TM), 0, 0)),
        ],
        out_specs=pl.BlockSpec((TM, D), lambda i: (i, 0)),
        out_shape=jax.ShapeDtypeStruct((R, D), F32),
        compiler_params=_cparams(("arbitrary",)),
        name="hgrn_out",
    )(o2, o2, gg, o_norm, w_o, x, mods_l)


def _router_kernel(x_ref, m_ref, g_ref, rt_ref, h8_ref, idx_ref, wt_ref):
    m = m_ref[...]
    h = _normmod(x_ref[...], g_ref[...], m[4:5], m[3:4])
    tm = h.shape[0]
    for s in range(ROW_TILE):
        h8_ref[pl.ds(s, tm, stride=ROW_TILE), :] = h[:, s * LANES:(s + 1) * LANES]
    h1, h2, h3 = _split3(h)
    r1, r2, r3 = _split3(rt_ref[...])
    lt = (_dot_nt(r1, h1) + _dot_nt(r1, h2) + _dot_nt(r2, h1)
          + _dot_nt(r1, h3) + _dot_nt(r3, h1) + _dot_nt(r2, h2))
    lg = lt[:N_EXP]
    e = jnp.exp(lg - jnp.max(lg, axis=0, keepdims=True))
    p = e / jnp.sum(e, axis=0, keepdims=True)
    io = lax.broadcasted_iota(I32, p.shape, 0)
    m1 = jnp.max(p, axis=0, keepdims=True)
    i1 = jnp.min(jnp.where(p == m1, io, N_EXP), axis=0, keepdims=True)
    p2 = jnp.where(io == i1, -1.0, p)
    m2 = jnp.max(p2, axis=0, keepdims=True)
    i2 = jnp.min(jnp.where(p2 == m2, io, N_EXP), axis=0, keepdims=True)
    den = m1 + m2
    idx_ref[...] = jnp.concatenate([i1, i2], axis=0)
    wt_ref[...] = jnp.concatenate([m1 / den, m2 / den], axis=0)


def _router(x, mods_l, gain, router_t):
    return pl.pallas_call(
        _router_kernel,
        grid=(R // TM,),
        in_specs=[
            pl.BlockSpec((TM, D), lambda i: (i, 0)),
            pl.BlockSpec((None, N_MOD, D), lambda i: (_cond_idx(i, TM), 0, 0)),
            pl.BlockSpec((1, D), lambda i: (0, 0)),
            pl.BlockSpec((2 * SUB, D), lambda i: (0, 0)),
        ],
        out_specs=[
            pl.BlockSpec((TM * ROW_TILE, LANES), lambda i: (i, 0)),
            pl.BlockSpec((TOP_K, TM), lambda i: (0, i)),
            pl.BlockSpec((TOP_K, TM), lambda i: (0, i)),
        ],
        out_shape=[
            jax.ShapeDtypeStruct((R * ROW_TILE, LANES), F32),
            jax.ShapeDtypeStruct((TOP_K, R), I32),
            jax.ShapeDtypeStruct((TOP_K, R), F32),
        ],
        compiler_params=_cparams(("arbitrary",)),
        name="moe_router",
    )(x, mods_l, gain, router_t)


def _row_dma_kernel(pos_ref, src_ref, *rest, scatter):
    dst_ref, sem = rest[-2], rest[-1]
    base = pl.program_id(0) * DMA_CHUNK

    def copy(j):
        pair = base + j
        slot = pos_ref[j]
        if scatter:
            src_row, dst_row = pair & (R - 1), slot
        else:
            src_row, dst_row = slot, pair
        return pltpu.make_async_copy(
            src_ref.at[pl.ds(pl.multiple_of(src_row * SUB, SUB), SUB)],
            dst_ref.at[pl.ds(pl.multiple_of(dst_row * SUB, SUB), SUB)],
            sem)

    def issue(j, carry):
        copy(j).start()
        return carry

    def drain(j, carry):
        copy(j).wait()
        return carry

    lax.fori_loop(0, DMA_CHUNK, issue, 0)
    lax.fori_loop(0, DMA_CHUNK, drain, 0)


def _dispatch(pos, h8):
    zeros = jnp.zeros((N_SLOT * ROW_TILE, LANES), F32)
    return pl.pallas_call(
        functools.partial(_row_dma_kernel, scatter=True),
        grid=(TOP_K * R // DMA_CHUNK,),
        in_specs=[
            pl.BlockSpec((DMA_CHUNK,), lambda c: (c,), memory_space=pltpu.SMEM),
            pl.BlockSpec(memory_space=pl.ANY),
            pl.BlockSpec(memory_space=pl.ANY),
        ],
        out_specs=pl.BlockSpec(memory_space=pl.ANY),
        out_shape=jax.ShapeDtypeStruct((N_SLOT * ROW_TILE, LANES), F32),
        scratch_shapes=[pltpu.SemaphoreType.DMA(())],
        input_output_aliases={2: 0},
        compiler_params=_cparams(("arbitrary",)),
        name="moe_dispatch",
    )(pos, h8, zeros)


def _combine_gather(pos, y8):
    return pl.pallas_call(
        functools.partial(_row_dma_kernel, scatter=False),
        grid=(TOP_K * R // DMA_CHUNK,),
        in_specs=[
            pl.BlockSpec((DMA_CHUNK,), lambda c: (c,), memory_space=pltpu.SMEM),
            pl.BlockSpec(memory_space=pl.ANY),
        ],
        out_specs=pl.BlockSpec(memory_space=pl.ANY),
        out_shape=jax.ShapeDtypeStruct((TOP_K * R * ROW_TILE, LANES), F32),
        scratch_shapes=[pltpu.SemaphoreType.DMA(())],
        compiler_params=_cparams(("arbitrary",)),
        name="moe_combine_gather",
    )(pos, y8)


EXP_CHUNK = 896
N_EXP_CHUNK = E_FF // EXP_CHUNK


def _expert_kernel(te_ref, tv_ref, x8_ref, w1_ref, w3_ref, w2_ref, y8_ref, xb_scr, acc_scr):
    i = pl.program_id(0)
    kc = pl.program_id(1)
    valid = tv_ref[i] == 1

    @pl.when(valid & (kc == 0))
    def _():
        for s in range(ROW_TILE):
            xb_scr[:, s * LANES:(s + 1) * LANES] = x8_ref[pl.ds(s, TM_MOE, stride=ROW_TILE), :].astype(BF16)
        acc_scr[...] = jnp.zeros_like(acc_scr)

    @pl.when(valid)
    def _():
        x = xb_scr[...]
        a = _dot(x, w1_ref[...])
        b = _dot(x, w3_ref[...])
        acc_scr[...] += _dot((_silu(a) * b).astype(BF16), w2_ref[...])

    @pl.when(valid & (kc == N_EXP_CHUNK - 1))
    def _():
        for s in range(ROW_TILE):
            y8_ref[pl.ds(s, TM_MOE, stride=ROW_TILE), :] = acc_scr[:, s * LANES:(s + 1) * LANES]

    @pl.when(jnp.logical_not(valid) & (kc == N_EXP_CHUNK - 1))
    def _():
        y8_ref[...] = jnp.zeros_like(y8_ref)


def _experts(tile_expert, tile_valid, hs8, w1, w3, w2):
    def kc_eff(kc, tv, i):
        return jnp.where(tv[i] == 1, kc, N_EXP_CHUNK - 1)

    grid_spec = pltpu.PrefetchScalarGridSpec(
        num_scalar_prefetch=2,
        grid=(N_TILE, N_EXP_CHUNK),
        in_specs=[
            pl.BlockSpec((TM_MOE * ROW_TILE, LANES), lambda i, kc, te, tv: (i, 0)),
            pl.BlockSpec((None, D, EXP_CHUNK), lambda i, kc, te, tv: (te[i], 0, kc_eff(kc, tv, i))),
            pl.BlockSpec((None, D, EXP_CHUNK), lambda i, kc, te, tv: (te[i], 0, kc_eff(kc, tv, i))),
            pl.BlockSpec((None, EXP_CHUNK, D), lambda i, kc, te, tv: (te[i], kc_eff(kc, tv, i), 0)),
        ],
        out_specs=pl.BlockSpec((TM_MOE * ROW_TILE, LANES), lambda i, kc, te, tv: (i, 0)),
        scratch_shapes=[pltpu.VMEM((TM_MOE, D), BF16), pltpu.VMEM((TM_MOE, D), F32)],
    )
    return pl.pallas_call(
        _expert_kernel,
        grid_spec=grid_spec,
        out_shape=jax.ShapeDtypeStruct((N_SLOT * ROW_TILE, LANES), F32),
        compiler_params=_cparams(("arbitrary", "arbitrary"), 48),
        name="moe_experts",
    )(tile_expert, tile_valid, hs8, w1, w3, w2)


def _moe_resid_kernel(y0_ref, y1_ref, wc_ref, x_ref, m_ref, fn_ref, o_ref, *, final):
    tm = x_ref.shape[0]

    def rows(ref):
        return jnp.concatenate([ref[pl.ds(s, tm, stride=ROW_TILE), :] for s in range(ROW_TILE)], axis=1)

    w = wc_ref[...]
    f = w[:, 0:1] * rows(y0_ref) + w[:, 1:2] * rows(y1_ref)
    xn = x_ref[...] + m_ref[...][5:6] * f
    if final:
        xn = _rms(xn, fn_ref[...])
    o_ref[...] = xn


def _moe_resid(yk8, wcol, x, mods_l, final_gain, final):
    nt = R // TM
    return pl.pallas_call(
        functools.partial(_moe_resid_kernel, final=final),
        grid=(nt,),
        in_specs=[
            pl.BlockSpec((TM * ROW_TILE, LANES), lambda i: (i, 0)),
            pl.BlockSpec((TM * ROW_TILE, LANES), lambda i: (nt + i, 0)),
            pl.BlockSpec((TM, TOP_K), lambda i: (i, 0)),
            pl.BlockSpec((TM, D), lambda i: (i, 0)),
            pl.BlockSpec((None, N_MOD, D), lambda i: (_cond_idx(i, TM), 0, 0)),
            pl.BlockSpec((1, D), lambda i: (0, 0)),
        ],
        out_specs=pl.BlockSpec((TM, D), lambda i: (i, 0)),
        out_shape=jax.ShapeDtypeStruct((R, D), F32),
        compiler_params=_cparams(("arbitrary",)),
        name="moe_resid",
    )(yk8, yk8, wcol, x, mods_l, final_gain)


def _route_tables(idx):
    e = idx.reshape(-1)
    onehot = (e[:, None] == jnp.arange(N_EXP, dtype=I32)[None, :]).astype(I32)
    csum = jnp.cumsum(onehot, axis=0)
    rank = jnp.take_along_axis(csum, e[:, None], axis=1)[:, 0] - 1
    counts = csum[-1]
    padded = ((counts + TM_MOE - 1) // TM_MOE) * TM_MOE
    gend = jnp.cumsum(padded)
    pos = (gend - padded)[e] + rank
    tile_start = jnp.arange(N_TILE, dtype=I32) * TM_MOE
    te = jnp.sum((tile_start[:, None] >= gend[None, :]).astype(I32), axis=1)
    valid = tile_start < gend[-1]
    te_last = te[gend[-1] // TM_MOE - 1]
    te = jnp.minimum(jnp.where(valid, te, te_last), N_EXP - 1)
    return pos.astype(I32), te.astype(I32), valid.astype(I32)


def _rot_half(w):
    wa = w.reshape(w.shape[:-1] + (2, 2, ROPE_AXIS // 2))
    return jnp.stack([-wa[..., 1, :], wa[..., 0, :]], axis=-2).reshape(w.shape)


def _rope_tables(tm):
    rows = DEC_SEQ // GRID_W
    r = jnp.repeat(jnp.arange(rows), GRID_W).astype(F32)
    c = jnp.tile(jnp.arange(GRID_W), rows).astype(F32)
    inv = ROPE_THETA ** (-jnp.arange(0, ROPE_AXIS, 2, dtype=F32) / ROPE_AXIS)
    ang_r = r[:, None] * inv
    ang_c = c[:, None] * inv
    ang = jnp.concatenate([ang_r, ang_r, ang_c, ang_c], axis=-1)
    cos, sin = jnp.cos(ang), jnp.sin(ang)
    z32 = jnp.zeros_like(cos)
    one = jnp.ones((tm, QK_ROPE), F32)
    zt = jnp.zeros((tm, QK_ROPE), F32)
    tq_c = jnp.concatenate([jnp.concatenate([cos, cos, z32, z32], 1), jnp.concatenate([one, one, zt, zt], 1)], 0)
    tq_s = jnp.concatenate([jnp.concatenate([sin, sin, z32, z32], 1), jnp.zeros((tm, LANES), F32)], 0)
    tk_c = jnp.concatenate([jnp.concatenate([cos, z32, z32, z32], 1), jnp.concatenate([one, zt, zt, zt], 1)], 0)
    tk_s = jnp.concatenate([jnp.concatenate([sin, z32, z32, z32], 1), jnp.zeros((tm, LANES), F32)], 0)
    return tq_c, tq_s, tk_c, tk_s


def _mla_weights(w_dq, w_uq, w_dkv, w_uk, w_uv, w_o):
    kr_w = w_dkv[:, KV_LORA:]
    w_a = jnp.concatenate(
        [w_dq, w_dkv[:, :KV_LORA], kr_w, _rot_half(kr_w), jnp.zeros((D, LANES - 2 * QK_ROPE), F32)], axis=1)
    uq = w_uq.reshape(Q_LORA, N_PAIR, 2, QK_NOPE + QK_ROPE)
    nope = uq[..., :QK_NOPE].reshape(Q_LORA, N_PAIR, 2 * QK_NOPE)
    rope = uq[..., QK_NOPE:]
    w_uq2 = jnp.concatenate(
        [nope, rope.reshape(Q_LORA, N_PAIR, 2 * QK_ROPE), _rot_half(rope).reshape(Q_LORA, N_PAIR, 2 * QK_ROPE)],
        axis=-1).reshape(Q_LORA, N_PAIR * 2 * LANES)
    uk = w_uk.reshape(KV_LORA, N_PAIR, 2 * QK_NOPE)
    wk = jnp.concatenate([uk, jnp.zeros((KV_LORA, N_PAIR, LANES), F32)], axis=-1).reshape(KV_LORA, N_PAIR * 2 * LANES)
    eye = jnp.eye(LANES, QK_ROPE, dtype=F32)
    pair = jnp.concatenate([jnp.zeros((LANES, LANES), F32), eye, eye, jnp.zeros((LANES, LANES - 2 * QK_ROPE), F32)], 1)
    wkr = jnp.tile(pair, (1, N_PAIR))
    wv = w_uv.reshape(KV_LORA, HEADS * V_HEAD)
    wo = w_o.reshape(HEADS * V_HEAD, D)
    return [w.astype(BF16) for w in (w_a, w_uq2, wk, wkr, wv, wo)]


def kernel(x_prompt, x_sample, cache_ckv, cache_krope, state_hgrn, c, c_ctx, w_ada, b_ada, norm_mix, norm_ffn, mla_w_dq, mla_q_norm, mla_w_uq, mla_w_dkv, mla_kv_norm, mla_w_uk, mla_w_uv, mla_w_o, hg_w_q, hg_w_f, hg_w_i, hg_w_g, hg_lb_logits, hg_o_norm, hg_w_o, ffn_w1, ffn_w3, ffn_w2, moe_router, moe_w1, moe_w3, moe_w2, final_norm):
    x = jnp.concatenate([x_prompt.reshape(R_P, D), x_sample.reshape(R_S, D)], axis=0)
    cond = jnp.concatenate([c, c_ctx[None], jnp.zeros((COND_PAD - N_COND, D), F32)], axis=0)
    mods = _ada(cond, w_ada, b_ada)
    lb_all = _lower_bounds(hg_lb_logits)
    tq_c, tq_s, tk_c, tk_s = _rope_tables(TM)
    new_ckv, new_krope, new_hgrn = [], [], []

    for l in range(DEPTH):
        j = l // 2
        mods_l = mods[l]
        gain_mix = norm_mix[l][None]
        gain_ffn = norm_ffn[l][None]
        if l % 2 == 0:
            w_a, w_uq2, wk, wkr, wv, wo = _mla_weights(
                mla_w_dq[j], mla_w_uq[j], mla_w_dkv[j], mla_w_uk[j], mla_w_uv[j], mla_w_o[j])
            cq, ckv, kr = _mla_proj(x, mods_l, gain_mix, w_a, mla_q_norm[j][None], mla_kv_norm[j][None], tk_c, tk_s)
            q2 = _q_up(cq, w_uq2, tq_c, tq_s)
            k2_p, v_p = _kv_up(ckv, kr, R_P, wk, wkr, wv)
            ckv_s = jnp.concatenate([cache_ckv[:, j], ckv[R_P:].reshape(DEC_BATCH, DEC_SEQ, KV_LORA)], axis=1)
            kr_cache = jnp.pad(cache_krope[:, j], ((0, 0), (0, 0), (0, LANES - QK_ROPE)))
            kr_s = jnp.concatenate([kr_cache, kr[R_P:].reshape(DEC_BATCH, DEC_SEQ, LANES)], axis=1)
            s_all = PAST + DEC_SEQ
            k2_s, v_s = _kv_up(ckv_s.reshape(DEC_BATCH * s_all, KV_LORA), kr_s.reshape(DEC_BATCH * s_all, LANES),
                               DEC_BATCH * s_all, wk, wkr, wv)
            o_p = _attention(q2, k2_p, v_p, BATCH, SEQ, SEQ, 0, SEQ)
            o_s = _attention(q2, k2_s, v_s, DEC_BATCH, DEC_SEQ, s_all, R_P, 512)
            x = _mm_resid(jnp.concatenate([o_p, o_s], axis=0), wo, x, mods_l, 2)
            new_ckv.append(ckv[:R_P].reshape(BATCH, SEQ, KV_LORA))
            new_krope.append(kr[:R_P, :QK_ROPE].reshape(BATCH, SEQ, QK_ROPE))
        else:
            w5 = jnp.stack([hg_w_q[j], hg_w_f[j, 0], hg_w_f[j, 1], hg_w_i[j], hg_w_g[j]]).astype(BF16)
            q, k, lf, v, gg = _hg_proj(x, mods_l, gain_mix, w5, lb_all[j])
            s_zero = jnp.zeros((BATCH, 2, HG_H, HG_DK, HG_DK), F32)
            o_p, st_p = _gla(q, k, lf, v, s_zero, BATCH, SEQ, 0)
            o_s, _ = _gla(q, k, lf, v, state_hgrn[:, j], DEC_BATCH, DEC_SEQ, R_P)
            x = _hg_out(jnp.concatenate([o_p, o_s], axis=1), gg, hg_o_norm[j][None], hg_w_o[j].astype(BF16), x, mods_l)
            new_hgrn.append(st_p)
        if l % 2 == 0:
            x = _ffn(x, mods_l, gain_ffn, ffn_w1[j].astype(BF16), ffn_w3[j].astype(BF16), ffn_w2[j].astype(BF16))
        else:
            router_t = jnp.concatenate([moe_router[j].T, jnp.zeros((2 * SUB - N_EXP, D), F32)], axis=0)
            h8, idx, wts = _router(x, mods_l, gain_ffn, router_t)
            pos, tile_expert, tile_valid = _route_tables(idx)
            hs8 = _dispatch(pos, h8)
            y8 = _experts(tile_expert, tile_valid, hs8,
                          moe_w1[j].astype(BF16), moe_w3[j].astype(BF16), moe_w2[j].astype(BF16))
            yk8 = _combine_gather(pos, y8)
            x = _moe_resid(yk8, wts.T, x, mods_l, final_norm[None], final=(l == DEPTH - 1))

    y_prompt = x[:R_P].reshape(BATCH, SEQ, D)
    y_sample = x[R_P:].reshape(DEC_BATCH, DEC_SEQ, D)
    return (y_prompt, y_sample, jnp.stack(new_ckv, axis=1), jnp.stack(new_krope, axis=1),
            jnp.stack(new_hgrn, axis=1))
```

```python
import functools
import math

import jax
import jax.numpy as jnp
from jax import lax
from jax.experimental import pallas as pl
from jax.experimental.pallas import tpu as pltpu

F32 = jnp.float32
BF16 = jnp.bfloat16
I32 = jnp.int32

D = 1024
BATCH, SEQ = 32, 256
DEC_BATCH, DEC_SEQ = 8, 1024
PAST = 256
DEPTH = 4
R_P = BATCH * SEQ
R_S = DEC_BATCH * DEC_SEQ
R = R_P + R_S
N_COND = DEC_BATCH + 1
COND_PAD = 16
N_MOD = 6
EPS = 1e-6
GRID_W = 64
HEADS, QK_NOPE, QK_ROPE, V_HEAD = 16, 64, 32, 64
Q_LORA, KV_LORA = 512, 256
N_PAIR = HEADS // 2
ROPE_AXIS = QK_ROPE // 2
ROPE_THETA = 10000.0
HG_H, HG_DK = 8, 128
BLK = 128
CHUNK = 16
D_FF = 2816
N_EXP, TOP_K, E_FF = 8, 2, 3584
LANES = 128
SUB = 8
ROW_TILE = D // LANES
TM = 512
TM_MOE = 512
N_SLOT = TOP_K * R + N_EXP * TM_MOE
N_TILE = N_SLOT // TM_MOE
MIB = 1024 * 1024
assert DEPTH % 2 == 0


def _cparams(sem, vmem_mib=40):
    return pltpu.CompilerParams(dimension_semantics=sem, vmem_limit_bytes=vmem_mib * MIB)


def _cond_idx(i, tm):
    r0 = i * tm
    return jnp.where(r0 < R_P, DEC_BATCH, (r0 - R_P) // DEC_SEQ)


def _rope_idx(i, tm):
    nb = DEC_SEQ // tm
    r0 = i * tm
    return jnp.where(r0 < R_P, nb, ((r0 - R_P) // tm) % nb)


def _normmod(x, gain, scale, shift):
    ms = jnp.mean(x * x, axis=-1, keepdims=True)
    return (x * lax.rsqrt(ms + EPS)) * gain * (1.0 + scale) + shift


def _rms(x, gain):
    ms = jnp.mean(x * x, axis=-1, keepdims=True)
    return (x * lax.rsqrt(ms + EPS)) * gain


def _silu(x):
    return x / (1.0 + jnp.exp(-x))


def _dot(a, b):
    return jnp.dot(a, b, preferred_element_type=F32)


def _dot_nt(a, b):
    return lax.dot_general(a, b, (((1,), (1,)), ((), ())), preferred_element_type=F32)


def _split3(x):
    hi = x.astype(BF16)
    r1 = x - hi.astype(F32)
    mid = r1.astype(BF16)
    lo = (r1 - mid.astype(F32)).astype(BF16)
    return hi, mid, lo


def _ada_kernel(c_ref, w_ref, b_ref, o_ref):
    o_ref[...] = _dot(_silu(c_ref[...]), w_ref[...]) + b_ref[...]


def _ada(cond, w_ada, b_ada):
    tn = 1536
    out = pl.pallas_call(
        _ada_kernel,
        grid=(DEPTH, N_MOD * D // tn),
        in_specs=[
            pl.BlockSpec((COND_PAD, D), lambda l, j: (0, 0)),
            pl.BlockSpec((None, D, tn), lambda l, j: (l, 0, j)),
            pl.BlockSpec((None, 1, tn), lambda l, j: (l, 0, j)),
        ],
        out_specs=pl.BlockSpec((None, COND_PAD, tn), lambda l, j: (l, 0, j)),
        out_shape=jax.ShapeDtypeStruct((DEPTH, COND_PAD, N_MOD * D), F32),
        compiler_params=_cparams(("arbitrary", "arbitrary")),
        name="ada",
    )(cond, w_ada, b_ada.reshape(DEPTH, 1, N_MOD * D))
    return out[:, :N_COND].reshape(DEPTH, N_COND, N_MOD, D)


def _lb_kernel(x_ref, o_ref):
    x = x_ref[...]
    e = jnp.exp(x - jnp.max(x, axis=0, keepdims=True))
    sm = e / jnp.sum(e, axis=0, keepdims=True)
    n = x.shape[0]
    cum = sm[0]
    o_ref[0] = jnp.zeros_like(cum)
    for i in range(1, n):
        cum_i = cum + sm[i]
        o_ref[i] = cum_i - sm[0]
        cum = cum_i


def _lower_bounds(lb_logits):
    return pl.pallas_call(
        _lb_kernel,
        out_shape=jax.ShapeDtypeStruct(lb_logits.shape, F32),
        name="hgrn_lower_bounds",
    )(lb_logits)


def _mla_proj_kernel(x_ref, m_ref, g_ref, w_ref, qn_ref, kvn_ref, tc_ref, ts_ref,
                     cq_ref, ckv_ref, kr_ref):
    m = m_ref[...]
    h = _normmod(x_ref[...], g_ref[...], m[1:2], m[0:1]).astype(BF16)
    y = _dot(h, w_ref[...])
    cq_ref[...] = _rms(y[:, :Q_LORA], qn_ref[...]).astype(BF16)
    ckv_ref[...] = _rms(y[:, Q_LORA:Q_LORA + KV_LORA], kvn_ref[...])
    slab = y[:, Q_LORA + KV_LORA:]
    kr_ref[...] = slab * tc_ref[...] + pltpu.roll(slab, LANES - QK_ROPE, 1) * ts_ref[...]


def _mla_proj(x, mods_l, gain, w_a, q_norm, kv_norm, tk_c, tk_s):
    n_a = w_a.shape[1]
    row = lambda i: (i, 0)
    fixed = lambda i: (0, 0)
    return pl.pallas_call(
        _mla_proj_kernel,
        grid=(R // TM,),
        in_specs=[
            pl.BlockSpec((TM, D), row),
            pl.BlockSpec((None, N_MOD, D), lambda i: (_cond_idx(i, TM), 0, 0)),
            pl.BlockSpec((1, D), fixed),
            pl.BlockSpec((D, n_a), fixed),
            pl.BlockSpec((1, Q_LORA), fixed),
            pl.BlockSpec((1, KV_LORA), fixed),
            pl.BlockSpec((TM, LANES), lambda i: (_rope_idx(i, TM), 0)),
            pl.BlockSpec((TM, LANES), lambda i: (_rope_idx(i, TM), 0)),
        ],
        out_specs=[
            pl.BlockSpec((TM, Q_LORA), row),
            pl.BlockSpec((TM, KV_LORA), row),
            pl.BlockSpec((TM, LANES), row),
        ],
        out_shape=[
            jax.ShapeDtypeStruct((R, Q_LORA), BF16),
            jax.ShapeDtypeStruct((R, KV_LORA), F32),
            jax.ShapeDtypeStruct((R, LANES), F32),
        ],
        compiler_params=_cparams(("arbitrary",)),
        name="mla_proj",
    )(x, mods_l, gain, w_a, q_norm, kv_norm, tk_c, tk_s)


def _q_up_kernel(cq_ref, w_ref, tc_ref, ts_ref, q_ref):
    y = _dot(cq_ref[...], w_ref[...])
    tc = tc_ref[...]
    ts = ts_ref[...]
    for p in range(N_PAIR):
        lo = p * 2 * LANES
        q_ref[:, lo:lo + LANES] = y[:, lo:lo + LANES].astype(BF16)
        hi = y[:, lo + LANES:lo + 2 * LANES]
        q_ref[:, lo + LANES:lo + 2 * LANES] = (hi * tc + pltpu.roll(hi, LANES // 2, 1) * ts).astype(BF16)


def _q_up(cq, w_uq2, tq_c, tq_s):
    nq = w_uq2.shape[1]
    return pl.pallas_call(
        _q_up_kernel,
        grid=(R // TM,),
        in_specs=[
            pl.BlockSpec((TM, Q_LORA), lambda i: (i, 0)),
            pl.BlockSpec((Q_LORA, nq), lambda i: (0, 0)),
            pl.BlockSpec((TM, LANES), lambda i: (_rope_idx(i, TM), 0)),
            pl.BlockSpec((TM, LANES), lambda i: (_rope_idx(i, TM), 0)),
        ],
        out_specs=pl.BlockSpec((TM, nq), lambda i: (i, 0)),
        out_shape=jax.ShapeDtypeStruct((R, nq), BF16),
        compiler_params=_cparams(("arbitrary",)),
        name="mla_q_up",
    )(cq, w_uq2, tq_c, tq_s)


def _kv_up_kernel(c_ref, r_ref, wk_ref, wkr_ref, wv_ref, k_ref, v_ref):
    c = c_ref[...].astype(BF16)
    r = r_ref[...].astype(BF16)
    k_ref[...] = (_dot(c, wk_ref[...]) + _dot(r, wkr_ref[...])).astype(BF16)
    v_ref[...] = _dot(c, wv_ref[...]).astype(BF16)


def _kv_up(ckv, kr, n_rows, wk, wkr, wv):
    nk = wk.shape[1]
    nv = wv.shape[1]
    fixed = lambda i: (0, 0)
    return pl.pallas_call(
        _kv_up_kernel,
        grid=(n_rows // TM,),
        in_specs=[
            pl.BlockSpec((TM, KV_LORA), lambda i: (i, 0)),
            pl.BlockSpec((TM, LANES), lambda i: (i, 0)),
            pl.BlockSpec((KV_LORA, nk), fixed),
            pl.BlockSpec((LANES, nk), fixed),
            pl.BlockSpec((KV_LORA, nv), fixed),
        ],
        out_specs=[
            pl.BlockSpec((TM, nk), lambda i: (i, 0)),
            pl.BlockSpec((TM, nv), lambda i: (i, 0)),
        ],
        out_shape=[
            jax.ShapeDtypeStruct((n_rows, nk), BF16),
            jax.ShapeDtypeStruct((n_rows, nv), BF16),
        ],
        compiler_params=_cparams(("arbitrary",)),
        name="mla_kv_up",
    )(ckv, kr, wk, wkr, wv)


def _attn_kernel(q_ref, k_ref, v_ref, o_ref):
    q = q_ref[...]
    k = k_ref[...]
    v = v_ref[...]
    scale = 1.0 / math.sqrt(QK_NOPE + QK_ROPE)
    lq = lax.broadcasted_iota(I32, (1, 2 * LANES), 1)
    lv = lax.broadcasted_iota(I32, (1, LANES), 1)
    sel_a = (lq < QK_NOPE) | ((lq >= LANES) & (lq < LANES + QK_ROPE))
    sel_b = ((lq >= QK_NOPE) & (lq < LANES)) | ((lq >= LANES + QK_ROPE) & (lq < LANES + 2 * QK_ROPE))
    zq = jnp.zeros_like(q)
    zv = jnp.zeros_like(v)
    out = None
    for sel, vsel in ((sel_a, lv < V_HEAD), (sel_b, lv >= V_HEAD)):
        s = _dot_nt(jnp.where(sel, q, zq), k) * scale
        e = jnp.exp(s - jnp.max(s, axis=-1, keepdims=True))
        den = jnp.sum(e, axis=-1, keepdims=True)
        o = _dot(e.astype(BF16), jnp.where(vsel, v, zv)) / den
        out = o if out is None else out + o
    o_ref[...] = out.astype(BF16)


def _attention(q2, k2, v, n_batch, t_len, s_len, q_row0, tq):
    nq = t_len // tq
    qb0 = q_row0 // tq
    return pl.pallas_call(
        _attn_kernel,
        grid=(n_batch, N_PAIR, nq),
        in_specs=[
            pl.BlockSpec((tq, 2 * LANES), lambda b, p, i: (qb0 + b * nq + i, p)),
            pl.BlockSpec((s_len, 2 * LANES), lambda b, p, i: (b, p)),
            pl.BlockSpec((s_len, LANES), lambda b, p, i: (b, p)),
        ],
        out_specs=pl.BlockSpec((tq, LANES), lambda b, p, i: (b * nq + i, p)),
        out_shape=jax.ShapeDtypeStruct((n_batch * t_len, HEADS * V_HEAD), BF16),
        compiler_params=_cparams(("arbitrary", "arbitrary", "arbitrary")),
        name="mla_attention",
    )(q2, k2, v)


def _mm_resid_kernel(a_ref, w_ref, x_ref, m_ref, o_ref, *, gate_idx):
    gate = m_ref[...][gate_idx:gate_idx + 1]
    o_ref[...] = x_ref[...] + gate * _dot(a_ref[...], w_ref[...])


def _mm_resid(a, w, x, mods_l, gate_idx):
    k = a.shape[1]
    return pl.pallas_call(
        functools.partial(_mm_resid_kernel, gate_idx=gate_idx),
        grid=(R // TM,),
        in_specs=[
            pl.BlockSpec((TM, k), lambda i: (i, 0)),
            pl.BlockSpec((k, D), lambda i: (0, 0)),
            pl.BlockSpec((TM, D), lambda i: (i, 0)),
            pl.BlockSpec((None, N_MOD, D), lambda i: (_cond_idx(i, TM), 0, 0)),
        ],
        out_specs=pl.BlockSpec((TM, D), lambda i: (i, 0)),
        out_shape=jax.ShapeDtypeStruct((R, D), F32),
        compiler_params=_cparams(("arbitrary",)),
        name="mm_resid",
    )(a, w, x, mods_l)


FFN_CHUNK = 1408


def _ffn_kernel(x_ref, m_ref, g_ref, w1_ref, w3_ref, w2_ref, o_ref):
    x = x_ref[...]
    m = m_ref[...]
    h = _normmod(x, g_ref[...], m[4:5], m[3:4]).astype(BF16)
    acc = jnp.zeros(x.shape, F32)
    for c in range(D_FF // FFN_CHUNK):
        sl = slice(c * FFN_CHUNK, (c + 1) * FFN_CHUNK)
        a = _dot(h, w1_ref[:, sl])
        b = _dot(h, w3_ref[:, sl])
        acc = acc + _dot((_silu(a) * b).astype(BF16), w2_ref[sl, :])
    o_ref[...] = x + m[5:6] * acc


def _ffn(x, mods_l, gain, w1, w3, w2):
    tm = 256
    fixed = lambda i: (0, 0)
    return pl.pallas_call(
        _ffn_kernel,
        grid=(R // tm,),
        in_specs=[
            pl.BlockSpec((tm, D), lambda i: (i, 0)),
            pl.BlockSpec((None, N_MOD, D), lambda i: (_cond_idx(i, tm), 0, 0)),
            pl.BlockSpec((1, D), fixed),
            pl.BlockSpec((D, D_FF), fixed),
            pl.BlockSpec((D, D_FF), fixed),
            pl.BlockSpec((D_FF, D), fixed),
        ],
        out_specs=pl.BlockSpec((tm, D), lambda i: (i, 0)),
        out_shape=jax.ShapeDtypeStruct((R, D), F32),
        compiler_params=_cparams(("arbitrary",), 56),
        name="dense_swiglu",
    )(x, mods_l, gain, w1, w3, w2)


def _forget_gate(z, lb):
    e = jnp.exp(-jnp.abs(z))
    log_sig = jnp.minimum(z, 0.0) - jnp.log1p(e)
    a = jnp.log(lb)
    b = jnp.log1p(-lb) + log_sig
    log_f = jnp.maximum(a, b) + jnp.log1p(jnp.exp(-jnp.abs(a - b)))
    k = (1.0 - lb) * (jnp.where(z >= 0, e, 1.0) / (1.0 + e))
    return k, log_f


def _hg_proj_kernel(x_ref, m_ref, g_ref, w_ref, lb_ref, q_ref, k_ref, lf_ref, v_ref, gg_ref, h_scr):
    j = pl.program_id(1)

    @pl.when(j == 0)
    def _():
        m = m_ref[...]
        h_scr[...] = _normmod(x_ref[...], g_ref[...], m[1:2], m[0:1]).astype(BF16)

    y = _dot(h_scr[...], w_ref[...])

    @pl.when(j == 0)
    def _():
        q_ref[...] = y

    @pl.when((j == 1) | (j == 2))
    def _():
        k, log_f = _forget_gate(y, lb_ref[...])
        k_ref[...] = k
        lf_ref[...] = log_f

    @pl.when(j == 3)
    def _():
        v_ref[...] = y.astype(BF16)

    @pl.when(j == 4)
    def _():
        gg_ref[...] = y


def _hg_proj(x, mods_l, gain, w5, lb):
    row = lambda i, j: (i, 0)
    dirs = lambda i, j: (jnp.clip(j - 1, 0, 1), i, 0)
    return pl.pallas_call(
        _hg_proj_kernel,
        grid=(R // TM, 5),
        in_specs=[
            pl.BlockSpec((TM, D), row),
            pl.BlockSpec((None, N_MOD, D), lambda i, j: (_cond_idx(i, TM), 0, 0)),
            pl.BlockSpec((1, D), lambda i, j: (0, 0)),
            pl.BlockSpec((None, D, D), lambda i, j: (j, 0, 0)),
            pl.BlockSpec((None, 1, D), lambda i, j: (jnp.clip(j - 1, 0, 1), 0, 0)),
        ],
        out_specs=[
            pl.BlockSpec((TM, D), row),
            pl.BlockSpec((None, TM, D), dirs),
            pl.BlockSpec((None, TM, D), dirs),
            pl.BlockSpec((TM, D), row),
            pl.BlockSpec((TM, D), row),
        ],
        out_shape=[
            jax.ShapeDtypeStruct((R, D), F32),
            jax.ShapeDtypeStruct((2, R, D), F32),
            jax.ShapeDtypeStruct((2, R, D), F32),
            jax.ShapeDtypeStruct((R, D), BF16),
            jax.ShapeDtypeStruct((R, D), F32),
        ],
        scratch_shapes=[pltpu.VMEM((TM, D), BF16)],
        compiler_params=_cparams(("arbitrary", "arbitrary")),
        name="hgrn_proj",
    )(x, mods_l, gain, w5, lb.reshape(2, 1, D))


LEVEL_HALVES = (64, 32, 16)


def _gla_kernel(q_ref, k_ref, lf_ref, v_ref, s0_ref, o_ref, sn_ref, st_scr, b_scr, *, nb):
    d = pl.program_id(1)
    n = pl.program_id(2)

    @pl.when(n == 0)
    def _():
        for h in range(HG_H):
            st_scr[h] = s0_ref[h].T

    row = lax.broadcasted_iota(I32, (BLK, BLK), 0)
    col = lax.broadcasted_iota(I32, (BLK, BLK), 1)
    ut = row + d * (BLK - 1 - 2 * row)
    us = col + d * (BLK - 1 - 2 * col)
    causal = us <= ut
    tri = jnp.where(causal, 1.0, 0.0).astype(BF16)

    hi, mid, lo = _split3(lf_ref[...])
    b_scr[...] = _dot(tri, hi) + _dot(tri, mid) + _dot(tri, lo)

    level_masks = []
    for hs in LEVEL_HALVES:
        sh = int(math.log2(2 * hs))
        same = (ut >> sh) == (us >> sh)
        level_masks.append(same & ((ut & (2 * hs - 1)) >= hs) & ((us & (2 * hs - 1)) < hs))
    base_mask = ((ut >> 4) == (us >> 4)) & causal

    def split_rows(sl, half):
        parts = []
        for jr in range(BLK // (2 * half)):
            r0 = jr * 2 * half + half - 1
            r = jnp.where(d == 0, b_scr[r0:r0 + 1, sl], b_scr[r0 + 1:r0 + 2, sl])
            parts.append(jnp.broadcast_to(r, (2 * half, HG_DK)))
        return parts[0] if len(parts) == 1 else jnp.concatenate(parts, axis=0)

    for h in range(HG_H):
        sl = slice(h * HG_DK, (h + 1) * HG_DK)
        bh = b_scr[:, sl]
        q = q_ref[:, sl]
        k = k_ref[:, sl]
        v = v_ref[:, sl]

        xq = bh - split_rows(sl, CHUNK // 2)
        att = jnp.where(base_mask, _dot_nt((q * jnp.exp(xq)).astype(BF16), (k * jnp.exp(-xq)).astype(BF16)), 0.0)
        for hs, msk in zip(LEVEL_HALVES, level_masks):
            e = jnp.exp(-jnp.abs(bh - split_rows(sl, hs)))
            att = jnp.where(msk, _dot_nt((q * e).astype(BF16), (k * e).astype(BF16)), att)

        b_last = jnp.where(d == 0, b_scr[BLK - 1:BLK, sl], b_scr[0:1, sl])
        q_in = (q * jnp.exp(bh)).astype(BF16)
        k_in = (k * jnp.exp(b_last - bh)).astype(BF16)
        st = st_scr[h]
        o_ref[:, sl] = _dot(att.astype(BF16), v) + _dot_nt(q_in, st.astype(BF16))
        v_t = v.astype(F32).T.astype(BF16)
        st_scr[h] = st * jnp.exp(b_last) + _dot(v_t, k_in)

    @pl.when(n == nb - 1)
    def _():
        for h in range(HG_H):
            sn_ref[h] = st_scr[h].T


def _gla(q, k, lf, v, s0, n_batch, t_len, row0):
    nb = t_len // BLK
    rb0 = row0 // BLK

    def rb(b, d, n):
        return rb0 + b * nb + n + d * (nb - 1 - 2 * n)

    return pl.pallas_call(
        functools.partial(_gla_kernel, nb=nb),
        grid=(n_batch, 2, nb),
        in_specs=[
            pl.BlockSpec((BLK, D), lambda b, d, n: (rb(b, d, n), 0)),
            pl.BlockSpec((None, BLK, D), lambda b, d, n: (d, rb(b, d, n), 0)),
            pl.BlockSpec((None, BLK, D), lambda b, d, n: (d, rb(b, d, n), 0)),
            pl.BlockSpec((BLK, D), lambda b, d, n: (rb(b, d, n), 0)),
            pl.BlockSpec((None, None, HG_H, HG_DK, HG_DK), lambda b, d, n: (b, d, 0, 0, 0)),
        ],
        out_specs=[
            pl.BlockSpec((None, BLK, D), lambda b, d, n: (d, rb(b, d, n) - rb0, 0)),
            pl.BlockSpec((None, None, HG_H, HG_DK, HG_DK), lambda b, d, n: (b, d, 0, 0, 0)),
        ],
        out_shape=[
            jax.ShapeDtypeStruct((2, n_batch * t_len, D), F32),
            jax.ShapeDtypeStruct((n_batch, 2, HG_H, HG_DK, HG_DK), F32),
        ],
        scratch_shapes=[pltpu.VMEM((HG_H, HG_DK, HG_DK), F32), pltpu.VMEM((BLK, D), F32)],
        compiler_params=_cparams(("arbitrary", "arbitrary", "arbitrary")),
        name="hgrn_recurrence",
    )(q, k, lf, v, s0)


def _hg_out_kernel(of_ref, ob_ref, gg_ref, on_ref, w_ref, x_ref, m_ref, o_ref):
    o = of_ref[...] + ob_ref[...]
    gain = on_ref[...]
    parts = []
    for h in range(HG_H):
        parts.append(_rms(o[:, h * HG_DK:(h + 1) * HG_DK], gain))
    a = (jnp.concatenate(parts, axis=1) * _silu(gg_ref[...])).astype(BF16)
    o_ref[...] = x_ref[...] + m_ref[...][2:3] * _dot(a, w_ref[...])


def _hg_out(o2, gg, o_norm, w_o, x, mods_l):
    fixed = lambda i: (0, 0)
    return pl.pallas_call(
        _hg_out_kernel,
        grid=(R // TM,),
        in_specs=[
            pl.BlockSpec((None, TM, D), lambda i: (0, i, 0)),
            pl.BlockSpec((None, TM, D), lambda i: (1, i, 0)),
            pl.BlockSpec((TM, D), lambda i: (i, 0)),
            pl.BlockSpec((1, HG_DK), fixed),
            pl.BlockSpec((D, D), fixed),
            pl.BlockSpec((TM, D), lambda i: (i, 0)),
            pl.BlockSpec((None, N_MOD, D), lambda i: (_cond_idx(i, TM), 0, 0)),
        ],
        out_specs=pl.BlockSpec((TM, D), lambda i: (i, 0)),
        out_shape=jax.ShapeDtypeStruct((R, D), F32),
        compiler_params=_cparams(("arbitrary",)),
        name="hgrn_out",
    )(o2, o2, gg, o_norm, w_o, x, mods_l)


def _router_kernel(x_ref, m_ref, g_ref, rt_ref, h8_ref, idx_ref, wt_ref):
    m = m_ref[...]
    h = _normmod(x_ref[...], g_ref[...], m[4:5], m[3:4])
    tm = h.shape[0]
    for s in range(ROW_TILE):
        h8_ref[pl.ds(s, tm, stride=ROW_TILE), :] = h[:, s * LANES:(s + 1) * LANES]
    h1, h2, h3 = _split3(h)
    r1, r2, r3 = _split3(rt_ref[...])
    lt = (_dot_nt(r1, h1) + _dot_nt(r1, h2) + _dot_nt(r2, h1)
          + _dot_nt(r1, h3) + _dot_nt(r3, h1) + _dot_nt(r2, h2))
    lg = lt[:N_EXP]
    e = jnp.exp(lg - jnp.max(lg, axis=0, keepdims=True))
    p = e / jnp.sum(e, axis=0, keepdims=True)
    io = lax.broadcasted_iota(I32, p.shape, 0)
    m1 = jnp.max(p, axis=0, keepdims=True)
    i1 = jnp.min(jnp.where(p == m1, io, N_EXP), axis=0, keepdims=True)
    p2 = jnp.where(io == i1, -1.0, p)
    m2 = jnp.max(p2, axis=0, keepdims=True)
    i2 = jnp.min(jnp.where(p2 == m2, io, N_EXP), axis=0, keepdims=True)
    den = m1 + m2
    idx_ref[...] = jnp.concatenate([i1, i2], axis=0)
    wt_ref[...] = jnp.concatenate([m1 / den, m2 / den], axis=0)


def _router(x, mods_l, gain, router_t):
    return pl.pallas_call(
        _router_kernel,
        grid=(R // TM,),
        in_specs=[
            pl.BlockSpec((TM, D), lambda i: (i, 0)),
            pl.BlockSpec((None, N_MOD, D), lambda i: (_cond_idx(i, TM), 0, 0)),
            pl.BlockSpec((1, D), lambda i: (0, 0)),
            pl.BlockSpec((2 * SUB, D), lambda i: (0, 0)),
        ],
        out_specs=[
            pl.BlockSpec((TM * ROW_TILE, LANES), lambda i: (i, 0)),
            pl.BlockSpec((TOP_K, TM), lambda i: (0, i)),
            pl.BlockSpec((TOP_K, TM), lambda i: (0, i)),
        ],
        out_shape=[
            jax.ShapeDtypeStruct((R * ROW_TILE, LANES), F32),
            jax.ShapeDtypeStruct((TOP_K, R), I32),
            jax.ShapeDtypeStruct((TOP_K, R), F32),
        ],
        compiler_params=_cparams(("arbitrary",)),
        name="moe_router",
    )(x, mods_l, gain, router_t)


def _gather_rows(idx_ref, idx0, n_rows, src_ref, dst_ref, sem):
    def copy(r):
        src_row = idx_ref[idx0 + r]
        return pltpu.make_async_copy(
            src_ref.at[pl.ds(pl.multiple_of(src_row * SUB, SUB), SUB)],
            dst_ref.at[pl.ds(pl.multiple_of(r * SUB, SUB), SUB)],
            sem)

    def issue(r, carry):
        copy(r).start()
        return carry

    def drain(r, carry):
        copy(r).wait()
        return carry

    lax.fori_loop(0, n_rows, issue, 0)
    lax.fori_loop(0, n_rows, drain, 0)


EXP_CHUNK = 896
N_EXP_CHUNK = E_FF // EXP_CHUNK
IDX_BLOCK = 1024


def _expert_kernel(te_ref, tv_ref, tok_ref, h8_ref, w1_ref, w3_ref, w2_ref, y8_ref, x8_scr, xb_scr, acc_scr, sem):
    i = pl.program_id(0)
    kc = pl.program_id(1)
    valid = tv_ref[i] == 1

    @pl.when(valid & (kc == 0))
    def _():
        _gather_rows(tok_ref, (i % (IDX_BLOCK // TM_MOE)) * TM_MOE, TM_MOE, h8_ref, x8_scr, sem)
        for s in range(ROW_TILE):
            xb_scr[:, s * LANES:(s + 1) * LANES] = x8_scr[pl.ds(s, TM_MOE, stride=ROW_TILE), :].astype(BF16)
        acc_scr[...] = jnp.zeros_like(acc_scr)

    @pl.when(valid)
    def _():
        x = xb_scr[...]
        a = _dot(x, w1_ref[...])
        b = _dot(x, w3_ref[...])
        acc_scr[...] += _dot((_silu(a) * b).astype(BF16), w2_ref[...])

    @pl.when(valid & (kc == N_EXP_CHUNK - 1))
    def _():
        for s in range(ROW_TILE):
            y8_ref[pl.ds(s, TM_MOE, stride=ROW_TILE), :] = acc_scr[:, s * LANES:(s + 1) * LANES]

    @pl.when(jnp.logical_not(valid) & (kc == N_EXP_CHUNK - 1))
    def _():
        y8_ref[...] = jnp.zeros_like(y8_ref)


def _experts(tile_expert, tile_valid, slot_token, h8, w1, w3, w2):
    def kc_eff(kc, tv, i):
        return jnp.where(tv[i] == 1, kc, N_EXP_CHUNK - 1)

    grid_spec = pltpu.PrefetchScalarGridSpec(
        num_scalar_prefetch=2,
        grid=(N_TILE, N_EXP_CHUNK),
        in_specs=[
            pl.BlockSpec((IDX_BLOCK,), lambda i, kc, te, tv: (i // (IDX_BLOCK // TM_MOE),),
                         memory_space=pltpu.SMEM),
            pl.BlockSpec(memory_space=pl.ANY),
            pl.BlockSpec((None, D, EXP_CHUNK), lambda i, kc, te, tv: (te[i], 0, kc_eff(kc, tv, i))),
            pl.BlockSpec((None, D, EXP_CHUNK), lambda i, kc, te, tv: (te[i], 0, kc_eff(kc, tv, i))),
            pl.BlockSpec((None, EXP_CHUNK, D), lambda i, kc, te, tv: (te[i], kc_eff(kc, tv, i), 0)),
        ],
        out_specs=pl.BlockSpec((TM_MOE * ROW_TILE, LANES), lambda i, kc, te, tv: (i, 0)),
        scratch_shapes=[pltpu.VMEM((TM_MOE * ROW_TILE, LANES), F32), pltpu.VMEM((TM_MOE, D), BF16),
                        pltpu.VMEM((TM_MOE, D), F32), pltpu.SemaphoreType.DMA(())],
    )
    return pl.pallas_call(
        _expert_kernel,
        grid_spec=grid_spec,
        out_shape=jax.ShapeDtypeStruct((N_SLOT * ROW_TILE, LANES), F32),
        compiler_params=_cparams(("arbitrary", "arbitrary"), 48),
        name="moe_experts",
    )(tile_expert, tile_valid, slot_token, h8, w1, w3, w2)


TM_COMB = IDX_BLOCK


def _moe_resid_kernel(p0_ref, p1_ref, y8_ref, wc_ref, x_ref, m_ref, fn_ref, o_ref, y0_scr, y1_scr, sem, *, final):
    tm = x_ref.shape[0]
    _gather_rows(p0_ref, 0, tm, y8_ref, y0_scr, sem)
    _gather_rows(p1_ref, 0, tm, y8_ref, y1_scr, sem)

    def rows(ref):
        return jnp.concatenate([ref[pl.ds(s, tm, stride=ROW_TILE), :] for s in range(ROW_TILE)], axis=1)

    w = wc_ref[...]
    f = w[:, 0:1] * rows(y0_scr) + w[:, 1:2] * rows(y1_scr)
    xn = x_ref[...] + m_ref[...][5:6] * f
    if final:
        xn = _rms(xn, fn_ref[...])
    o_ref[...] = xn


def _moe_resid(pos, y8, wcol, x, mods_l, final_gain, final):
    tm = TM_COMB
    nt = R // tm
    return pl.pallas_call(
        functools.partial(_moe_resid_kernel, final=final),
        grid=(nt,),
        in_specs=[
            pl.BlockSpec((IDX_BLOCK,), lambda i: (i,), memory_space=pltpu.SMEM),
            pl.BlockSpec((IDX_BLOCK,), lambda i: (nt + i,), memory_space=pltpu.SMEM),
            pl.BlockSpec(memory_space=pl.ANY),
            pl.BlockSpec((tm, TOP_K), lambda i: (i, 0)),
            pl.BlockSpec((tm, D), lambda i: (i, 0)),
            pl.BlockSpec((None, N_MOD, D), lambda i: (_cond_idx(i, tm), 0, 0)),
            pl.BlockSpec((1, D), lambda i: (0, 0)),
        ],
        out_specs=pl.BlockSpec((tm, D), lambda i: (i, 0)),
        out_shape=jax.ShapeDtypeStruct((R, D), F32),
        scratch_shapes=[pltpu.VMEM((tm * ROW_TILE, LANES), F32), pltpu.VMEM((tm * ROW_TILE, LANES), F32),
                        pltpu.SemaphoreType.DMA(())],
        compiler_params=_cparams(("arbitrary",), 48),
        name="moe_resid",
    )(pos, pos, y8, wcol, x, mods_l, final_gain)


def _route_tables(idx):
    e = idx.reshape(-1)
    onehot = (e[:, None] == jnp.arange(N_EXP, dtype=I32)[None, :]).astype(I32)
    csum = jnp.cumsum(onehot, axis=0)
    rank = jnp.take_along_axis(csum, e[:, None], axis=1)[:, 0] - 1
    counts = csum[-1]
    padded = ((counts + TM_MOE - 1) // TM_MOE) * TM_MOE
    gend = jnp.cumsum(padded)
    pos = ((gend - padded)[e] + rank).astype(I32)
    token = jnp.arange(TOP_K * R, dtype=I32) & (R - 1)
    slot_token = jnp.zeros((N_SLOT,), I32).at[pos].set(token)
    tile_start = jnp.arange(N_TILE, dtype=I32) * TM_MOE
    te = jnp.sum((tile_start[:, None] >= gend[None, :]).astype(I32), axis=1)
    valid = tile_start < gend[-1]
    te_last = te[gend[-1] // TM_MOE - 1]
    te = jnp.minimum(jnp.where(valid, te, te_last), N_EXP - 1)
    return pos, slot_token, te.astype(I32), valid.astype(I32)


def _rot_half(w):
    wa = w.reshape(w.shape[:-1] + (2, 2, ROPE_AXIS // 2))
    return jnp.stack([-wa[..., 1, :], wa[..., 0, :]], axis=-2).reshape(w.shape)


def _rope_tables(tm):
    rows = DEC_SEQ // GRID_W
    r = jnp.repeat(jnp.arange(rows), GRID_W).astype(F32)
    c = jnp.tile(jnp.arange(GRID_W), rows).astype(F32)
    inv = ROPE_THETA ** (-jnp.arange(0, ROPE_AXIS, 2, dtype=F32) / ROPE_AXIS)
    ang_r = r[:, None] * inv
    ang_c = c[:, None] * inv
    ang = jnp.concatenate([ang_r, ang_r, ang_c, ang_c], axis=-1)
    cos, sin = jnp.cos(ang), jnp.sin(ang)
    z32 = jnp.zeros_like(cos)
    one = jnp.ones((tm, QK_ROPE), F32)
    zt = jnp.zeros((tm, QK_ROPE), F32)
    tq_c = jnp.concatenate([jnp.concatenate([cos, cos, z32, z32], 1), jnp.concatenate([one, one, zt, zt], 1)], 0)
    tq_s = jnp.concatenate([jnp.concatenate([sin, sin, z32, z32], 1), jnp.zeros((tm, LANES), F32)], 0)
    tk_c = jnp.concatenate([jnp.concatenate([cos, z32, z32, z32], 1), jnp.concatenate([one, zt, zt, zt], 1)], 0)
    tk_s = jnp.concatenate([jnp.concatenate([sin, z32, z32, z32], 1), jnp.zeros((tm, LANES), F32)], 0)
    return tq_c, tq_s, tk_c, tk_s


def _mla_weights(w_dq, w_uq, w_dkv, w_uk, w_uv, w_o):
    kr_w = w_dkv[:, KV_LORA:]
    w_a = jnp.concatenate(
        [w_dq, w_dkv[:, :KV_LORA], kr_w, _rot_half(kr_w), jnp.zeros((D, LANES - 2 * QK_ROPE), F32)], axis=1)
    uq = w_uq.reshape(Q_LORA, N_PAIR, 2, QK_NOPE + QK_ROPE)
    nope = uq[..., :QK_NOPE].reshape(Q_LORA, N_PAIR, 2 * QK_NOPE)
    rope = uq[..., QK_NOPE:]
    w_uq2 = jnp.concatenate(
        [nope, rope.reshape(Q_LORA, N_PAIR, 2 * QK_ROPE), _rot_half(rope).reshape(Q_LORA, N_PAIR, 2 * QK_ROPE)],
        axis=-1).reshape(Q_LORA, N_PAIR * 2 * LANES)
    uk = w_uk.reshape(KV_LORA, N_PAIR, 2 * QK_NOPE)
    wk = jnp.concatenate([uk, jnp.zeros((KV_LORA, N_PAIR, LANES), F32)], axis=-1).reshape(KV_LORA, N_PAIR * 2 * LANES)
    eye = jnp.eye(LANES, QK_ROPE, dtype=F32)
    pair = jnp.concatenate([jnp.zeros((LANES, LANES), F32), eye, eye, jnp.zeros((LANES, LANES - 2 * QK_ROPE), F32)], 1)
    wkr = jnp.tile(pair, (1, N_PAIR))
    wv = w_uv.reshape(KV_LORA, HEADS * V_HEAD)
    wo = w_o.reshape(HEADS * V_HEAD, D)
    return [w.astype(BF16) for w in (w_a, w_uq2, wk, wkr, wv, wo)]


def kernel(x_prompt, x_sample, cache_ckv, cache_krope, state_hgrn, c, c_ctx, w_ada, b_ada, norm_mix, norm_ffn, mla_w_dq, mla_q_norm, mla_w_uq, mla_w_dkv, mla_kv_norm, mla_w_uk, mla_w_uv, mla_w_o, hg_w_q, hg_w_f, hg_w_i, hg_w_g, hg_lb_logits, hg_o_norm, hg_w_o, ffn_w1, ffn_w3, ffn_w2, moe_router, moe_w1, moe_w3, moe_w2, final_norm):
    x = jnp.concatenate([x_prompt.reshape(R_P, D), x_sample.reshape(R_S, D)], axis=0)
    cond = jnp.concatenate([c, c_ctx[None], jnp.zeros((COND_PAD - N_COND, D), F32)], axis=0)
    mods = _ada(cond, w_ada, b_ada)
    lb_all = _lower_bounds(hg_lb_logits)
    tq_c, tq_s, tk_c, tk_s = _rope_tables(TM)
    new_ckv, new_krope, new_hgrn = [], [], []

    for l in range(DEPTH):
        j = l // 2
        mods_l = mods[l]
        gain_mix = norm_mix[l][None]
        gain_ffn = norm_ffn[l][None]
        if l % 2 == 0:
            w_a, w_uq2, wk, wkr, wv, wo = _mla_weights(
                mla_w_dq[j], mla_w_uq[j], mla_w_dkv[j], mla_w_uk[j], mla_w_uv[j], mla_w_o[j])
            cq, ckv, kr = _mla_proj(x, mods_l, gain_mix, w_a, mla_q_norm[j][None], mla_kv_norm[j][None], tk_c, tk_s)
            q2 = _q_up(cq, w_uq2, tq_c, tq_s)
            k2_p, v_p = _kv_up(ckv, kr, R_P, wk, wkr, wv)
            ckv_s = jnp.concatenate([cache_ckv[:, j], ckv[R_P:].reshape(DEC_BATCH, DEC_SEQ, KV_LORA)], axis=1)
            kr_cache = jnp.pad(cache_krope[:, j], ((0, 0), (0, 0), (0, LANES - QK_ROPE)))
            kr_s = jnp.concatenate([kr_cache, kr[R_P:].reshape(DEC_BATCH, DEC_SEQ, LANES)], axis=1)
            s_all = PAST + DEC_SEQ
            k2_s, v_s = _kv_up(ckv_s.reshape(DEC_BATCH * s_all, KV_LORA), kr_s.reshape(DEC_BATCH * s_all, LANES),
                               DEC_BATCH * s_all, wk, wkr, wv)
            o_p = _attention(q2, k2_p, v_p, BATCH, SEQ, SEQ, 0, SEQ)
            o_s = _attention(q2, k2_s, v_s, DEC_BATCH, DEC_SEQ, s_all, R_P, 512)
            x = _mm_resid(jnp.concatenate([o_p, o_s], axis=0), wo, x, mods_l, 2)
            new_ckv.append(ckv[:R_P].reshape(BATCH, SEQ, KV_LORA))
            new_krope.append(kr[:R_P, :QK_ROPE].reshape(BATCH, SEQ, QK_ROPE))
        else:
            w5 = jnp.stack([hg_w_q[j], hg_w_f[j, 0], hg_w_f[j, 1], hg_w_i[j], hg_w_g[j]]).astype(BF16)
            q, k, lf, v, gg = _hg_proj(x, mods_l, gain_mix, w5, lb_all[j])
            s_zero = jnp.zeros((BATCH, 2, HG_H, HG_DK, HG_DK), F32)
            o_p, st_p = _gla(q, k, lf, v, s_zero, BATCH, SEQ, 0)
            o_s, _ = _gla(q, k, lf, v, state_hgrn[:, j], DEC_BATCH, DEC_SEQ, R_P)
            x = _hg_out(jnp.concatenate([o_p, o_s], axis=1), gg, hg_o_norm[j][None], hg_w_o[j].astype(BF16), x, mods_l)
            new_hgrn.append(st_p)
        if l % 2 == 0:
            x = _ffn(x, mods_l, gain_ffn, ffn_w1[j].astype(BF16), ffn_w3[j].astype(BF16), ffn_w2[j].astype(BF16))
        else:
            router_t = jnp.concatenate([moe_router[j].T, jnp.zeros((2 * SUB - N_EXP, D), F32)], axis=0)
            h8, idx, wts = _router(x, mods_l, gain_ffn, router_t)
            pos, slot_token, tile_expert, tile_valid = _route_tables(idx)
            y8 = _experts(tile_expert, tile_valid, slot_token, h8,
                          moe_w1[j].astype(BF16), moe_w3[j].astype(BF16), moe_w2[j].astype(BF16))
            x = _moe_resid(pos, y8, wts.T, x, mods_l, final_norm[None], final=(l == DEPTH - 1))

    y_prompt = x[:R_P].reshape(BATCH, SEQ, D)
    y_sample = x[R_P:].reshape(DEC_BATCH, DEC_SEQ, D)
    return (y_prompt, y_sample, jnp.stack(new_ckv, axis=1), jnp.stack(new_krope, axis=1),
            jnp.stack(new_hgrn, axis=1))
```

```python
import functools
import math

import jax
import jax.numpy as jnp
from jax import lax
from jax.experimental import pallas as pl
from jax.experimental.pallas import tpu as pltpu

F32 = jnp.float32
BF16 = jnp.bfloat16
I32 = jnp.int32

D = 1024
BATCH, SEQ = 32, 256
DEC_BATCH, DEC_SEQ = 8, 1024
PAST = 256
DEPTH = 4
R_P = BATCH * SEQ
R_S = DEC_BATCH * DEC_SEQ
R = R_P + R_S
N_COND = DEC_BATCH + 1
COND_PAD = 16
N_MOD = 6
EPS = 1e-6
GRID_W = 64
HEADS, QK_NOPE, QK_ROPE, V_HEAD = 16, 64, 32, 64
Q_LORA, KV_LORA = 512, 256
N_PAIR = HEADS // 2
ROPE_AXIS = QK_ROPE // 2
ROPE_THETA = 10000.0
HG_H, HG_DK = 8, 128
BLK = 128
CHUNK = 16
D_FF = 2816
N_EXP, TOP_K, E_FF = 8, 2, 3584
LANES = 128
SUB = 8
ROW_TILE = D // LANES
TM = 512
TM_MOE = 512
N_SLOT = TOP_K * R + N_EXP * TM_MOE
N_TILE = N_SLOT // TM_MOE
MIB = 1024 * 1024
assert DEPTH % 2 == 0


def _cparams(sem, vmem_mib=40):
    return pltpu.CompilerParams(dimension_semantics=sem, vmem_limit_bytes=vmem_mib * MIB)


def _cond_idx(i, tm):
    r0 = i * tm
    return jnp.where(r0 < R_P, DEC_BATCH, (r0 - R_P) // DEC_SEQ)


def _rope_idx(i, tm):
    nb = DEC_SEQ // tm
    r0 = i * tm
    return jnp.where(r0 < R_P, nb, ((r0 - R_P) // tm) % nb)


def _normmod(x, gain, scale, shift):
    ms = jnp.mean(x * x, axis=-1, keepdims=True)
    return (x * lax.rsqrt(ms + EPS)) * gain * (1.0 + scale) + shift


def _rms(x, gain):
    ms = jnp.mean(x * x, axis=-1, keepdims=True)
    return (x * lax.rsqrt(ms + EPS)) * gain


def _silu(x):
    return x / (1.0 + jnp.exp(-x))


def _dot(a, b):
    return jnp.dot(a, b, preferred_element_type=F32)


def _dot_nt(a, b):
    return lax.dot_general(a, b, (((1,), (1,)), ((), ())), preferred_element_type=F32)


def _split3(x):
    hi = x.astype(BF16)
    r1 = x - hi.astype(F32)
    mid = r1.astype(BF16)
    lo = (r1 - mid.astype(F32)).astype(BF16)
    return hi, mid, lo


def _ada_kernel(c_ref, w_ref, b_ref, o_ref):
    o_ref[...] = _dot(_silu(c_ref[...]), w_ref[...]) + b_ref[...]


def _ada(cond, w_ada, b_ada):
    tn = 1536
    out = pl.pallas_call(
        _ada_kernel,
        grid=(DEPTH, N_MOD * D // tn),
        in_specs=[
            pl.BlockSpec((COND_PAD, D), lambda l, j: (0, 0)),
            pl.BlockSpec((None, D, tn), lambda l, j: (l, 0, j)),
            pl.BlockSpec((None, 1, tn), lambda l, j: (l, 0, j)),
        ],
        out_specs=pl.BlockSpec((None, COND_PAD, tn), lambda l, j: (l, 0, j)),
        out_shape=jax.ShapeDtypeStruct((DEPTH, COND_PAD, N_MOD * D), F32),
        compiler_params=_cparams(("arbitrary", "arbitrary")),
        name="ada",
    )(cond, w_ada, b_ada.reshape(DEPTH, 1, N_MOD * D))
    return out[:, :N_COND].reshape(DEPTH, N_COND, N_MOD, D)


def _lb_kernel(x_ref, o_ref):
    x = x_ref[...]
    e = jnp.exp(x - jnp.max(x, axis=0, keepdims=True))
    sm = e / jnp.sum(e, axis=0, keepdims=True)
    n = x.shape[0]
    cum = sm[0]
    o_ref[0] = jnp.zeros_like(cum)
    for i in range(1, n):
        cum_i = cum + sm[i]
        o_ref[i] = cum_i - sm[0]
        cum = cum_i


def _lower_bounds(lb_logits):
    return pl.pallas_call(
        _lb_kernel,
        out_shape=jax.ShapeDtypeStruct(lb_logits.shape, F32),
        name="hgrn_lower_bounds",
    )(lb_logits)


def _mla_proj_kernel(x_ref, m_ref, g_ref, w_ref, qn_ref, kvn_ref, tc_ref, ts_ref,
                     cq_ref, ckv_ref, kr_ref):
    m = m_ref[...]
    h = _normmod(x_ref[...], g_ref[...], m[1:2], m[0:1]).astype(BF16)
    y = _dot(h, w_ref[...])
    cq_ref[...] = _rms(y[:, :Q_LORA], qn_ref[...]).astype(BF16)
    ckv_ref[...] = _rms(y[:, Q_LORA:Q_LORA + KV_LORA], kvn_ref[...])
    slab = y[:, Q_LORA + KV_LORA:]
    kr_ref[...] = slab * tc_ref[...] + pltpu.roll(slab, LANES - QK_ROPE, 1) * ts_ref[...]


def _mla_proj(x, mods_l, gain, w_a, q_norm, kv_norm, tk_c, tk_s):
    n_a = w_a.shape[1]
    row = lambda i: (i, 0)
    fixed = lambda i: (0, 0)
    return pl.pallas_call(
        _mla_proj_kernel,
        grid=(R // TM,),
        in_specs=[
            pl.BlockSpec((TM, D), row),
            pl.BlockSpec((None, N_MOD, D), lambda i: (_cond_idx(i, TM), 0, 0)),
            pl.BlockSpec((1, D), fixed),
            pl.BlockSpec((D, n_a), fixed),
            pl.BlockSpec((1, Q_LORA), fixed),
            pl.BlockSpec((1, KV_LORA), fixed),
            pl.BlockSpec((TM, LANES), lambda i: (_rope_idx(i, TM), 0)),
            pl.BlockSpec((TM, LANES), lambda i: (_rope_idx(i, TM), 0)),
        ],
        out_specs=[
            pl.BlockSpec((TM, Q_LORA), row),
            pl.BlockSpec((TM, KV_LORA), row),
            pl.BlockSpec((TM, LANES), row),
        ],
        out_shape=[
            jax.ShapeDtypeStruct((R, Q_LORA), BF16),
            jax.ShapeDtypeStruct((R, KV_LORA), F32),
            jax.ShapeDtypeStruct((R, LANES), F32),
        ],
        compiler_params=_cparams(("arbitrary",)),
        name="mla_proj",
    )(x, mods_l, gain, w_a, q_norm, kv_norm, tk_c, tk_s)


def _q_up_kernel(cq_ref, w_ref, tc_ref, ts_ref, q_ref):
    y = _dot(cq_ref[...], w_ref[...])
    tc = tc_ref[...]
    ts = ts_ref[...]
    for p in range(N_PAIR):
        lo = p * 2 * LANES
        q_ref[:, lo:lo + LANES] = y[:, lo:lo + LANES].astype(BF16)
        hi = y[:, lo + LANES:lo + 2 * LANES]
        q_ref[:, lo + LANES:lo + 2 * LANES] = (hi * tc + pltpu.roll(hi, LANES // 2, 1) * ts).astype(BF16)


def _q_up(cq, w_uq2, tq_c, tq_s):
    nq = w_uq2.shape[1]
    return pl.pallas_call(
        _q_up_kernel,
        grid=(R // TM,),
        in_specs=[
            pl.BlockSpec((TM, Q_LORA), lambda i: (i, 0)),
            pl.BlockSpec((Q_LORA, nq), lambda i: (0, 0)),
            pl.BlockSpec((TM, LANES), lambda i: (_rope_idx(i, TM), 0)),
            pl.BlockSpec((TM, LANES), lambda i: (_rope_idx(i, TM), 0)),
        ],
        out_specs=pl.BlockSpec((TM, nq), lambda i: (i, 0)),
        out_shape=jax.ShapeDtypeStruct((R, nq), BF16),
        compiler_params=_cparams(("arbitrary",)),
        name="mla_q_up",
    )(cq, w_uq2, tq_c, tq_s)


def _kv_up_kernel(c_ref, r_ref, wk_ref, wkr_ref, wv_ref, k_ref, v_ref):
    c = c_ref[...].astype(BF16)
    r = r_ref[...].astype(BF16)
    k_ref[...] = (_dot(c, wk_ref[...]) + _dot(r, wkr_ref[...])).astype(BF16)
    v_ref[...] = _dot(c, wv_ref[...]).astype(BF16)


def _kv_up(ckv, kr, n_rows, wk, wkr, wv):
    nk = wk.shape[1]
    nv = wv.shape[1]
    fixed = lambda i: (0, 0)
    return pl.pallas_call(
        _kv_up_kernel,
        grid=(n_rows // TM,),
        in_specs=[
            pl.BlockSpec((TM, KV_LORA), lambda i: (i, 0)),
            pl.BlockSpec((TM, LANES), lambda i: (i, 0)),
            pl.BlockSpec((KV_LORA, nk), fixed),
            pl.BlockSpec((LANES, nk), fixed),
            pl.BlockSpec((KV_LORA, nv), fixed),
        ],
        out_specs=[
            pl.BlockSpec((TM, nk), lambda i: (i, 0)),
            pl.BlockSpec((TM, nv), lambda i: (i, 0)),
        ],
        out_shape=[
            jax.ShapeDtypeStruct((n_rows, nk), BF16),
            jax.ShapeDtypeStruct((n_rows, nv), BF16),
        ],
        compiler_params=_cparams(("arbitrary",)),
        name="mla_kv_up",
    )(ckv, kr, wk, wkr, wv)


def _attn_kernel(q_ref, k_ref, v_ref, *rest, n_pair):
    o_ref = rest[-1]
    c = math.log2(math.e) / math.sqrt(QK_NOPE + QK_ROPE)
    lq = lax.broadcasted_iota(I32, (1, 2 * LANES), 1)
    lv = lax.broadcasted_iota(I32, (1, LANES), 1)
    sel_a = (lq < QK_NOPE) | ((lq >= LANES) & (lq < LANES + QK_ROPE))
    sel_b = ((lq >= QK_NOPE) & (lq < LANES)) | ((lq >= LANES + QK_ROPE) & (lq < LANES + 2 * QK_ROPE))
    for p in range(n_pair):
        q = q_ref[:, p * 2 * LANES:(p + 1) * 2 * LANES]
        k = k_ref[:, p * 2 * LANES:(p + 1) * 2 * LANES]
        v = v_ref[:, p * LANES:(p + 1) * LANES]
        zq = jnp.zeros_like(q)
        zv = jnp.zeros_like(v)
        out = None
        for sel, vsel in ((sel_a, lv < V_HEAD), (sel_b, lv >= V_HEAD)):
            s = _dot_nt(jnp.where(sel, q, zq), k) * c
            e = jnp.exp2(s - jnp.max(s, axis=-1, keepdims=True))
            den = jnp.sum(e, axis=-1, keepdims=True)
            o = _dot(e.astype(BF16), jnp.where(vsel, v, zv)) / den
            out = o if out is None else out + o
        o_ref[:, p * LANES:(p + 1) * LANES] = out.astype(BF16)


def _attention(q2, k2, v, n_batch, t_len, s_len, q_row0, tq, n_pair):
    nq = t_len // tq
    qb0 = q_row0 // tq
    return pl.pallas_call(
        functools.partial(_attn_kernel, n_pair=n_pair),
        grid=(n_batch, N_PAIR // n_pair, nq),
        in_specs=[
            pl.BlockSpec((tq, n_pair * 2 * LANES), lambda b, p, i: (qb0 + b * nq + i, p)),
            pl.BlockSpec((s_len, n_pair * 2 * LANES), lambda b, p, i: (b, p)),
            pl.BlockSpec((s_len, n_pair * LANES), lambda b, p, i: (b, p)),
        ],
        out_specs=pl.BlockSpec((tq, n_pair * LANES), lambda b, p, i: (b * nq + i, p)),
        out_shape=jax.ShapeDtypeStruct((n_batch * t_len, HEADS * V_HEAD), BF16),
        compiler_params=_cparams(("arbitrary", "arbitrary", "arbitrary")),
        name="mla_attention",
    )(q2, k2, v)


N_TILE_P = R_P // TM


def _prompt_blk(i):
    return jnp.minimum(i, N_TILE_P - 1)


def _sample_blk(i):
    return jnp.maximum(i - N_TILE_P, 0)


def _mm_resid_kernel(ap_ref, as_ref, w_ref, x_ref, m_ref, o_ref, *, gate_idx):
    gate = m_ref[...][gate_idx:gate_idx + 1]
    a = jnp.where(pl.program_id(0) < N_TILE_P, ap_ref[...], as_ref[...])
    o_ref[...] = x_ref[...] + gate * _dot(a, w_ref[...])


def _mm_resid(a_p, a_s, w, x, mods_l, gate_idx):
    k = a_p.shape[1]
    return pl.pallas_call(
        functools.partial(_mm_resid_kernel, gate_idx=gate_idx),
        grid=(R // TM,),
        in_specs=[
            pl.BlockSpec((TM, k), lambda i: (_prompt_blk(i), 0)),
            pl.BlockSpec((TM, k), lambda i: (_sample_blk(i), 0)),
            pl.BlockSpec((k, D), lambda i: (0, 0)),
            pl.BlockSpec((TM, D), lambda i: (i, 0)),
            pl.BlockSpec((None, N_MOD, D), lambda i: (_cond_idx(i, TM), 0, 0)),
        ],
        out_specs=pl.BlockSpec((TM, D), lambda i: (i, 0)),
        out_shape=jax.ShapeDtypeStruct((R, D), F32),
        compiler_params=_cparams(("arbitrary",)),
        name="mm_resid",
    )(a_p, a_s, w, x, mods_l)


FFN_CHUNK = 1408


def _ffn_kernel(x_ref, m_ref, g_ref, w1_ref, w3_ref, w2_ref, o_ref):
    x = x_ref[...]
    m = m_ref[...]
    h = _normmod(x, g_ref[...], m[4:5], m[3:4]).astype(BF16)
    acc = jnp.zeros(x.shape, F32)
    for c in range(D_FF // FFN_CHUNK):
        sl = slice(c * FFN_CHUNK, (c + 1) * FFN_CHUNK)
        a = _dot(h, w1_ref[:, sl])
        b = _dot(h, w3_ref[:, sl])
        acc = acc + _dot((_silu(a) * b).astype(BF16), w2_ref[sl, :])
    o_ref[...] = x + m[5:6] * acc


def _ffn(x, mods_l, gain, w1, w3, w2):
    tm = 256
    fixed = lambda i: (0, 0)
    return pl.pallas_call(
        _ffn_kernel,
        grid=(R // tm,),
        in_specs=[
            pl.BlockSpec((tm, D), lambda i: (i, 0)),
            pl.BlockSpec((None, N_MOD, D), lambda i: (_cond_idx(i, tm), 0, 0)),
            pl.BlockSpec((1, D), fixed),
            pl.BlockSpec((D, D_FF), fixed),
            pl.BlockSpec((D, D_FF), fixed),
            pl.BlockSpec((D_FF, D), fixed),
        ],
        out_specs=pl.BlockSpec((tm, D), lambda i: (i, 0)),
        out_shape=jax.ShapeDtypeStruct((R, D), F32),
        compiler_params=_cparams(("arbitrary",), 56),
        name="dense_swiglu",
    )(x, mods_l, gain, w1, w3, w2)


def _forget_gate(z, lb):
    e = jnp.exp(-jnp.abs(z))
    log_sig = jnp.minimum(z, 0.0) - jnp.log1p(e)
    a = jnp.log(lb)
    b = jnp.log1p(-lb) + log_sig
    log_f = jnp.maximum(a, b) + jnp.log1p(jnp.exp(-jnp.abs(a - b)))
    k = (1.0 - lb) * (jnp.where(z >= 0, e, 1.0) / (1.0 + e))
    return k, log_f


def _hg_proj_kernel(x_ref, m_ref, g_ref, w_ref, lb_ref, q_ref, k_ref, lf_ref, v_ref, gg_ref):
    m = m_ref[...]
    h = _normmod(x_ref[...], g_ref[...], m[1:2], m[0:1]).astype(BF16)
    q_ref[...] = _dot(h, w_ref[0])
    for dr in range(2):
        k, log_f = _forget_gate(_dot(h, w_ref[1 + dr]), lb_ref[dr])
        k_ref[dr] = k
        lf_ref[dr] = log_f
    v_ref[...] = _dot(h, w_ref[3]).astype(BF16)
    gg_ref[...] = _dot(h, w_ref[4])


def _hg_proj(x, mods_l, gain, w5, lb):
    tm = 256
    row = lambda i: (i, 0)
    dirs = lambda i: (0, i, 0)
    return pl.pallas_call(
        _hg_proj_kernel,
        grid=(R // tm,),
        in_specs=[
            pl.BlockSpec((tm, D), row),
            pl.BlockSpec((None, N_MOD, D), lambda i: (_cond_idx(i, tm), 0, 0)),
            pl.BlockSpec((1, D), lambda i: (0, 0)),
            pl.BlockSpec((5, D, D), lambda i: (0, 0, 0)),
            pl.BlockSpec((2, 1, D), lambda i: (0, 0, 0)),
        ],
        out_specs=[
            pl.BlockSpec((tm, D), row),
            pl.BlockSpec((2, tm, D), dirs),
            pl.BlockSpec((2, tm, D), dirs),
            pl.BlockSpec((tm, D), row),
            pl.BlockSpec((tm, D), row),
        ],
        out_shape=[
            jax.ShapeDtypeStruct((R, D), F32),
            jax.ShapeDtypeStruct((2, R, D), F32),
            jax.ShapeDtypeStruct((2, R, D), F32),
            jax.ShapeDtypeStruct((R, D), BF16),
            jax.ShapeDtypeStruct((R, D), F32),
        ],
        compiler_params=_cparams(("arbitrary",), 48),
        name="hgrn_proj",
    )(x, mods_l, gain, w5, lb.reshape(2, 1, D))


LEVEL_HALVES = (64, 32, 16)


def _gla_kernel(q_ref, k_ref, lf_ref, v_ref, s0_ref, *rest, nb):
    o_ref, sn_ref, st_scr, b_scr = rest[-4:]
    d = pl.program_id(1)
    n = pl.program_id(2)

    @pl.when(n == 0)
    def _():
        for h in range(HG_H):
            st_scr[h] = s0_ref[h].T

    row = lax.broadcasted_iota(I32, (BLK, BLK), 0)
    col = lax.broadcasted_iota(I32, (BLK, BLK), 1)
    ut = row + d * (BLK - 1 - 2 * row)
    us = col + d * (BLK - 1 - 2 * col)
    causal = us <= ut
    tri = jnp.where(causal, 1.0, 0.0).astype(BF16)

    hi, mid, lo = _split3(lf_ref[...])
    b_scr[...] = _dot(tri, hi) + _dot(tri, mid) + _dot(tri, lo)

    level_masks = []
    for hs in LEVEL_HALVES:
        sh = int(math.log2(2 * hs))
        same = (ut >> sh) == (us >> sh)
        level_masks.append(same & ((ut & (2 * hs - 1)) >= hs) & ((us & (2 * hs - 1)) < hs))
    base_mask = ((ut >> 4) == (us >> 4)) & causal

    def split_rows(sl, half):
        parts = []
        for jr in range(BLK // (2 * half)):
            r0 = jr * 2 * half + half - 1
            r = jnp.where(d == 0, b_scr[r0:r0 + 1, sl], b_scr[r0 + 1:r0 + 2, sl])
            parts.append(jnp.broadcast_to(r, (2 * half, HG_DK)))
        return parts[0] if len(parts) == 1 else jnp.concatenate(parts, axis=0)

    for h in range(HG_H):
        sl = slice(h * HG_DK, (h + 1) * HG_DK)
        bh = b_scr[:, sl]
        q = q_ref[:, sl]
        k = k_ref[:, sl]
        v = v_ref[:, sl]

        xq = bh - split_rows(sl, CHUNK // 2)
        att = jnp.where(base_mask, _dot_nt((q * jnp.exp(xq)).astype(BF16), (k * jnp.exp(-xq)).astype(BF16)), 0.0)
        for hs, msk in zip(LEVEL_HALVES, level_masks):
            e = jnp.exp(-jnp.abs(bh - split_rows(sl, hs)))
            att = jnp.where(msk, _dot_nt((q * e).astype(BF16), (k * e).astype(BF16)), att)

        b_last = jnp.where(d == 0, b_scr[BLK - 1:BLK, sl], b_scr[0:1, sl])
        q_in = (q * jnp.exp(bh)).astype(BF16)
        k_in = (k * jnp.exp(b_last - bh)).astype(BF16)
        st = st_scr[h]
        o_ref[:, sl] = _dot(att.astype(BF16), v) + _dot_nt(q_in, st.astype(BF16))
        v_t = v.astype(F32).T.astype(BF16)
        st_scr[h] = st * jnp.exp(b_last) + _dot(v_t, k_in)

    @pl.when(n == nb - 1)
    def _():
        for h in range(HG_H):
            sn_ref[h] = st_scr[h].T


def _gla(q, k, lf, v, s0, n_batch, t_len, row0):
    nb = t_len // BLK
    rb0 = row0 // BLK

    def rb(b, d, n):
        return b * nb + n + d * (nb - 1 - 2 * n)

    return pl.pallas_call(
        functools.partial(_gla_kernel, nb=nb),
        grid=(n_batch, 2, nb),
        in_specs=[
            pl.BlockSpec((BLK, D), lambda b, d, n: (rb0 + rb(b, d, n), 0)),
            pl.BlockSpec((None, BLK, D), lambda b, d, n: (d, rb0 + rb(b, d, n), 0)),
            pl.BlockSpec((None, BLK, D), lambda b, d, n: (d, rb0 + rb(b, d, n), 0)),
            pl.BlockSpec((BLK, D), lambda b, d, n: (rb0 + rb(b, d, n), 0)),
            pl.BlockSpec((None, None, HG_H, HG_DK, HG_DK), lambda b, d, n: (b, d, 0, 0, 0)),
        ],
        out_specs=[
            pl.BlockSpec((None, BLK, D), lambda b, d, n: (d, rb(b, d, n), 0)),
            pl.BlockSpec((None, None, HG_H, HG_DK, HG_DK), lambda b, d, n: (b, d, 0, 0, 0)),
        ],
        out_shape=[
            jax.ShapeDtypeStruct((2, n_batch * t_len, D), F32),
            jax.ShapeDtypeStruct((n_batch, 2, HG_H, HG_DK, HG_DK), F32),
        ],
        scratch_shapes=[pltpu.VMEM((HG_H, HG_DK, HG_DK), F32), pltpu.VMEM((BLK, D), F32)],
        compiler_params=_cparams(("arbitrary", "arbitrary", "arbitrary")),
        name="hgrn_recurrence",
    )(q, k, lf, v, s0)


def _hg_out_kernel(ofp_ref, obp_ref, ofs_ref, obs_ref, gg_ref, on_ref, w_ref, x_ref, m_ref, o_ref):
    o = jnp.where(pl.program_id(0) < N_TILE_P, ofp_ref[...] + obp_ref[...], ofs_ref[...] + obs_ref[...])
    gain = on_ref[...]
    parts = []
    for h in range(HG_H):
        parts.append(_rms(o[:, h * HG_DK:(h + 1) * HG_DK], gain))
    a = (jnp.concatenate(parts, axis=1) * _silu(gg_ref[...])).astype(BF16)
    o_ref[...] = x_ref[...] + m_ref[...][2:3] * _dot(a, w_ref[...])


def _hg_out(o2_p, o2_s, gg, o_norm, w_o, x, mods_l):
    fixed = lambda i: (0, 0)
    return pl.pallas_call(
        _hg_out_kernel,
        grid=(R // TM,),
        in_specs=[
            pl.BlockSpec((None, TM, D), lambda i: (0, _prompt_blk(i), 0)),
            pl.BlockSpec((None, TM, D), lambda i: (1, _prompt_blk(i), 0)),
            pl.BlockSpec((None, TM, D), lambda i: (0, _sample_blk(i), 0)),
            pl.BlockSpec((None, TM, D), lambda i: (1, _sample_blk(i), 0)),
            pl.BlockSpec((TM, D), lambda i: (i, 0)),
            pl.BlockSpec((1, HG_DK), fixed),
            pl.BlockSpec((D, D), fixed),
            pl.BlockSpec((TM, D), lambda i: (i, 0)),
            pl.BlockSpec((None, N_MOD, D), lambda i: (_cond_idx(i, TM), 0, 0)),
        ],
        out_specs=pl.BlockSpec((TM, D), lambda i: (i, 0)),
        out_shape=jax.ShapeDtypeStruct((R, D), F32),
        compiler_params=_cparams(("arbitrary",)),
        name="hgrn_out",
    )(o2_p, o2_p, o2_s, o2_s, gg, o_norm, w_o, x, mods_l)


def _router_kernel(x_ref, m_ref, g_ref, rt_ref, h8_ref, idx_ref, wt_ref, cnt_ref):
    m = m_ref[...]
    h = _normmod(x_ref[...], g_ref[...], m[4:5], m[3:4])
    tm = h.shape[0]
    for s in range(ROW_TILE):
        h8_ref[pl.ds(s, tm, stride=ROW_TILE), :] = h[:, s * LANES:(s + 1) * LANES]
    h1, h2, h3 = _split3(h)
    r1, r2, r3 = _split3(rt_ref[...])
    lt = (_dot_nt(r1, h1) + _dot_nt(r1, h2) + _dot_nt(r2, h1)
          + _dot_nt(r1, h3) + _dot_nt(r3, h1) + _dot_nt(r2, h2))
    lg = lt[:N_EXP]
    e = jnp.exp(lg - jnp.max(lg, axis=0, keepdims=True))
    p = e / jnp.sum(e, axis=0, keepdims=True)
    io = lax.broadcasted_iota(I32, p.shape, 0)
    m1 = jnp.max(p, axis=0, keepdims=True)
    i1 = jnp.min(jnp.where(p == m1, io, N_EXP), axis=0, keepdims=True)
    p2 = jnp.where(io == i1, -1.0, p)
    m2 = jnp.max(p2, axis=0, keepdims=True)
    i2 = jnp.min(jnp.where(p2 == m2, io, N_EXP), axis=0, keepdims=True)
    den = m1 + m2
    idx_ref[...] = jnp.concatenate([i1, i2], axis=0)
    wt_ref[...] = jnp.concatenate([m1 / den, m2 / den], axis=0)
    chosen = jnp.where(io == i1, 1.0, 0.0) + jnp.where(io == i2, 1.0, 0.0)

    @pl.when(pl.program_id(0) == 0)
    def _():
        cnt_ref[...] = jnp.zeros_like(cnt_ref)

    cnt_ref[...] += jnp.broadcast_to(jnp.sum(chosen, axis=1, keepdims=True), cnt_ref.shape)


def _router(x, mods_l, gain, router_t):
    return pl.pallas_call(
        _router_kernel,
        grid=(R // TM,),
        in_specs=[
            pl.BlockSpec((TM, D), lambda i: (i, 0)),
            pl.BlockSpec((None, N_MOD, D), lambda i: (_cond_idx(i, TM), 0, 0)),
            pl.BlockSpec((1, D), lambda i: (0, 0)),
            pl.BlockSpec((2 * SUB, D), lambda i: (0, 0)),
        ],
        out_specs=[
            pl.BlockSpec((TM * ROW_TILE, LANES), lambda i: (i, 0)),
            pl.BlockSpec((TOP_K, TM), lambda i: (0, i)),
            pl.BlockSpec((TOP_K, TM), lambda i: (0, i)),
            pl.BlockSpec((N_EXP, LANES), lambda i: (0, 0)),
        ],
        out_shape=[
            jax.ShapeDtypeStruct((R * ROW_TILE, LANES), F32),
            jax.ShapeDtypeStruct((TOP_K, R), I32),
            jax.ShapeDtypeStruct((TOP_K, R), F32),
            jax.ShapeDtypeStruct((N_EXP, LANES), F32),
        ],
        compiler_params=_cparams(("arbitrary",)),
        name="moe_router",
    )(x, mods_l, gain, router_t)


IDX_BLOCK = 1024


def _slot_kernel(e_ref, gs_ref, pos_ref, tri_scr, carry_scr):
    @pl.when(pl.program_id(0) == 0)
    def _():
        r = lax.broadcasted_iota(I32, tri_scr.shape, 0)
        c = lax.broadcasted_iota(I32, tri_scr.shape, 1)
        tri_scr[...] = jnp.where(r <= c, 1.0, 0.0).astype(BF16)
        carry_scr[...] = jnp.zeros_like(carry_scr)

    e = e_ref[...]
    io = lax.broadcasted_iota(I32, (2 * SUB, IDX_BLOCK), 0)
    onehot = jnp.where(io == e, 1.0, 0.0)
    cum = _dot(onehot.astype(BF16), tri_scr[...])
    carry = carry_scr[...]
    slot = cum - 1.0 + carry[:, 0:1] + gs_ref[...][:, 0:1]
    pos_ref[...] = jnp.sum(onehot * slot, axis=0, keepdims=True).astype(I32)
    carry_scr[...] = carry + jnp.broadcast_to(cum[:, IDX_BLOCK - 1:IDX_BLOCK], carry.shape)


def _slots(idx, group_start):
    n_blk = TOP_K * R // IDX_BLOCK
    gs = jnp.broadcast_to(
        jnp.concatenate([group_start, jnp.zeros((2 * SUB - N_EXP,), I32)]).astype(F32)[:, None], (2 * SUB, LANES))
    pos = pl.pallas_call(
        _slot_kernel,
        grid=(n_blk,),
        in_specs=[
            pl.BlockSpec((None, 1, IDX_BLOCK), lambda c: (c, 0, 0)),
            pl.BlockSpec((2 * SUB, LANES), lambda c: (0, 0)),
        ],
        out_specs=pl.BlockSpec((None, 1, IDX_BLOCK), lambda c: (c, 0, 0)),
        out_shape=jax.ShapeDtypeStruct((n_blk, 1, IDX_BLOCK), I32),
        scratch_shapes=[pltpu.VMEM((IDX_BLOCK, IDX_BLOCK), BF16), pltpu.VMEM((2 * SUB, LANES), F32)],
        compiler_params=_cparams(("arbitrary",)),
        name="moe_slots",
    )(idx.reshape(n_blk, 1, IDX_BLOCK), gs)
    return pos.reshape(TOP_K * R)


def _dispatch_kernel(p0_ref, p1_ref, h8_ref, init_ref, hs8_ref, sem):
    del init_ref
    n_tok = h8_ref.shape[0] // ROW_TILE
    for p_ref in (p0_ref, p1_ref):
        def issue(r, carry, p_ref=p_ref):
            pltpu.make_async_copy(
                h8_ref.at[pl.ds(pl.multiple_of(r * SUB, SUB), SUB)],
                hs8_ref.at[pl.ds(pl.multiple_of(p_ref[r] * SUB, SUB), SUB)],
                sem).start()
            return carry

        lax.fori_loop(0, n_tok, issue, 0)
    for _ in range(TOP_K):
        pltpu.make_async_copy(h8_ref, hs8_ref.at[pl.ds(0, n_tok * ROW_TILE)], sem).wait()


def _dispatch(pos, h8):
    nt = R // IDX_BLOCK
    zeros = jnp.zeros((N_SLOT * ROW_TILE, LANES), F32)
    return pl.pallas_call(
        _dispatch_kernel,
        grid=(nt,),
        in_specs=[
            pl.BlockSpec((IDX_BLOCK,), lambda i: (i,), memory_space=pltpu.SMEM),
            pl.BlockSpec((IDX_BLOCK,), lambda i: (nt + i,), memory_space=pltpu.SMEM),
            pl.BlockSpec((IDX_BLOCK * ROW_TILE, LANES), lambda i: (i, 0)),
            pl.BlockSpec(memory_space=pl.ANY),
        ],
        out_specs=pl.BlockSpec(memory_space=pl.ANY),
        out_shape=jax.ShapeDtypeStruct((N_SLOT * ROW_TILE, LANES), F32),
        scratch_shapes=[pltpu.SemaphoreType.DMA(())],
        input_output_aliases={3: 0},
        compiler_params=_cparams(("arbitrary",)),
        name="moe_dispatch",
    )(pos, pos, h8, zeros)


EXP_CHUNK = 1792
N_EXP_CHUNK = E_FF // EXP_CHUNK


def _expert_kernel(te_ref, tv_ref, x8_ref, w1_ref, w3_ref, w2_ref, y8_ref, xb_scr, acc_scr):
    i = pl.program_id(0)
    kc = pl.program_id(1)
    valid = tv_ref[i] == 1

    @pl.when(valid & (kc == 0))
    def _():
        for s in range(ROW_TILE):
            xb_scr[:, s * LANES:(s + 1) * LANES] = x8_ref[pl.ds(s, TM_MOE, stride=ROW_TILE), :].astype(BF16)
        acc_scr[...] = jnp.zeros_like(acc_scr)

    @pl.when(valid)
    def _():
        x = xb_scr[...]
        a = _dot(x, w1_ref[...])
        b = _dot(x, w3_ref[...])
        acc_scr[...] += _dot((_silu(a) * b).astype(BF16), w2_ref[...])

    @pl.when(valid & (kc == N_EXP_CHUNK - 1))
    def _():
        for s in range(ROW_TILE):
            y8_ref[pl.ds(s, TM_MOE, stride=ROW_TILE), :] = acc_scr[:, s * LANES:(s + 1) * LANES]

    @pl.when(jnp.logical_not(valid) & (kc == N_EXP_CHUNK - 1))
    def _():
        y8_ref[...] = jnp.zeros_like(y8_ref)


def _experts(tile_expert, tile_valid, hs8, w1, w3, w2):
    def kc_eff(kc, tv, i):
        return jnp.where(tv[i] == 1, kc, N_EXP_CHUNK - 1)

    grid_spec = pltpu.PrefetchScalarGridSpec(
        num_scalar_prefetch=2,
        grid=(N_TILE, N_EXP_CHUNK),
        in_specs=[
            pl.BlockSpec((TM_MOE * ROW_TILE, LANES), lambda i, kc, te, tv: (i, 0)),
            pl.BlockSpec((None, D, EXP_CHUNK), lambda i, kc, te, tv: (te[i], 0, kc_eff(kc, tv, i))),
            pl.BlockSpec((None, D, EXP_CHUNK), lambda i, kc, te, tv: (te[i], 0, kc_eff(kc, tv, i))),
            pl.BlockSpec((None, EXP_CHUNK, D), lambda i, kc, te, tv: (te[i], kc_eff(kc, tv, i), 0)),
        ],
        out_specs=pl.BlockSpec((TM_MOE * ROW_TILE, LANES), lambda i, kc, te, tv: (i, 0)),
        scratch_shapes=[pltpu.VMEM((TM_MOE, D), BF16), pltpu.VMEM((TM_MOE, D), F32)],
    )
    return pl.pallas_call(
        _expert_kernel,
        grid_spec=grid_spec,
        out_shape=jax.ShapeDtypeStruct((N_SLOT * ROW_TILE, LANES), F32),
        compiler_params=_cparams(("arbitrary", "arbitrary"), 56),
        name="moe_experts",
    )(tile_expert, tile_valid, hs8, w1, w3, w2)


TILES_PER_IDX_BLOCK = IDX_BLOCK // TM


def _moe_resid_kernel(p0_ref, p1_ref, y8_ref, wc_ref, x_ref, m_ref, fn_ref, o_ref, ya_scr, yb_scr, sems, *, final):
    i = pl.program_id(0)
    nt = pl.num_programs(0)
    tm = x_ref.shape[0]
    bufs = (ya_scr, yb_scr)

    def start(tile, b):
        off = (tile % TILES_PER_IDX_BLOCK) * tm
        for c, p_ref in enumerate((p0_ref, p1_ref)):
            def issue(r, carry, p_ref=p_ref, c=c):
                pltpu.make_async_copy(
                    y8_ref.at[pl.ds(pl.multiple_of(p_ref[off + r] * SUB, SUB), SUB)],
                    bufs[b].at[c, pl.ds(pl.multiple_of(r * SUB, SUB), SUB)],
                    sems.at[b]).start()
                return carry

            lax.fori_loop(0, tm, issue, 0)

    def finish(b):
        for c in range(TOP_K):
            pltpu.make_async_copy(y8_ref.at[pl.ds(0, tm * ROW_TILE)], bufs[b].at[c], sems.at[b]).wait()

    def rows(ref):
        return jnp.concatenate([ref[pl.ds(s, tm, stride=ROW_TILE), :] for s in range(ROW_TILE)], axis=1)

    def combine(b):
        w = wc_ref[...]
        f = w[:, 0:1] * rows(bufs[b].at[0]) + w[:, 1:2] * rows(bufs[b].at[1])
        xn = x_ref[...] + m_ref[...][5:6] * f
        if final:
            xn = _rms(xn, fn_ref[...])
        o_ref[...] = xn

    @pl.when(i == 0)
    def _():
        start(0, 0)

    for b in range(2):
        @pl.when(i % 2 == b)
        def _(b=b):
            @pl.when(i + 1 < nt)
            def _():
                start(i + 1, 1 - b)

            finish(b)
            combine(b)


def _moe_resid(pos, y8, wcol, x, mods_l, final_gain, final):
    nt = R // TM
    nblk = R // IDX_BLOCK

    def next_blk(i):
        return jnp.minimum(i + 1, nt - 1) // TILES_PER_IDX_BLOCK

    return pl.pallas_call(
        functools.partial(_moe_resid_kernel, final=final),
        grid=(nt,),
        in_specs=[
            pl.BlockSpec((IDX_BLOCK,), lambda i: (next_blk(i),), memory_space=pltpu.SMEM),
            pl.BlockSpec((IDX_BLOCK,), lambda i: (nblk + next_blk(i),), memory_space=pltpu.SMEM),
            pl.BlockSpec(memory_space=pl.ANY),
            pl.BlockSpec((TM, TOP_K), lambda i: (i, 0)),
            pl.BlockSpec((TM, D), lambda i: (i, 0)),
            pl.BlockSpec((None, N_MOD, D), lambda i: (_cond_idx(i, TM), 0, 0)),
            pl.BlockSpec((1, D), lambda i: (0, 0)),
        ],
        out_specs=pl.BlockSpec((TM, D), lambda i: (i, 0)),
        out_shape=jax.ShapeDtypeStruct((R, D), F32),
        scratch_shapes=[pltpu.VMEM((TOP_K, TM * ROW_TILE, LANES), F32), pltpu.VMEM((TOP_K, TM * ROW_TILE, LANES), F32),
                        pltpu.SemaphoreType.DMA((2,))],
        compiler_params=_cparams(("arbitrary",), 48),
        name="moe_resid",
    )(pos, pos, y8, wcol, x, mods_l, final_gain)


def _tile_tables(counts):
    padded = ((counts + TM_MOE - 1) // TM_MOE) * TM_MOE
    gend = jnp.cumsum(padded)
    tile_start = jnp.arange(N_TILE, dtype=I32) * TM_MOE
    te = jnp.sum((tile_start[:, None] >= gend[None, :]).astype(I32), axis=1)
    valid = tile_start < gend[-1]
    te_last = te[gend[-1] // TM_MOE - 1]
    te = jnp.minimum(jnp.where(valid, te, te_last), N_EXP - 1)
    return (gend - padded).astype(I32), te.astype(I32), valid.astype(I32)


def _rot_half(w):
    wa = w.reshape(w.shape[:-1] + (2, 2, ROPE_AXIS // 2))
    return jnp.stack([-wa[..., 1, :], wa[..., 0, :]], axis=-2).reshape(w.shape)


def _rope_tables(tm):
    rows = DEC_SEQ // GRID_W
    r = jnp.repeat(jnp.arange(rows), GRID_W).astype(F32)
    c = jnp.tile(jnp.arange(GRID_W), rows).astype(F32)
    inv = ROPE_THETA ** (-jnp.arange(0, ROPE_AXIS, 2, dtype=F32) / ROPE_AXIS)
    ang_r = r[:, None] * inv
    ang_c = c[:, None] * inv
    ang = jnp.concatenate([ang_r, ang_r, ang_c, ang_c], axis=-1)
    cos, sin = jnp.cos(ang), jnp.sin(ang)
    z32 = jnp.zeros_like(cos)
    one = jnp.ones((tm, QK_ROPE), F32)
    zt = jnp.zeros((tm, QK_ROPE), F32)
    tq_c = jnp.concatenate([jnp.concatenate([cos, cos, z32, z32], 1), jnp.concatenate([one, one, zt, zt], 1)], 0)
    tq_s = jnp.concatenate([jnp.concatenate([sin, sin, z32, z32], 1), jnp.zeros((tm, LANES), F32)], 0)
    tk_c = jnp.concatenate([jnp.concatenate([cos, z32, z32, z32], 1), jnp.concatenate([one, zt, zt, zt], 1)], 0)
    tk_s = jnp.concatenate([jnp.concatenate([sin, z32, z32, z32], 1), jnp.zeros((tm, LANES), F32)], 0)
    return tq_c, tq_s, tk_c, tk_s


def _mla_weights(w_dq, w_uq, w_dkv, w_uk, w_uv, w_o):
    kr_w = w_dkv[:, KV_LORA:]
    w_a = jnp.concatenate(
        [w_dq, w_dkv[:, :KV_LORA], kr_w, _rot_half(kr_w), jnp.zeros((D, LANES - 2 * QK_ROPE), F32)], axis=1)
    uq = w_uq.reshape(Q_LORA, N_PAIR, 2, QK_NOPE + QK_ROPE)
    nope = uq[..., :QK_NOPE].reshape(Q_LORA, N_PAIR, 2 * QK_NOPE)
    rope = uq[..., QK_NOPE:]
    w_uq2 = jnp.concatenate(
        [nope, rope.reshape(Q_LORA, N_PAIR, 2 * QK_ROPE), _rot_half(rope).reshape(Q_LORA, N_PAIR, 2 * QK_ROPE)],
        axis=-1).reshape(Q_LORA, N_PAIR * 2 * LANES)
    uk = w_uk.reshape(KV_LORA, N_PAIR, 2 * QK_NOPE)
    wk = jnp.concatenate([uk, jnp.zeros((KV_LORA, N_PAIR, LANES), F32)], axis=-1).reshape(KV_LORA, N_PAIR * 2 * LANES)
    eye = jnp.eye(LANES, QK_ROPE, dtype=F32)
    pair = jnp.concatenate([jnp.zeros((LANES, LANES), F32), eye, eye, jnp.zeros((LANES, LANES - 2 * QK_ROPE), F32)], 1)
    wkr = jnp.tile(pair, (1, N_PAIR))
    wv = w_uv.reshape(KV_LORA, HEADS * V_HEAD)
    wo = w_o.reshape(HEADS * V_HEAD, D)
    return [w.astype(BF16) for w in (w_a, w_uq2, wk, wkr, wv, wo)]


def kernel(x_prompt, x_sample, cache_ckv, cache_krope, state_hgrn, c, c_ctx, w_ada, b_ada, norm_mix, norm_ffn, mla_w_dq, mla_q_norm, mla_w_uq, mla_w_dkv, mla_kv_norm, mla_w_uk, mla_w_uv, mla_w_o, hg_w_q, hg_w_f, hg_w_i, hg_w_g, hg_lb_logits, hg_o_norm, hg_w_o, ffn_w1, ffn_w3, ffn_w2, moe_router, moe_w1, moe_w3, moe_w2, final_norm):
    x = jnp.concatenate([x_prompt.reshape(R_P, D), x_sample.reshape(R_S, D)], axis=0)
    cond = jnp.concatenate([c, c_ctx[None], jnp.zeros((COND_PAD - N_COND, D), F32)], axis=0)
    mods = _ada(cond, w_ada, b_ada)
    lb_all = _lower_bounds(hg_lb_logits)
    tq_c, tq_s, tk_c, tk_s = _rope_tables(TM)
    new_ckv, new_krope, new_hgrn = [], [], []

    for l in range(DEPTH):
        j = l // 2
        mods_l = mods[l]
        gain_mix = norm_mix[l][None]
        gain_ffn = norm_ffn[l][None]
        if l % 2 == 0:
            w_a, w_uq2, wk, wkr, wv, wo = _mla_weights(
                mla_w_dq[j], mla_w_uq[j], mla_w_dkv[j], mla_w_uk[j], mla_w_uv[j], mla_w_o[j])
            cq, ckv, kr = _mla_proj(x, mods_l, gain_mix, w_a, mla_q_norm[j][None], mla_kv_norm[j][None], tk_c, tk_s)
            q2 = _q_up(cq, w_uq2, tq_c, tq_s)
            k2_p, v_p = _kv_up(ckv, kr, R_P, wk, wkr, wv)
            ckv_s = jnp.concatenate([cache_ckv[:, j], ckv[R_P:].reshape(DEC_BATCH, DEC_SEQ, KV_LORA)], axis=1)
            kr_cache = jnp.pad(cache_krope[:, j], ((0, 0), (0, 0), (0, LANES - QK_ROPE)))
            kr_s = jnp.concatenate([kr_cache, kr[R_P:].reshape(DEC_BATCH, DEC_SEQ, LANES)], axis=1)
            s_all = PAST + DEC_SEQ
            k2_s, v_s = _kv_up(ckv_s.reshape(DEC_BATCH * s_all, KV_LORA), kr_s.reshape(DEC_BATCH * s_all, LANES),
                               DEC_BATCH * s_all, wk, wkr, wv)
            o_p = _attention(q2, k2_p, v_p, BATCH, SEQ, SEQ, 0, SEQ, 4)
            o_s = _attention(q2, k2_s, v_s, DEC_BATCH, DEC_SEQ, s_all, R_P, 512, 1)
            x = _mm_resid(o_p, o_s, wo, x, mods_l, 2)
            new_ckv.append(ckv[:R_P].reshape(BATCH, SEQ, KV_LORA))
            new_krope.append(kr[:R_P, :QK_ROPE].reshape(BATCH, SEQ, QK_ROPE))
        else:
            w5 = jnp.stack([hg_w_q[j], hg_w_f[j, 0], hg_w_f[j, 1], hg_w_i[j], hg_w_g[j]]).astype(BF16)
            q, k, lf, v, gg = _hg_proj(x, mods_l, gain_mix, w5, lb_all[j])
            s_zero = jnp.zeros((BATCH, 2, HG_H, HG_DK, HG_DK), F32)
            o2_p, st_p = _gla(q, k, lf, v, s_zero, BATCH, SEQ, 0)
            o2_s, _ = _gla(q, k, lf, v, state_hgrn[:, j], DEC_BATCH, DEC_SEQ, R_P)
            x = _hg_out(o2_p, o2_s, gg, hg_o_norm[j][None], hg_w_o[j].astype(BF16), x, mods_l)
            new_hgrn.append(st_p)
        if l % 2 == 0:
            x = _ffn(x, mods_l, gain_ffn, ffn_w1[j].astype(BF16), ffn_w3[j].astype(BF16), ffn_w2[j].astype(BF16))
        else:
            router_t = jnp.concatenate([moe_router[j].T, jnp.zeros((2 * SUB - N_EXP, D), F32)], axis=0)
            h8, idx, wts, cnt = _router(x, mods_l, gain_ffn, router_t)
            group_start, tile_expert, tile_valid = _tile_tables(cnt[:, 0].astype(I32))
            pos = _slots(idx, group_start)
            y8 = _experts(tile_expert, tile_valid, _dispatch(pos, h8),
                          moe_w1[j].astype(BF16), moe_w3[j].astype(BF16), moe_w2[j].astype(BF16))
            x = _moe_resid(pos, y8, wts.T, x, mods_l, final_norm[None], final=(l == DEPTH - 1))

    y_prompt = x[:R_P].reshape(BATCH, SEQ, D)
    y_sample = x[R_P:].reshape(DEC_BATCH, DEC_SEQ, D)
    return (y_prompt, y_sample, jnp.stack(new_ckv, axis=1), jnp.stack(new_krope, axis=1),
            jnp.stack(new_hgrn, axis=1))
```

```python
import functools
import math

import jax
import jax.numpy as jnp
from jax import lax
from jax.experimental import pallas as pl
from jax.experimental.pallas import tpu as pltpu

F32 = jnp.float32
BF16 = jnp.bfloat16
I32 = jnp.int32

D = 1024
BATCH, SEQ = 32, 256
DEC_BATCH, DEC_SEQ = 8, 1024
PAST = 256
DEPTH = 4
R_P = BATCH * SEQ
R_S = DEC_BATCH * DEC_SEQ
R = R_P + R_S
N_COND = DEC_BATCH + 1
COND_PAD = 16
N_MOD = 6
EPS = 1e-6
GRID_W = 64
HEADS, QK_NOPE, QK_ROPE, V_HEAD = 16, 64, 32, 64
Q_LORA, KV_LORA = 512, 256
N_PAIR = HEADS // 2
ROPE_AXIS = QK_ROPE // 2
ROPE_THETA = 10000.0
HG_H, HG_DK = 8, 128
BLK = 128
CHUNK = 16
D_FF = 2816
N_EXP, TOP_K, E_FF = 8, 2, 3584
LANES = 128
SUB = 8
ROW_TILE = D // LANES
TM = 512
TM_MOE = 512
N_SLOT = TOP_K * R + N_EXP * TM_MOE
N_TILE = N_SLOT // TM_MOE
MIB = 1024 * 1024
assert DEPTH % 2 == 0


def _cparams(sem, vmem_mib=40):
    return pltpu.CompilerParams(dimension_semantics=sem, vmem_limit_bytes=vmem_mib * MIB)


def _cond_idx(i, tm):
    r0 = i * tm
    return jnp.where(r0 < R_P, DEC_BATCH, (r0 - R_P) // DEC_SEQ)


def _rope_idx(i, tm):
    nb = DEC_SEQ // tm
    r0 = i * tm
    return jnp.where(r0 < R_P, nb, ((r0 - R_P) // tm) % nb)


def _normmod(x, gain, scale, shift):
    ms = jnp.mean(x * x, axis=-1, keepdims=True)
    return (x * lax.rsqrt(ms + EPS)) * gain * (1.0 + scale) + shift


def _rms(x, gain):
    ms = jnp.mean(x * x, axis=-1, keepdims=True)
    return (x * lax.rsqrt(ms + EPS)) * gain


def _silu(x):
    return x / (1.0 + jnp.exp(-x))


def _dot(a, b):
    return jnp.dot(a, b, preferred_element_type=F32)


def _dot_nt(a, b):
    return lax.dot_general(a, b, (((1,), (1,)), ((), ())), preferred_element_type=F32)


def _split3(x):
    hi = x.astype(BF16)
    r1 = x - hi.astype(F32)
    mid = r1.astype(BF16)
    lo = (r1 - mid.astype(F32)).astype(BF16)
    return hi, mid, lo


def _ada_kernel(c_ref, w_ref, b_ref, o_ref):
    o_ref[...] = _dot(_silu(c_ref[...]), w_ref[...]) + b_ref[...]


def _ada(cond, w_ada, b_ada):
    tn = 1536
    out = pl.pallas_call(
        _ada_kernel,
        grid=(DEPTH, N_MOD * D // tn),
        in_specs=[
            pl.BlockSpec((COND_PAD, D), lambda l, j: (0, 0)),
            pl.BlockSpec((None, D, tn), lambda l, j: (l, 0, j)),
            pl.BlockSpec((None, 1, tn), lambda l, j: (l, 0, j)),
        ],
        out_specs=pl.BlockSpec((None, COND_PAD, tn), lambda l, j: (l, 0, j)),
        out_shape=jax.ShapeDtypeStruct((DEPTH, COND_PAD, N_MOD * D), F32),
        compiler_params=_cparams(("arbitrary", "arbitrary")),
        name="ada",
    )(cond, w_ada, b_ada.reshape(DEPTH, 1, N_MOD * D))
    return out[:, :N_COND].reshape(DEPTH, N_COND, N_MOD, D)


def _lb_kernel(x_ref, o_ref):
    x = x_ref[...]
    e = jnp.exp(x - jnp.max(x, axis=0, keepdims=True))
    sm = e / jnp.sum(e, axis=0, keepdims=True)
    n = x.shape[0]
    cum = sm[0]
    o_ref[0] = jnp.zeros_like(cum)
    for i in range(1, n):
        cum_i = cum + sm[i]
        o_ref[i] = cum_i - sm[0]
        cum = cum_i


def _lower_bounds(lb_logits):
    return pl.pallas_call(
        _lb_kernel,
        out_shape=jax.ShapeDtypeStruct(lb_logits.shape, F32),
        name="hgrn_lower_bounds",
    )(lb_logits)


def _two_part_rows(x):
    if isinstance(x, tuple):
        x_p, x_s = x
        s_off = 0
    else:
        x_p = x_s = x
        s_off = N_TILE_P
    w = x_p.shape[1]
    specs = [pl.BlockSpec((TM, w), lambda i: (_prompt_blk(i), 0)),
             pl.BlockSpec((TM, w), lambda i: (s_off + _sample_blk(i), 0))]
    return [x_p, x_s], specs


def _pick_rows(p_ref, s_ref):
    return jnp.where(pl.program_id(0) < N_TILE_P, p_ref[...], s_ref[...])


def _mla_proj_kernel(xp_ref, xs_ref, m_ref, g_ref, w_ref, qn_ref, kvn_ref, tc_ref, ts_ref,
                     cq_ref, ckv_ref, kr_ref):
    m = m_ref[...]
    h = _normmod(_pick_rows(xp_ref, xs_ref), g_ref[...], m[1:2], m[0:1]).astype(BF16)
    y = _dot(h, w_ref[...])
    cq_ref[...] = _rms(y[:, :Q_LORA], qn_ref[...]).astype(BF16)
    ckv_ref[...] = _rms(y[:, Q_LORA:Q_LORA + KV_LORA], kvn_ref[...])
    slab = y[:, Q_LORA + KV_LORA:]
    kr_ref[...] = slab * tc_ref[...] + pltpu.roll(slab, LANES - QK_ROPE, 1) * ts_ref[...]


def _mla_proj(x, mods_l, gain, w_a, q_norm, kv_norm, tk_c, tk_s):
    n_a = w_a.shape[1]
    row = lambda i: (i, 0)
    fixed = lambda i: (0, 0)
    x_args, x_specs = _two_part_rows(x)
    return pl.pallas_call(
        _mla_proj_kernel,
        grid=(R // TM,),
        in_specs=x_specs + [
            pl.BlockSpec((None, N_MOD, D), lambda i: (_cond_idx(i, TM), 0, 0)),
            pl.BlockSpec((1, D), fixed),
            pl.BlockSpec((D, n_a), fixed),
            pl.BlockSpec((1, Q_LORA), fixed),
            pl.BlockSpec((1, KV_LORA), fixed),
            pl.BlockSpec((TM, LANES), lambda i: (_rope_idx(i, TM), 0)),
            pl.BlockSpec((TM, LANES), lambda i: (_rope_idx(i, TM), 0)),
        ],
        out_specs=[
            pl.BlockSpec((TM, Q_LORA), row),
            pl.BlockSpec((TM, KV_LORA), row),
            pl.BlockSpec((TM, LANES), row),
        ],
        out_shape=[
            jax.ShapeDtypeStruct((R, Q_LORA), BF16),
            jax.ShapeDtypeStruct((R, KV_LORA), F32),
            jax.ShapeDtypeStruct((R, LANES), F32),
        ],
        compiler_params=_cparams(("arbitrary",)),
        name="mla_proj",
    )(*x_args, mods_l, gain, w_a, q_norm, kv_norm, tk_c, tk_s)


def _q_up_kernel(cq_ref, w_ref, tc_ref, ts_ref, q_ref):
    y = _dot(cq_ref[...], w_ref[...])
    tc = tc_ref[...]
    ts = ts_ref[...]
    for p in range(N_PAIR):
        lo = p * 2 * LANES
        q_ref[:, lo:lo + LANES] = y[:, lo:lo + LANES].astype(BF16)
        hi = y[:, lo + LANES:lo + 2 * LANES]
        q_ref[:, lo + LANES:lo + 2 * LANES] = (hi * tc + pltpu.roll(hi, LANES // 2, 1) * ts).astype(BF16)


def _q_up(cq, w_uq2, tq_c, tq_s):
    nq = w_uq2.shape[1]
    return pl.pallas_call(
        _q_up_kernel,
        grid=(R // TM,),
        in_specs=[
            pl.BlockSpec((TM, Q_LORA), lambda i: (i, 0)),
            pl.BlockSpec((Q_LORA, nq), lambda i: (0, 0)),
            pl.BlockSpec((TM, LANES), lambda i: (_rope_idx(i, TM), 0)),
            pl.BlockSpec((TM, LANES), lambda i: (_rope_idx(i, TM), 0)),
        ],
        out_specs=pl.BlockSpec((TM, nq), lambda i: (i, 0)),
        out_shape=jax.ShapeDtypeStruct((R, nq), BF16),
        compiler_params=_cparams(("arbitrary",)),
        name="mla_q_up",
    )(cq, w_uq2, tq_c, tq_s)


def _kv_up_kernel(c_ref, r_ref, wk_ref, wkr_ref, wv_ref, k_ref, v_ref):
    c = c_ref[...].astype(BF16)
    r = r_ref[...].astype(BF16)
    k_ref[...] = (_dot(c, wk_ref[...]) + _dot(r, wkr_ref[...])).astype(BF16)
    v_ref[...] = _dot(c, wv_ref[...]).astype(BF16)


def _kv_up(ckv, kr, n_rows, wk, wkr, wv):
    nk = wk.shape[1]
    nv = wv.shape[1]
    fixed = lambda i: (0, 0)
    return pl.pallas_call(
        _kv_up_kernel,
        grid=(n_rows // TM,),
        in_specs=[
            pl.BlockSpec((TM, KV_LORA), lambda i: (i, 0)),
            pl.BlockSpec((TM, LANES), lambda i: (i, 0)),
            pl.BlockSpec((KV_LORA, nk), fixed),
            pl.BlockSpec((LANES, nk), fixed),
            pl.BlockSpec((KV_LORA, nv), fixed),
        ],
        out_specs=[
            pl.BlockSpec((TM, nk), lambda i: (i, 0)),
            pl.BlockSpec((TM, nv), lambda i: (i, 0)),
        ],
        out_shape=[
            jax.ShapeDtypeStruct((n_rows, nk), BF16),
            jax.ShapeDtypeStruct((n_rows, nv), BF16),
        ],
        compiler_params=_cparams(("arbitrary",)),
        name="mla_kv_up",
    )(ckv, kr, wk, wkr, wv)


def _attn_kernel(q_ref, k_ref, v_ref, *rest, n_pair):
    o_ref = rest[-1]
    c = math.log2(math.e) / math.sqrt(QK_NOPE + QK_ROPE)
    lq = lax.broadcasted_iota(I32, (1, 2 * LANES), 1)
    lv = lax.broadcasted_iota(I32, (1, LANES), 1)
    sel_a = (lq < QK_NOPE) | ((lq >= LANES) & (lq < LANES + QK_ROPE))
    sel_b = ((lq >= QK_NOPE) & (lq < LANES)) | ((lq >= LANES + QK_ROPE) & (lq < LANES + 2 * QK_ROPE))
    for p in range(n_pair):
        q = q_ref[:, p * 2 * LANES:(p + 1) * 2 * LANES]
        k = k_ref[:, p * 2 * LANES:(p + 1) * 2 * LANES]
        v = v_ref[:, p * LANES:(p + 1) * LANES]
        zq = jnp.zeros_like(q)
        zv = jnp.zeros_like(v)
        out = None
        for sel, vsel in ((sel_a, lv < V_HEAD), (sel_b, lv >= V_HEAD)):
            s = _dot_nt(jnp.where(sel, q, zq), k) * c
            e = jnp.exp2(s - jnp.max(s, axis=-1, keepdims=True))
            den = jnp.sum(e, axis=-1, keepdims=True)
            o = _dot(e.astype(BF16), jnp.where(vsel, v, zv)) / den
            out = o if out is None else out + o
        o_ref[:, p * LANES:(p + 1) * LANES] = out.astype(BF16)


def _attention(q2, k2, v, n_batch, t_len, s_len, q_row0, tq, n_pair):
    nq = t_len // tq
    qb0 = q_row0 // tq
    return pl.pallas_call(
        functools.partial(_attn_kernel, n_pair=n_pair),
        grid=(n_batch, N_PAIR // n_pair, nq),
        in_specs=[
            pl.BlockSpec((tq, n_pair * 2 * LANES), lambda b, p, i: (qb0 + b * nq + i, p)),
            pl.BlockSpec((s_len, n_pair * 2 * LANES), lambda b, p, i: (b, p)),
            pl.BlockSpec((s_len, n_pair * LANES), lambda b, p, i: (b, p)),
        ],
        out_specs=pl.BlockSpec((tq, n_pair * LANES), lambda b, p, i: (b * nq + i, p)),
        out_shape=jax.ShapeDtypeStruct((n_batch * t_len, HEADS * V_HEAD), BF16),
        compiler_params=_cparams(("arbitrary", "arbitrary", "arbitrary")),
        name="mla_attention",
    )(q2, k2, v)


N_TILE_P = R_P // TM


def _prompt_blk(i):
    return jnp.minimum(i, N_TILE_P - 1)


def _sample_blk(i):
    return jnp.maximum(i - N_TILE_P, 0)


def _mm_resid_kernel(ap_ref, as_ref, w_ref, xp_ref, xs_ref, m_ref, o_ref, *, gate_idx):
    gate = m_ref[...][gate_idx:gate_idx + 1]
    o_ref[...] = _pick_rows(xp_ref, xs_ref) + gate * _dot(_pick_rows(ap_ref, as_ref), w_ref[...])


def _mm_resid(a, w, x, mods_l, gate_idx):
    a_args, a_specs = _two_part_rows(a)
    x_args, x_specs = _two_part_rows(x)
    k = a_args[0].shape[1]
    return pl.pallas_call(
        functools.partial(_mm_resid_kernel, gate_idx=gate_idx),
        grid=(R // TM,),
        in_specs=a_specs + [pl.BlockSpec((k, D), lambda i: (0, 0))] + x_specs + [
            pl.BlockSpec((None, N_MOD, D), lambda i: (_cond_idx(i, TM), 0, 0)),
        ],
        out_specs=pl.BlockSpec((TM, D), lambda i: (i, 0)),
        out_shape=jax.ShapeDtypeStruct((R, D), F32),
        compiler_params=_cparams(("arbitrary",)),
        name="mm_resid",
    )(*a_args, w, *x_args, mods_l)


FFN_CHUNK = 1408


def _ffn_kernel(x_ref, m_ref, g_ref, w1_ref, w3_ref, w2_ref, o_ref):
    x = x_ref[...]
    m = m_ref[...]
    h = _normmod(x, g_ref[...], m[4:5], m[3:4]).astype(BF16)
    acc = jnp.zeros(x.shape, F32)
    for c in range(D_FF // FFN_CHUNK):
        sl = slice(c * FFN_CHUNK, (c + 1) * FFN_CHUNK)
        a = _dot(h, w1_ref[:, sl])
        b = _dot(h, w3_ref[:, sl])
        acc = acc + _dot((_silu(a) * b).astype(BF16), w2_ref[sl, :])
    o_ref[...] = x + m[5:6] * acc


def _ffn(x, mods_l, gain, w1, w3, w2):
    tm = 256
    fixed = lambda i: (0, 0)
    return pl.pallas_call(
        _ffn_kernel,
        grid=(R // tm,),
        in_specs=[
            pl.BlockSpec((tm, D), lambda i: (i, 0)),
            pl.BlockSpec((None, N_MOD, D), lambda i: (_cond_idx(i, tm), 0, 0)),
            pl.BlockSpec((1, D), fixed),
            pl.BlockSpec((D, D_FF), fixed),
            pl.BlockSpec((D, D_FF), fixed),
            pl.BlockSpec((D_FF, D), fixed),
        ],
        out_specs=pl.BlockSpec((tm, D), lambda i: (i, 0)),
        out_shape=jax.ShapeDtypeStruct((R, D), F32),
        compiler_params=_cparams(("arbitrary",), 56),
        name="dense_swiglu",
    )(x, mods_l, gain, w1, w3, w2)


def _forget_gate(z, lb):
    e = jnp.exp(-jnp.abs(z))
    log_sig = jnp.minimum(z, 0.0) - jnp.log(1.0 + e)
    a = jnp.log(lb)
    b = jnp.log1p(-lb) + log_sig
    log_f = jnp.maximum(a, b) + jnp.log(1.0 + jnp.exp(-jnp.abs(a - b)))
    k = (1.0 - lb) * (jnp.where(z >= 0, e, 1.0) / (1.0 + e))
    return k, log_f


def _hg_proj_kernel(x_ref, m_ref, g_ref, w_ref, lb_ref, q_ref, k_ref, lf_ref, v_ref, gg_ref):
    m = m_ref[...]
    h = _normmod(x_ref[...], g_ref[...], m[1:2], m[0:1]).astype(BF16)
    q_ref[...] = _dot(h, w_ref[0])
    for dr in range(2):
        k, log_f = _forget_gate(_dot(h, w_ref[1 + dr]), lb_ref[dr])
        k_ref[dr] = k
        lf_ref[dr] = log_f
    v_ref[...] = _dot(h, w_ref[3]).astype(BF16)
    gg_ref[...] = _dot(h, w_ref[4])


def _hg_proj(x, mods_l, gain, w5, lb):
    tm = 256
    row = lambda i: (i, 0)
    dirs = lambda i: (0, i, 0)
    return pl.pallas_call(
        _hg_proj_kernel,
        grid=(R // tm,),
        in_specs=[
            pl.BlockSpec((tm, D), row),
            pl.BlockSpec((None, N_MOD, D), lambda i: (_cond_idx(i, tm), 0, 0)),
            pl.BlockSpec((1, D), lambda i: (0, 0)),
            pl.BlockSpec((5, D, D), lambda i: (0, 0, 0)),
            pl.BlockSpec((2, 1, D), lambda i: (0, 0, 0)),
        ],
        out_specs=[
            pl.BlockSpec((tm, D), row),
            pl.BlockSpec((2, tm, D), dirs),
            pl.BlockSpec((2, tm, D), dirs),
            pl.BlockSpec((tm, D), row),
            pl.BlockSpec((tm, D), row),
        ],
        out_shape=[
            jax.ShapeDtypeStruct((R, D), F32),
            jax.ShapeDtypeStruct((2, R, D), F32),
            jax.ShapeDtypeStruct((2, R, D), F32),
            jax.ShapeDtypeStruct((R, D), BF16),
            jax.ShapeDtypeStruct((R, D), F32),
        ],
        compiler_params=_cparams(("arbitrary",), 48),
        name="hgrn_proj",
    )(x, mods_l, gain, w5, lb.reshape(2, 1, D))


LEVEL_HALVES = (64, 32, 16)


def _gla_kernel(q_ref, k_ref, lf_ref, v_ref, s0_ref, *rest, nb):
    o_ref, sn_ref, st_scr, b_scr = rest[-4:]
    d = pl.program_id(1)
    n = pl.program_id(2)

    @pl.when(n == 0)
    def _():
        for h in range(HG_H):
            st_scr[h] = s0_ref[h].T

    row = lax.broadcasted_iota(I32, (BLK, BLK), 0)
    col = lax.broadcasted_iota(I32, (BLK, BLK), 1)
    ut = row + d * (BLK - 1 - 2 * row)
    us = col + d * (BLK - 1 - 2 * col)
    causal = us <= ut
    tri = jnp.where(causal, 1.0, 0.0).astype(BF16)

    hi, mid, lo = _split3(lf_ref[...])
    b_scr[...] = _dot(tri, hi) + _dot(tri, mid) + _dot(tri, lo)

    level_masks = []
    for hs in LEVEL_HALVES:
        sh = int(math.log2(2 * hs))
        same = (ut >> sh) == (us >> sh)
        level_masks.append(same & ((ut & (2 * hs - 1)) >= hs) & ((us & (2 * hs - 1)) < hs))
    base_mask = ((ut >> 4) == (us >> 4)) & causal

    def split_rows(sl, half):
        parts = []
        for jr in range(BLK // (2 * half)):
            r0 = jr * 2 * half + half - 1
            r = jnp.where(d == 0, b_scr[r0:r0 + 1, sl], b_scr[r0 + 1:r0 + 2, sl])
            parts.append(jnp.broadcast_to(r, (2 * half, HG_DK)))
        return parts[0] if len(parts) == 1 else jnp.concatenate(parts, axis=0)

    for h in range(HG_H):
        sl = slice(h * HG_DK, (h + 1) * HG_DK)
        bh = b_scr[:, sl]
        q = q_ref[:, sl]
        k = k_ref[:, sl]
        v = v_ref[:, sl]

        xq = bh - split_rows(sl, CHUNK // 2)
        att = jnp.where(base_mask, _dot_nt((q * jnp.exp(xq)).astype(BF16), (k * jnp.exp(-xq)).astype(BF16)), 0.0)
        for hs, msk in zip(LEVEL_HALVES, level_masks):
            e = jnp.exp(-jnp.abs(bh - split_rows(sl, hs)))
            att = jnp.where(msk, _dot_nt((q * e).astype(BF16), (k * e).astype(BF16)), att)

        b_last = jnp.where(d == 0, b_scr[BLK - 1:BLK, sl], b_scr[0:1, sl])
        q_in = (q * jnp.exp(bh)).astype(BF16)
        k_in = (k * jnp.exp(b_last - bh)).astype(BF16)
        st = st_scr[h]
        o_ref[:, sl] = _dot(att.astype(BF16), v) + _dot_nt(q_in, st.astype(BF16))
        v_t = v.astype(F32).T.astype(BF16)
        st_scr[h] = st * jnp.exp(b_last) + _dot(v_t, k_in)

    @pl.when(n == nb - 1)
    def _():
        for h in range(HG_H):
            sn_ref[h] = st_scr[h].T


def _gla(q, k, lf, v, s0, n_batch, t_len, row0):
    nb = t_len // BLK
    rb0 = row0 // BLK

    def rb(b, d, n):
        return b * nb + n + d * (nb - 1 - 2 * n)

    return pl.pallas_call(
        functools.partial(_gla_kernel, nb=nb),
        grid=(n_batch, 2, nb),
        in_specs=[
            pl.BlockSpec((BLK, D), lambda b, d, n: (rb0 + rb(b, d, n), 0)),
            pl.BlockSpec((None, BLK, D), lambda b, d, n: (d, rb0 + rb(b, d, n), 0)),
            pl.BlockSpec((None, BLK, D), lambda b, d, n: (d, rb0 + rb(b, d, n), 0)),
            pl.BlockSpec((BLK, D), lambda b, d, n: (rb0 + rb(b, d, n), 0)),
            pl.BlockSpec((None, None, HG_H, HG_DK, HG_DK), lambda b, d, n: (b, d, 0, 0, 0)),
        ],
        out_specs=[
            pl.BlockSpec((None, BLK, D), lambda b, d, n: (d, rb(b, d, n), 0)),
            pl.BlockSpec((None, None, HG_H, HG_DK, HG_DK), lambda b, d, n: (b, d, 0, 0, 0)),
        ],
        out_shape=[
            jax.ShapeDtypeStruct((2, n_batch * t_len, D), F32),
            jax.ShapeDtypeStruct((n_batch, 2, HG_H, HG_DK, HG_DK), F32),
        ],
        scratch_shapes=[pltpu.VMEM((HG_H, HG_DK, HG_DK), F32), pltpu.VMEM((BLK, D), F32)],
        compiler_params=_cparams(("arbitrary", "arbitrary", "arbitrary")),
        name="hgrn_recurrence",
    )(q, k, lf, v, s0)


def _hg_out_kernel(ofp_ref, obp_ref, ofs_ref, obs_ref, gg_ref, on_ref, w_ref, x_ref, m_ref, o_ref):
    o = jnp.where(pl.program_id(0) < N_TILE_P, ofp_ref[...] + obp_ref[...], ofs_ref[...] + obs_ref[...])
    gain = on_ref[...]
    parts = []
    for h in range(HG_H):
        parts.append(_rms(o[:, h * HG_DK:(h + 1) * HG_DK], gain))
    a = (jnp.concatenate(parts, axis=1) * _silu(gg_ref[...])).astype(BF16)
    o_ref[...] = x_ref[...] + m_ref[...][2:3] * _dot(a, w_ref[...])


def _hg_out(o2_p, o2_s, gg, o_norm, w_o, x, mods_l):
    fixed = lambda i: (0, 0)
    return pl.pallas_call(
        _hg_out_kernel,
        grid=(R // TM,),
        in_specs=[
            pl.BlockSpec((None, TM, D), lambda i: (0, _prompt_blk(i), 0)),
            pl.BlockSpec((None, TM, D), lambda i: (1, _prompt_blk(i), 0)),
            pl.BlockSpec((None, TM, D), lambda i: (0, _sample_blk(i), 0)),
            pl.BlockSpec((None, TM, D), lambda i: (1, _sample_blk(i), 0)),
            pl.BlockSpec((TM, D), lambda i: (i, 0)),
            pl.BlockSpec((1, HG_DK), fixed),
            pl.BlockSpec((D, D), fixed),
            pl.BlockSpec((TM, D), lambda i: (i, 0)),
            pl.BlockSpec((None, N_MOD, D), lambda i: (_cond_idx(i, TM), 0, 0)),
        ],
        out_specs=pl.BlockSpec((TM, D), lambda i: (i, 0)),
        out_shape=jax.ShapeDtypeStruct((R, D), F32),
        compiler_params=_cparams(("arbitrary",)),
        name="hgrn_out",
    )(o2_p, o2_p, o2_s, o2_s, gg, o_norm, w_o, x, mods_l)


def _router_kernel(x_ref, m_ref, g_ref, rt_ref, h8_ref, idx_ref, wt_ref, cnt_ref):
    m = m_ref[...]
    h = _normmod(x_ref[...], g_ref[...], m[4:5], m[3:4])
    tm = h.shape[0]
    for s in range(ROW_TILE):
        h8_ref[pl.ds(s, tm, stride=ROW_TILE), :] = h[:, s * LANES:(s + 1) * LANES]
    h1, h2, h3 = _split3(h)
    r1, r2, r3 = _split3(rt_ref[...])
    lt = (_dot_nt(r1, h1) + _dot_nt(r1, h2) + _dot_nt(r2, h1)
          + _dot_nt(r1, h3) + _dot_nt(r3, h1) + _dot_nt(r2, h2))
    lg = lt[:N_EXP]
    e = jnp.exp(lg - jnp.max(lg, axis=0, keepdims=True))
    p = e / jnp.sum(e, axis=0, keepdims=True)
    io = lax.broadcasted_iota(I32, p.shape, 0)
    m1 = jnp.max(p, axis=0, keepdims=True)
    i1 = jnp.min(jnp.where(p == m1, io, N_EXP), axis=0, keepdims=True)
    p2 = jnp.where(io == i1, -1.0, p)
    m2 = jnp.max(p2, axis=0, keepdims=True)
    i2 = jnp.min(jnp.where(p2 == m2, io, N_EXP), axis=0, keepdims=True)
    den = m1 + m2
    idx_ref[...] = jnp.concatenate([i1, i2], axis=0)
    wt_ref[...] = jnp.concatenate([m1 / den, m2 / den], axis=0)
    chosen = jnp.where(io == i1, 1.0, 0.0) + jnp.where(io == i2, 1.0, 0.0)

    @pl.when(pl.program_id(0) == 0)
    def _():
        cnt_ref[...] = jnp.zeros_like(cnt_ref)

    cnt_ref[...] += jnp.broadcast_to(jnp.sum(chosen, axis=1, keepdims=True), cnt_ref.shape)


def _router(x, mods_l, gain, router_t):
    return pl.pallas_call(
        _router_kernel,
        grid=(R // TM,),
        in_specs=[
            pl.BlockSpec((TM, D), lambda i: (i, 0)),
            pl.BlockSpec((None, N_MOD, D), lambda i: (_cond_idx(i, TM), 0, 0)),
            pl.BlockSpec((1, D), lambda i: (0, 0)),
            pl.BlockSpec((2 * SUB, D), lambda i: (0, 0)),
        ],
        out_specs=[
            pl.BlockSpec((TM * ROW_TILE, LANES), lambda i: (i, 0)),
            pl.BlockSpec((TOP_K, TM), lambda i: (0, i)),
            pl.BlockSpec((TOP_K, TM), lambda i: (0, i)),
            pl.BlockSpec((N_EXP, LANES), lambda i: (0, 0)),
        ],
        out_shape=[
            jax.ShapeDtypeStruct((R * ROW_TILE, LANES), F32),
            jax.ShapeDtypeStruct((TOP_K, R), I32),
            jax.ShapeDtypeStruct((TOP_K, R), F32),
            jax.ShapeDtypeStruct((N_EXP, LANES), F32),
        ],
        compiler_params=_cparams(("arbitrary",)),
        name="moe_router",
    )(x, mods_l, gain, router_t)


IDX_BLOCK = 1024
DMA_UNROLL = 8


def _slot_kernel(e_ref, gs_ref, pos_ref, tri_scr, carry_scr):
    @pl.when(pl.program_id(0) == 0)
    def _():
        r = lax.broadcasted_iota(I32, tri_scr.shape, 0)
        c = lax.broadcasted_iota(I32, tri_scr.shape, 1)
        tri_scr[...] = jnp.where(r <= c, 1.0, 0.0).astype(BF16)
        carry_scr[...] = jnp.zeros_like(carry_scr)

    e = e_ref[...]
    io = lax.broadcasted_iota(I32, (2 * SUB, IDX_BLOCK), 0)
    onehot = jnp.where(io == e, 1.0, 0.0)
    cum = _dot(onehot.astype(BF16), tri_scr[...])
    carry = carry_scr[...]
    slot = cum - 1.0 + carry[:, 0:1] + gs_ref[...][:, 0:1]
    pos_ref[...] = jnp.sum(onehot * slot, axis=0, keepdims=True).astype(I32)
    carry_scr[...] = carry + jnp.broadcast_to(cum[:, IDX_BLOCK - 1:IDX_BLOCK], carry.shape)


def _slots(idx, group_start):
    n_blk = TOP_K * R // IDX_BLOCK
    gs = jnp.broadcast_to(
        jnp.concatenate([group_start, jnp.zeros((2 * SUB - N_EXP,), I32)]).astype(F32)[:, None], (2 * SUB, LANES))
    pos = pl.pallas_call(
        _slot_kernel,
        grid=(n_blk,),
        in_specs=[
            pl.BlockSpec((None, 1, IDX_BLOCK), lambda c: (c, 0, 0)),
            pl.BlockSpec((2 * SUB, LANES), lambda c: (0, 0)),
        ],
        out_specs=pl.BlockSpec((None, 1, IDX_BLOCK), lambda c: (c, 0, 0)),
        out_shape=jax.ShapeDtypeStruct((n_blk, 1, IDX_BLOCK), I32),
        scratch_shapes=[pltpu.VMEM((IDX_BLOCK, IDX_BLOCK), BF16), pltpu.VMEM((2 * SUB, LANES), F32)],
        compiler_params=_cparams(("arbitrary",)),
        name="moe_slots",
    )(idx.reshape(n_blk, 1, IDX_BLOCK), gs)
    return pos.reshape(TOP_K * R)


def _dispatch_kernel(p0_ref, p1_ref, h8_ref, init_ref, hs8_ref, sem):
    del init_ref
    n_tok = h8_ref.shape[0] // ROW_TILE
    for p_ref in (p0_ref, p1_ref):
        def issue(g, carry, p_ref=p_ref):
            for u in range(DMA_UNROLL):
                r = g * DMA_UNROLL + u
                pltpu.make_async_copy(
                    h8_ref.at[pl.ds(pl.multiple_of(r * SUB, SUB), SUB)],
                    hs8_ref.at[pl.ds(pl.multiple_of(p_ref[r] * SUB, SUB), SUB)],
                    sem).start()
            return carry

        lax.fori_loop(0, n_tok // DMA_UNROLL, issue, 0)
    for _ in range(TOP_K):
        pltpu.make_async_copy(h8_ref, hs8_ref.at[pl.ds(0, n_tok * ROW_TILE)], sem).wait()


def _dispatch(pos, h8):
    nt = R // IDX_BLOCK
    zeros = jnp.zeros((N_SLOT * ROW_TILE, LANES), F32)
    return pl.pallas_call(
        _dispatch_kernel,
        grid=(nt,),
        in_specs=[
            pl.BlockSpec((IDX_BLOCK,), lambda i: (i,), memory_space=pltpu.SMEM),
            pl.BlockSpec((IDX_BLOCK,), lambda i: (nt + i,), memory_space=pltpu.SMEM),
            pl.BlockSpec((IDX_BLOCK * ROW_TILE, LANES), lambda i: (i, 0)),
            pl.BlockSpec(memory_space=pl.ANY),
        ],
        out_specs=pl.BlockSpec(memory_space=pl.ANY),
        out_shape=jax.ShapeDtypeStruct((N_SLOT * ROW_TILE, LANES), F32),
        scratch_shapes=[pltpu.SemaphoreType.DMA(())],
        input_output_aliases={3: 0},
        compiler_params=_cparams(("arbitrary",)),
        name="moe_dispatch",
    )(pos, pos, h8, zeros)


EXP_CHUNK = 1792
N_EXP_CHUNK = E_FF // EXP_CHUNK


def _expert_kernel(te_ref, tv_ref, x8_ref, w1_ref, w3_ref, w2_ref, y8_ref, xb_scr, acc_scr):
    i = pl.program_id(0)
    kc = pl.program_id(1)
    valid = tv_ref[i] == 1

    @pl.when(valid & (kc == 0))
    def _():
        for s in range(ROW_TILE):
            xb_scr[:, s * LANES:(s + 1) * LANES] = x8_ref[pl.ds(s, TM_MOE, stride=ROW_TILE), :].astype(BF16)
        acc_scr[...] = jnp.zeros_like(acc_scr)

    @pl.when(valid)
    def _():
        x = xb_scr[...]
        a = _dot(x, w1_ref[...])
        b = _dot(x, w3_ref[...])
        acc_scr[...] += _dot((_silu(a) * b).astype(BF16), w2_ref[...])

    @pl.when(valid & (kc == N_EXP_CHUNK - 1))
    def _():
        for s in range(ROW_TILE):
            y8_ref[pl.ds(s, TM_MOE, stride=ROW_TILE), :] = acc_scr[:, s * LANES:(s + 1) * LANES]

    @pl.when(jnp.logical_not(valid) & (kc == N_EXP_CHUNK - 1))
    def _():
        y8_ref[...] = jnp.zeros_like(y8_ref)


def _experts(tile_expert, tile_valid, hs8, w1, w3, w2):
    def kc_eff(kc, tv, i):
        return jnp.where(tv[i] == 1, kc, N_EXP_CHUNK - 1)

    grid_spec = pltpu.PrefetchScalarGridSpec(
        num_scalar_prefetch=2,
        grid=(N_TILE, N_EXP_CHUNK),
        in_specs=[
            pl.BlockSpec((TM_MOE * ROW_TILE, LANES), lambda i, kc, te, tv: (i, 0)),
            pl.BlockSpec((None, D, EXP_CHUNK), lambda i, kc, te, tv: (te[i], 0, kc_eff(kc, tv, i))),
            pl.BlockSpec((None, D, EXP_CHUNK), lambda i, kc, te, tv: (te[i], 0, kc_eff(kc, tv, i))),
            pl.BlockSpec((None, EXP_CHUNK, D), lambda i, kc, te, tv: (te[i], kc_eff(kc, tv, i), 0)),
        ],
        out_specs=pl.BlockSpec((TM_MOE * ROW_TILE, LANES), lambda i, kc, te, tv: (i, 0)),
        scratch_shapes=[pltpu.VMEM((TM_MOE, D), BF16), pltpu.VMEM((TM_MOE, D), F32)],
    )
    return pl.pallas_call(
        _expert_kernel,
        grid_spec=grid_spec,
        out_shape=jax.ShapeDtypeStruct((N_SLOT * ROW_TILE, LANES), F32),
        compiler_params=_cparams(("arbitrary", "arbitrary"), 56),
        name="moe_experts",
    )(tile_expert, tile_valid, hs8, w1, w3, w2)


TILES_PER_IDX_BLOCK = IDX_BLOCK // TM


def _moe_resid_kernel(p0_ref, p1_ref, y8_ref, wc_ref, x_ref, m_ref, fn_ref, o_ref, ya_scr, yb_scr, sems, *, final):
    i = pl.program_id(0)
    nt = pl.num_programs(0)
    tm = x_ref.shape[0]
    bufs = (ya_scr, yb_scr)

    def start(tile, b):
        off = (tile % TILES_PER_IDX_BLOCK) * tm
        for c, p_ref in enumerate((p0_ref, p1_ref)):
            def issue(g, carry, p_ref=p_ref, c=c):
                for u in range(DMA_UNROLL):
                    r = g * DMA_UNROLL + u
                    pltpu.make_async_copy(
                        y8_ref.at[pl.ds(pl.multiple_of(p_ref[off + r] * SUB, SUB), SUB)],
                        bufs[b].at[c, pl.ds(pl.multiple_of(r * SUB, SUB), SUB)],
                        sems.at[b]).start()
                return carry

            lax.fori_loop(0, tm // DMA_UNROLL, issue, 0)

    def finish(b):
        for c in range(TOP_K):
            pltpu.make_async_copy(y8_ref.at[pl.ds(0, tm * ROW_TILE)], bufs[b].at[c], sems.at[b]).wait()

    def rows(ref):
        return jnp.concatenate([ref[pl.ds(s, tm, stride=ROW_TILE), :] for s in range(ROW_TILE)], axis=1)

    def combine(b):
        w = wc_ref[...]
        f = w[:, 0:1] * rows(bufs[b].at[0]) + w[:, 1:2] * rows(bufs[b].at[1])
        xn = x_ref[...] + m_ref[...][5:6] * f
        if final:
            xn = _rms(xn, fn_ref[...])
        o_ref[...] = xn

    @pl.when(i == 0)
    def _():
        start(0, 0)

    for b in range(2):
        @pl.when(i % 2 == b)
        def _(b=b):
            @pl.when(i + 1 < nt)
            def _():
                start(i + 1, 1 - b)

            finish(b)
            combine(b)


def _moe_resid(pos, y8, wcol, x, mods_l, final_gain, final):
    nt = R // TM
    nblk = R // IDX_BLOCK

    def next_blk(i):
        return jnp.minimum(i + 1, nt - 1) // TILES_PER_IDX_BLOCK

    return pl.pallas_call(
        functools.partial(_moe_resid_kernel, final=final),
        grid=(nt,),
        in_specs=[
            pl.BlockSpec((IDX_BLOCK,), lambda i: (next_blk(i),), memory_space=pltpu.SMEM),
            pl.BlockSpec((IDX_BLOCK,), lambda i: (nblk + next_blk(i),), memory_space=pltpu.SMEM),
            pl.BlockSpec(memory_space=pl.ANY),
            pl.BlockSpec((TM, TOP_K), lambda i: (i, 0)),
            pl.BlockSpec((TM, D), lambda i: (i, 0)),
            pl.BlockSpec((None, N_MOD, D), lambda i: (_cond_idx(i, TM), 0, 0)),
            pl.BlockSpec((1, D), lambda i: (0, 0)),
        ],
        out_specs=pl.BlockSpec((TM, D), lambda i: (i, 0)),
        out_shape=jax.ShapeDtypeStruct((R, D), F32),
        scratch_shapes=[pltpu.VMEM((TOP_K, TM * ROW_TILE, LANES), F32), pltpu.VMEM((TOP_K, TM * ROW_TILE, LANES), F32),
                        pltpu.SemaphoreType.DMA((2,))],
        compiler_params=_cparams(("arbitrary",), 48),
        name="moe_resid",
    )(pos, pos, y8, wcol, x, mods_l, final_gain)


def _tile_tables(counts):
    padded = ((counts + TM_MOE - 1) // TM_MOE) * TM_MOE
    gend = jnp.cumsum(padded)
    tile_start = jnp.arange(N_TILE, dtype=I32) * TM_MOE
    te = jnp.sum((tile_start[:, None] >= gend[None, :]).astype(I32), axis=1)
    valid = tile_start < gend[-1]
    te_last = te[gend[-1] // TM_MOE - 1]
    te = jnp.minimum(jnp.where(valid, te, te_last), N_EXP - 1)
    return (gend - padded).astype(I32), te.astype(I32), valid.astype(I32)


def _rot_half(w):
    wa = w.reshape(w.shape[:-1] + (2, 2, ROPE_AXIS // 2))
    return jnp.stack([-wa[..., 1, :], wa[..., 0, :]], axis=-2).reshape(w.shape)


def _rope_tables(tm):
    rows = DEC_SEQ // GRID_W
    r = jnp.repeat(jnp.arange(rows), GRID_W).astype(F32)
    c = jnp.tile(jnp.arange(GRID_W), rows).astype(F32)
    inv = ROPE_THETA ** (-jnp.arange(0, ROPE_AXIS, 2, dtype=F32) / ROPE_AXIS)
    ang_r = r[:, None] * inv
    ang_c = c[:, None] * inv
    ang = jnp.concatenate([ang_r, ang_r, ang_c, ang_c], axis=-1)
    cos, sin = jnp.cos(ang), jnp.sin(ang)
    z32 = jnp.zeros_like(cos)
    one = jnp.ones((tm, QK_ROPE), F32)
    zt = jnp.zeros((tm, QK_ROPE), F32)
    tq_c = jnp.concatenate([jnp.concatenate([cos, cos, z32, z32], 1), jnp.concatenate([one, one, zt, zt], 1)], 0)
    tq_s = jnp.concatenate([jnp.concatenate([sin, sin, z32, z32], 1), jnp.zeros((tm, LANES), F32)], 0)
    tk_c = jnp.concatenate([jnp.concatenate([cos, z32, z32, z32], 1), jnp.concatenate([one, zt, zt, zt], 1)], 0)
    tk_s = jnp.concatenate([jnp.concatenate([sin, z32, z32, z32], 1), jnp.zeros((tm, LANES), F32)], 0)
    return tq_c, tq_s, tk_c, tk_s


def _mla_weights(w_dq, w_uq, w_dkv, w_uk, w_uv, w_o):
    kr_w = w_dkv[:, KV_LORA:]
    w_a = jnp.concatenate(
        [w_dq, w_dkv[:, :KV_LORA], kr_w, _rot_half(kr_w), jnp.zeros((D, LANES - 2 * QK_ROPE), F32)], axis=1)
    uq = w_uq.reshape(Q_LORA, N_PAIR, 2, QK_NOPE + QK_ROPE)
    nope = uq[..., :QK_NOPE].reshape(Q_LORA, N_PAIR, 2 * QK_NOPE)
    rope = uq[..., QK_NOPE:]
    w_uq2 = jnp.concatenate(
        [nope, rope.reshape(Q_LORA, N_PAIR, 2 * QK_ROPE), _rot_half(rope).reshape(Q_LORA, N_PAIR, 2 * QK_ROPE)],
        axis=-1).reshape(Q_LORA, N_PAIR * 2 * LANES)
    uk = w_uk.reshape(KV_LORA, N_PAIR, 2 * QK_NOPE)
    wk = jnp.concatenate([uk, jnp.zeros((KV_LORA, N_PAIR, LANES), F32)], axis=-1).reshape(KV_LORA, N_PAIR * 2 * LANES)
    eye = jnp.eye(LANES, QK_ROPE, dtype=F32)
    pair = jnp.concatenate([jnp.zeros((LANES, LANES), F32), eye, eye, jnp.zeros((LANES, LANES - 2 * QK_ROPE), F32)], 1)
    wkr = jnp.tile(pair, (1, N_PAIR))
    wv = w_uv.reshape(KV_LORA, HEADS * V_HEAD)
    wo = w_o.reshape(HEADS * V_HEAD, D)
    return [w.astype(BF16) for w in (w_a, w_uq2, wk, wkr, wv, wo)]


def kernel(x_prompt, x_sample, cache_ckv, cache_krope, state_hgrn, c, c_ctx, w_ada, b_ada, norm_mix, norm_ffn, mla_w_dq, mla_q_norm, mla_w_uq, mla_w_dkv, mla_kv_norm, mla_w_uk, mla_w_uv, mla_w_o, hg_w_q, hg_w_f, hg_w_i, hg_w_g, hg_lb_logits, hg_o_norm, hg_w_o, ffn_w1, ffn_w3, ffn_w2, moe_router, moe_w1, moe_w3, moe_w2, final_norm):
    x = (x_prompt.reshape(R_P, D), x_sample.reshape(R_S, D))
    cond = jnp.concatenate([c, c_ctx[None], jnp.zeros((COND_PAD - N_COND, D), F32)], axis=0)
    mods = _ada(cond, w_ada, b_ada)
    lb_all = _lower_bounds(hg_lb_logits)
    tq_c, tq_s, tk_c, tk_s = _rope_tables(TM)
    new_ckv, new_krope, new_hgrn = [], [], []

    for l in range(DEPTH):
        j = l // 2
        mods_l = mods[l]
        gain_mix = norm_mix[l][None]
        gain_ffn = norm_ffn[l][None]
        if l % 2 == 0:
            w_a, w_uq2, wk, wkr, wv, wo = _mla_weights(
                mla_w_dq[j], mla_w_uq[j], mla_w_dkv[j], mla_w_uk[j], mla_w_uv[j], mla_w_o[j])
            cq, ckv, kr = _mla_proj(x, mods_l, gain_mix, w_a, mla_q_norm[j][None], mla_kv_norm[j][None], tk_c, tk_s)
            q2 = _q_up(cq, w_uq2, tq_c, tq_s)
            k2_p, v_p = _kv_up(ckv, kr, R_P, wk, wkr, wv)
            ckv_s = jnp.concatenate([cache_ckv[:, j], ckv[R_P:].reshape(DEC_BATCH, DEC_SEQ, KV_LORA)], axis=1)
            kr_cache = jnp.pad(cache_krope[:, j], ((0, 0), (0, 0), (0, LANES - QK_ROPE)))
            kr_s = jnp.concatenate([kr_cache, kr[R_P:].reshape(DEC_BATCH, DEC_SEQ, LANES)], axis=1)
            s_all = PAST + DEC_SEQ
            k2_s, v_s = _kv_up(ckv_s.reshape(DEC_BATCH * s_all, KV_LORA), kr_s.reshape(DEC_BATCH * s_all, LANES),
                               DEC_BATCH * s_all, wk, wkr, wv)
            o_p = _attention(q2, k2_p, v_p, BATCH, SEQ, SEQ, 0, SEQ, 4)
            o_s = _attention(q2, k2_s, v_s, DEC_BATCH, DEC_SEQ, s_all, R_P, 512, 1)
            x = _mm_resid((o_p, o_s), wo, x, mods_l, 2)
            new_ckv.append(ckv[:R_P].reshape(BATCH, SEQ, KV_LORA))
            new_krope.append(kr[:R_P, :QK_ROPE].reshape(BATCH, SEQ, QK_ROPE))
        else:
            w5 = jnp.stack([hg_w_q[j], hg_w_f[j, 0], hg_w_f[j, 1], hg_w_i[j], hg_w_g[j]]).astype(BF16)
            q, k, lf, v, gg = _hg_proj(x, mods_l, gain_mix, w5, lb_all[j])
            s_zero = jnp.zeros((BATCH, 2, HG_H, HG_DK, HG_DK), F32)
            o2_p, st_p = _gla(q, k, lf, v, s_zero, BATCH, SEQ, 0)
            o2_s, _ = _gla(q, k, lf, v, state_hgrn[:, j], DEC_BATCH, DEC_SEQ, R_P)
            x = _hg_out(o2_p, o2_s, gg, hg_o_norm[j][None], hg_w_o[j].astype(BF16), x, mods_l)
            new_hgrn.append(st_p)
        if l % 2 == 0:
            x = _ffn(x, mods_l, gain_ffn, ffn_w1[j].astype(BF16), ffn_w3[j].astype(BF16), ffn_w2[j].astype(BF16))
        else:
            router_t = jnp.concatenate([moe_router[j].T, jnp.zeros((2 * SUB - N_EXP, D), F32)], axis=0)
            h8, idx, wts, cnt = _router(x, mods_l, gain_ffn, router_t)
            group_start, tile_expert, tile_valid = _tile_tables(cnt[:, 0].astype(I32))
            pos = _slots(idx, group_start)
            y8 = _experts(tile_expert, tile_valid, _dispatch(pos, h8),
                          moe_w1[j].astype(BF16), moe_w3[j].astype(BF16), moe_w2[j].astype(BF16))
            x = _moe_resid(pos, y8, wts.T, x, mods_l, final_norm[None], final=(l == DEPTH - 1))

    y_prompt = x[:R_P].reshape(BATCH, SEQ, D)
    y_sample = x[R_P:].reshape(DEC_BATCH, DEC_SEQ, D)
    return (y_prompt, y_sample, jnp.stack(new_ckv, axis=1), jnp.stack(new_krope, axis=1),
            jnp.stack(new_hgrn, axis=1))
```

```python
import functools
import math

import jax
import jax.numpy as jnp
from jax import lax
from jax.experimental import pallas as pl
from jax.experimental.pallas import tpu as pltpu

F32 = jnp.float32
BF16 = jnp.bfloat16
I32 = jnp.int32

D = 1024
BATCH, SEQ = 32, 256
DEC_BATCH, DEC_SEQ = 8, 1024
PAST = 256
DEPTH = 4
R_P = BATCH * SEQ
R_S = DEC_BATCH * DEC_SEQ
R = R_P + R_S
N_COND = DEC_BATCH + 1
COND_PAD = 16
N_MOD = 6
EPS = 1e-6
GRID_W = 64
HEADS, QK_NOPE, QK_ROPE, V_HEAD = 16, 64, 32, 64
Q_LORA, KV_LORA = 512, 256
N_PAIR = HEADS // 2
ROPE_AXIS = QK_ROPE // 2
ROPE_THETA = 10000.0
HG_H, HG_DK = 8, 128
BLK = 128
CHUNK = 16
D_FF = 2816
N_EXP, TOP_K, E_FF = 8, 2, 3584
LANES = 128
SUB = 8
ROW_TILE = D // LANES
TM = 512
TM_MOE = 512
N_SLOT = TOP_K * R + N_EXP * TM_MOE
N_TILE = N_SLOT // TM_MOE
MIB = 1024 * 1024
assert DEPTH % 2 == 0


def _cparams(sem, vmem_mib=40):
    return pltpu.CompilerParams(dimension_semantics=sem, vmem_limit_bytes=vmem_mib * MIB)


def _cond_idx(i, tm):
    r0 = i * tm
    return jnp.where(r0 < R_P, DEC_BATCH, (r0 - R_P) // DEC_SEQ)


def _rope_idx(i, tm):
    nb = DEC_SEQ // tm
    r0 = i * tm
    return jnp.where(r0 < R_P, nb, ((r0 - R_P) // tm) % nb)


def _normmod(x, gain, scale, shift):
    ms = jnp.mean(x * x, axis=-1, keepdims=True)
    return (x * lax.rsqrt(ms + EPS)) * gain * (1.0 + scale) + shift


def _rms(x, gain):
    ms = jnp.mean(x * x, axis=-1, keepdims=True)
    return (x * lax.rsqrt(ms + EPS)) * gain


def _silu(x):
    return x / (1.0 + jnp.exp(-x))


def _dot(a, b):
    return jnp.dot(a, b, preferred_element_type=F32)


def _dot_nt(a, b):
    return lax.dot_general(a, b, (((1,), (1,)), ((), ())), preferred_element_type=F32)


def _split3(x):
    hi = x.astype(BF16)
    r1 = x - hi.astype(F32)
    mid = r1.astype(BF16)
    lo = (r1 - mid.astype(F32)).astype(BF16)
    return hi, mid, lo


def _ada_kernel(c_ref, w_ref, b_ref, o_ref):
    o_ref[...] = _dot(_silu(c_ref[...]), w_ref[...]) + b_ref[...]


def _ada(cond, w_ada, b_ada):
    tn = 1536
    out = pl.pallas_call(
        _ada_kernel,
        grid=(DEPTH, N_MOD * D // tn),
        in_specs=[
            pl.BlockSpec((COND_PAD, D), lambda l, j: (0, 0)),
            pl.BlockSpec((None, D, tn), lambda l, j: (l, 0, j)),
            pl.BlockSpec((None, 1, tn), lambda l, j: (l, 0, j)),
        ],
        out_specs=pl.BlockSpec((None, COND_PAD, tn), lambda l, j: (l, 0, j)),
        out_shape=jax.ShapeDtypeStruct((DEPTH, COND_PAD, N_MOD * D), F32),
        compiler_params=_cparams(("arbitrary", "arbitrary")),
        name="ada",
    )(cond, w_ada, b_ada.reshape(DEPTH, 1, N_MOD * D))
    return out[:, :N_COND].reshape(DEPTH, N_COND, N_MOD, D)


def _lb_kernel(x_ref, o_ref):
    x = x_ref[...]
    e = jnp.exp(x - jnp.max(x, axis=0, keepdims=True))
    sm = e / jnp.sum(e, axis=0, keepdims=True)
    n = x.shape[0]
    cum = sm[0]
    o_ref[0] = jnp.zeros_like(cum)
    for i in range(1, n):
        cum_i = cum + sm[i]
        o_ref[i] = cum_i - sm[0]
        cum = cum_i


def _lower_bounds(lb_logits):
    return pl.pallas_call(
        _lb_kernel,
        out_shape=jax.ShapeDtypeStruct(lb_logits.shape, F32),
        name="hgrn_lower_bounds",
    )(lb_logits)


def _two_part_rows(x):
    if isinstance(x, tuple):
        x_p, x_s = x
        s_off = 0
    else:
        x_p = x_s = x
        s_off = N_TILE_P
    w = x_p.shape[1]
    specs = [pl.BlockSpec((TM, w), lambda i: (_prompt_blk(i), 0)),
             pl.BlockSpec((TM, w), lambda i: (s_off + _sample_blk(i), 0))]
    return [x_p, x_s], specs


def _pick_rows(p_ref, s_ref):
    return jnp.where(pl.program_id(0) < N_TILE_P, p_ref[...], s_ref[...])


def _mla_proj_kernel(xp_ref, xs_ref, m_ref, g_ref, w_ref, qn_ref, kvn_ref, tc_ref, ts_ref,
                     cq_ref, ckv_ref, kr_ref):
    m = m_ref[...]
    h = _normmod(_pick_rows(xp_ref, xs_ref), g_ref[...], m[1:2], m[0:1]).astype(BF16)
    y = _dot(h, w_ref[...])
    cq_ref[...] = _rms(y[:, :Q_LORA], qn_ref[...]).astype(BF16)
    ckv_ref[...] = _rms(y[:, Q_LORA:Q_LORA + KV_LORA], kvn_ref[...])
    slab = y[:, Q_LORA + KV_LORA:]
    kr_ref[...] = slab * tc_ref[...] + pltpu.roll(slab, LANES - QK_ROPE, 1) * ts_ref[...]


def _mla_proj(x, mods_l, gain, w_a, q_norm, kv_norm, tk_c, tk_s):
    n_a = w_a.shape[1]
    row = lambda i: (i, 0)
    fixed = lambda i: (0, 0)
    x_args, x_specs = _two_part_rows(x)
    return pl.pallas_call(
        _mla_proj_kernel,
        grid=(R // TM,),
        in_specs=x_specs + [
            pl.BlockSpec((None, N_MOD, D), lambda i: (_cond_idx(i, TM), 0, 0)),
            pl.BlockSpec((1, D), fixed),
            pl.BlockSpec((D, n_a), fixed),
            pl.BlockSpec((1, Q_LORA), fixed),
            pl.BlockSpec((1, KV_LORA), fixed),
            pl.BlockSpec((TM, LANES), lambda i: (_rope_idx(i, TM), 0)),
            pl.BlockSpec((TM, LANES), lambda i: (_rope_idx(i, TM), 0)),
        ],
        out_specs=[
            pl.BlockSpec((TM, Q_LORA), row),
            pl.BlockSpec((TM, KV_LORA), row),
            pl.BlockSpec((TM, LANES), row),
        ],
        out_shape=[
            jax.ShapeDtypeStruct((R, Q_LORA), BF16),
            jax.ShapeDtypeStruct((R, KV_LORA), F32),
            jax.ShapeDtypeStruct((R, LANES), F32),
        ],
        compiler_params=_cparams(("arbitrary",)),
        name="mla_proj",
    )(*x_args, mods_l, gain, w_a, q_norm, kv_norm, tk_c, tk_s)


QK_SCALE_LOG2 = math.log2(math.e) / math.sqrt(QK_NOPE + QK_ROPE)


def _q_up_kernel(cq_ref, w_ref, tc_ref, ts_ref, q_ref):
    y = _dot(cq_ref[...], w_ref[...]) * QK_SCALE_LOG2
    tc = tc_ref[...]
    ts = ts_ref[...]
    for p in range(N_PAIR):
        lo = p * 2 * LANES
        q_ref[:, lo:lo + LANES] = y[:, lo:lo + LANES].astype(BF16)
        hi = y[:, lo + LANES:lo + 2 * LANES]
        q_ref[:, lo + LANES:lo + 2 * LANES] = (hi * tc + pltpu.roll(hi, LANES // 2, 1) * ts).astype(BF16)


def _q_up(cq, w_uq2, tq_c, tq_s):
    nq = w_uq2.shape[1]
    return pl.pallas_call(
        _q_up_kernel,
        grid=(R // TM,),
        in_specs=[
            pl.BlockSpec((TM, Q_LORA), lambda i: (i, 0)),
            pl.BlockSpec((Q_LORA, nq), lambda i: (0, 0)),
            pl.BlockSpec((TM, LANES), lambda i: (_rope_idx(i, TM), 0)),
            pl.BlockSpec((TM, LANES), lambda i: (_rope_idx(i, TM), 0)),
        ],
        out_specs=pl.BlockSpec((TM, nq), lambda i: (i, 0)),
        out_shape=jax.ShapeDtypeStruct((R, nq), BF16),
        compiler_params=_cparams(("arbitrary",)),
        name="mla_q_up",
    )(cq, w_uq2, tq_c, tq_s)


def _kv_up_kernel(c_ref, r_ref, wk_ref, wkr_ref, wv_ref, k_ref, v_ref):
    c = c_ref[...].astype(BF16)
    r = r_ref[...].astype(BF16)
    k_ref[...] = (_dot(c, wk_ref[...]) + _dot(r, wkr_ref[...])).astype(BF16)
    v_ref[...] = _dot(c, wv_ref[...]).astype(BF16)


def _kv_up(ckv, kr, n_rows, wk, wkr, wv):
    nk = wk.shape[1]
    nv = wv.shape[1]
    fixed = lambda i: (0, 0)
    return pl.pallas_call(
        _kv_up_kernel,
        grid=(n_rows // TM,),
        in_specs=[
            pl.BlockSpec((TM, KV_LORA), lambda i: (i, 0)),
            pl.BlockSpec((TM, LANES), lambda i: (i, 0)),
            pl.BlockSpec((KV_LORA, nk), fixed),
            pl.BlockSpec((LANES, nk), fixed),
            pl.BlockSpec((KV_LORA, nv), fixed),
        ],
        out_specs=[
            pl.BlockSpec((TM, nk), lambda i: (i, 0)),
            pl.BlockSpec((TM, nv), lambda i: (i, 0)),
        ],
        out_shape=[
            jax.ShapeDtypeStruct((n_rows, nk), BF16),
            jax.ShapeDtypeStruct((n_rows, nv), BF16),
        ],
        compiler_params=_cparams(("arbitrary",)),
        name="mla_kv_up",
    )(ckv, kr, wk, wkr, wv)


def _attn_kernel(q_ref, k_ref, v_ref, *rest, n_pair):
    o_ref = rest[-1]
    lq = lax.broadcasted_iota(I32, (1, 2 * LANES), 1)
    lv = lax.broadcasted_iota(I32, (1, LANES), 1)
    sel_a = (lq < QK_NOPE) | ((lq >= LANES) & (lq < LANES + QK_ROPE))
    sel_b = ((lq >= QK_NOPE) & (lq < LANES)) | ((lq >= LANES + QK_ROPE) & (lq < LANES + 2 * QK_ROPE))
    for p in range(n_pair):
        q = q_ref[:, p * 2 * LANES:(p + 1) * 2 * LANES]
        k = k_ref[:, p * 2 * LANES:(p + 1) * 2 * LANES]
        v = v_ref[:, p * LANES:(p + 1) * LANES]
        zq = jnp.zeros_like(q)
        zv = jnp.zeros_like(v)
        out = None
        for sel, vsel in ((sel_a, lv < V_HEAD), (sel_b, lv >= V_HEAD)):
            s = _dot_nt(jnp.where(sel, q, zq), k)
            e = jnp.exp2(s - jnp.max(s, axis=-1, keepdims=True))
            den = jnp.sum(e, axis=-1, keepdims=True)
            o = _dot(e.astype(BF16), jnp.where(vsel, v, zv)) / den
            out = o if out is None else out + o
        o_ref[:, p * LANES:(p + 1) * LANES] = out.astype(BF16)


def _attention(q2, k2, v, n_batch, t_len, s_len, q_row0, tq, n_pair):
    nq = t_len // tq
    qb0 = q_row0 // tq
    return pl.pallas_call(
        functools.partial(_attn_kernel, n_pair=n_pair),
        grid=(n_batch, N_PAIR // n_pair, nq),
        in_specs=[
            pl.BlockSpec((tq, n_pair * 2 * LANES), lambda b, p, i: (qb0 + b * nq + i, p)),
            pl.BlockSpec((s_len, n_pair * 2 * LANES), lambda b, p, i: (b, p)),
            pl.BlockSpec((s_len, n_pair * LANES), lambda b, p, i: (b, p)),
        ],
        out_specs=pl.BlockSpec((tq, n_pair * LANES), lambda b, p, i: (b * nq + i, p)),
        out_shape=jax.ShapeDtypeStruct((n_batch * t_len, HEADS * V_HEAD), BF16),
        compiler_params=_cparams(("arbitrary", "arbitrary", "arbitrary")),
        name="mla_attention",
    )(q2, k2, v)


N_TILE_P = R_P // TM


def _prompt_blk(i):
    return jnp.minimum(i, N_TILE_P - 1)


def _sample_blk(i):
    return jnp.maximum(i - N_TILE_P, 0)


def _mm_resid_kernel(ap_ref, as_ref, w_ref, xp_ref, xs_ref, m_ref, o_ref, *, gate_idx):
    gate = m_ref[...][gate_idx:gate_idx + 1]
    o_ref[...] = _pick_rows(xp_ref, xs_ref) + gate * _dot(_pick_rows(ap_ref, as_ref), w_ref[...])


def _mm_resid(a, w, x, mods_l, gate_idx):
    a_args, a_specs = _two_part_rows(a)
    x_args, x_specs = _two_part_rows(x)
    k = a_args[0].shape[1]
    return pl.pallas_call(
        functools.partial(_mm_resid_kernel, gate_idx=gate_idx),
        grid=(R // TM,),
        in_specs=a_specs + [pl.BlockSpec((k, D), lambda i: (0, 0))] + x_specs + [
            pl.BlockSpec((None, N_MOD, D), lambda i: (_cond_idx(i, TM), 0, 0)),
        ],
        out_specs=pl.BlockSpec((TM, D), lambda i: (i, 0)),
        out_shape=jax.ShapeDtypeStruct((R, D), F32),
        compiler_params=_cparams(("arbitrary",)),
        name="mm_resid",
    )(*a_args, w, *x_args, mods_l)


FFN_CHUNK = 1408


def _ffn_kernel(x_ref, m_ref, g_ref, w1_ref, w3_ref, w2_ref, o_ref):
    x = x_ref[...]
    m = m_ref[...]
    h = _normmod(x, g_ref[...], m[4:5], m[3:4]).astype(BF16)
    acc = jnp.zeros(x.shape, F32)
    for c in range(D_FF // FFN_CHUNK):
        sl = slice(c * FFN_CHUNK, (c + 1) * FFN_CHUNK)
        a = _dot(h, w1_ref[:, sl])
        b = _dot(h, w3_ref[:, sl])
        acc = acc + _dot((_silu(a) * b).astype(BF16), w2_ref[sl, :])
    o_ref[...] = x + m[5:6] * acc


def _ffn(x, mods_l, gain, w1, w3, w2, j):
    tm = TM
    fixed = lambda i: (0, 0)
    layer = lambda i: (j, 0, 0)
    once = pl.Buffered(1)
    return pl.pallas_call(
        _ffn_kernel,
        grid=(R // tm,),
        in_specs=[
            pl.BlockSpec((tm, D), lambda i: (i, 0)),
            pl.BlockSpec((None, N_MOD, D), lambda i: (_cond_idx(i, tm), 0, 0)),
            pl.BlockSpec((1, D), fixed),
            pl.BlockSpec((None, D, D_FF), layer, pipeline_mode=once),
            pl.BlockSpec((None, D, D_FF), layer, pipeline_mode=once),
            pl.BlockSpec((None, D_FF, D), layer, pipeline_mode=once),
        ],
        out_specs=pl.BlockSpec((tm, D), lambda i: (i, 0)),
        out_shape=jax.ShapeDtypeStruct((R, D), F32),
        compiler_params=_cparams(("arbitrary",), 56),
        name="dense_swiglu",
    )(x, mods_l, gain, w1, w3, w2)


def _forget_gate(z, lb):
    e = jnp.exp(-jnp.abs(z))
    log_sig = jnp.minimum(z, 0.0) - jnp.log(1.0 + e)
    a = jnp.log(lb)
    b = jnp.log1p(-lb) + log_sig
    log_f = jnp.maximum(a, b) + jnp.log(1.0 + jnp.exp(-jnp.abs(a - b)))
    k = (1.0 - lb) * (jnp.where(z >= 0, e, 1.0) / (1.0 + e))
    return k, log_f


def _hg_proj_kernel(x_ref, m_ref, g_ref, w_ref, lb_ref, q_ref, k_ref, lf_ref, v_ref, gg_ref):
    m = m_ref[...]
    h = _normmod(x_ref[...], g_ref[...], m[1:2], m[0:1]).astype(BF16)
    q_ref[...] = _dot(h, w_ref[0])
    for dr in range(2):
        k, log_f = _forget_gate(_dot(h, w_ref[1 + dr]), lb_ref[dr])
        k_ref[dr] = k
        lf_ref[dr] = log_f
    v_ref[...] = _dot(h, w_ref[3]).astype(BF16)
    gg_ref[...] = _dot(h, w_ref[4])


def _hg_proj(x, mods_l, gain, w5, lb):
    tm = TM
    row = lambda i: (i, 0)
    dirs = lambda i: (0, i, 0)
    return pl.pallas_call(
        _hg_proj_kernel,
        grid=(R // tm,),
        in_specs=[
            pl.BlockSpec((tm, D), row),
            pl.BlockSpec((None, N_MOD, D), lambda i: (_cond_idx(i, tm), 0, 0)),
            pl.BlockSpec((1, D), lambda i: (0, 0)),
            pl.BlockSpec((5, D, D), lambda i: (0, 0, 0), pipeline_mode=pl.Buffered(1)),
            pl.BlockSpec((2, 1, D), lambda i: (0, 0, 0)),
        ],
        out_specs=[
            pl.BlockSpec((tm, D), row),
            pl.BlockSpec((2, tm, D), dirs),
            pl.BlockSpec((2, tm, D), dirs),
            pl.BlockSpec((tm, D), row),
            pl.BlockSpec((tm, D), row),
        ],
        out_shape=[
            jax.ShapeDtypeStruct((R, D), F32),
            jax.ShapeDtypeStruct((2, R, D), F32),
            jax.ShapeDtypeStruct((2, R, D), F32),
            jax.ShapeDtypeStruct((R, D), BF16),
            jax.ShapeDtypeStruct((R, D), F32),
        ],
        compiler_params=_cparams(("arbitrary",), 48),
        name="hgrn_proj",
    )(x, mods_l, gain, w5, lb.reshape(2, 1, D))


LEVEL_HALVES = (64, 32, 16)


def _gla_kernel(q_ref, k_ref, lf_ref, v_ref, s0_ref, *rest, nb):
    o_ref, sn_ref, st_scr, b_scr = rest[-4:]
    d = pl.program_id(1)
    n = pl.program_id(2)

    @pl.when(n == 0)
    def _():
        for h in range(HG_H):
            st_scr[h] = s0_ref[h].T

    row = lax.broadcasted_iota(I32, (BLK, BLK), 0)
    col = lax.broadcasted_iota(I32, (BLK, BLK), 1)
    ut = row + d * (BLK - 1 - 2 * row)
    us = col + d * (BLK - 1 - 2 * col)
    causal = us <= ut
    tri = jnp.where(causal, 1.0, 0.0).astype(BF16)

    hi, mid, lo = _split3(lf_ref[...])
    b_scr[...] = _dot(tri, hi) + _dot(tri, mid) + _dot(tri, lo)

    level_masks = []
    for hs in LEVEL_HALVES:
        sh = int(math.log2(2 * hs))
        same = (ut >> sh) == (us >> sh)
        level_masks.append(same & ((ut & (2 * hs - 1)) >= hs) & ((us & (2 * hs - 1)) < hs))
    base_mask = ((ut >> 4) == (us >> 4)) & causal

    def split_rows(sl, half):
        parts = []
        for jr in range(BLK // (2 * half)):
            r0 = jr * 2 * half + half - 1
            r = jnp.where(d == 0, b_scr[r0:r0 + 1, sl], b_scr[r0 + 1:r0 + 2, sl])
            parts.append(jnp.broadcast_to(r, (2 * half, HG_DK)))
        return parts[0] if len(parts) == 1 else jnp.concatenate(parts, axis=0)

    for h in range(HG_H):
        sl = slice(h * HG_DK, (h + 1) * HG_DK)
        bh = b_scr[:, sl]
        q = q_ref[:, sl]
        k = k_ref[:, sl]
        v = v_ref[:, sl]

        xq = bh - split_rows(sl, CHUNK // 2)
        att = jnp.where(base_mask, _dot_nt((q * jnp.exp(xq)).astype(BF16), (k * jnp.exp(-xq)).astype(BF16)), 0.0)
        for hs, msk in zip(LEVEL_HALVES, level_masks):
            e = jnp.exp(-jnp.abs(bh - split_rows(sl, hs)))
            att = jnp.where(msk, _dot_nt((q * e).astype(BF16), (k * e).astype(BF16)), att)

        b_last = jnp.where(d == 0, b_scr[BLK - 1:BLK, sl], b_scr[0:1, sl])
        q_in = (q * jnp.exp(bh)).astype(BF16)
        k_in = (k * jnp.exp(b_last - bh)).astype(BF16)
        st = st_scr[h]
        o_ref[:, sl] = _dot(att.astype(BF16), v) + _dot_nt(q_in, st.astype(BF16))
        v_t = v.astype(F32).T.astype(BF16)
        st_scr[h] = st * jnp.exp(b_last) + _dot(v_t, k_in)

    @pl.when(n == nb - 1)
    def _():
        for h in range(HG_H):
            sn_ref[h] = st_scr[h].T


def _gla(q, k, lf, v, s0, n_batch, t_len, row0):
    nb = t_len // BLK
    rb0 = row0 // BLK

    def rb(b, d, n):
        return b * nb + n + d * (nb - 1 - 2 * n)

    return pl.pallas_call(
        functools.partial(_gla_kernel, nb=nb),
        grid=(n_batch, 2, nb),
        in_specs=[
            pl.BlockSpec((BLK, D), lambda b, d, n: (rb0 + rb(b, d, n), 0)),
            pl.BlockSpec((None, BLK, D), lambda b, d, n: (d, rb0 + rb(b, d, n), 0)),
            pl.BlockSpec((None, BLK, D), lambda b, d, n: (d, rb0 + rb(b, d, n), 0)),
            pl.BlockSpec((BLK, D), lambda b, d, n: (rb0 + rb(b, d, n), 0)),
            pl.BlockSpec((None, None, HG_H, HG_DK, HG_DK), lambda b, d, n: (b, d, 0, 0, 0)),
        ],
        out_specs=[
            pl.BlockSpec((None, BLK, D), lambda b, d, n: (d, rb(b, d, n), 0)),
            pl.BlockSpec((None, None, HG_H, HG_DK, HG_DK), lambda b, d, n: (b, d, 0, 0, 0)),
        ],
        out_shape=[
            jax.ShapeDtypeStruct((2, n_batch * t_len, D), F32),
            jax.ShapeDtypeStruct((n_batch, 2, HG_H, HG_DK, HG_DK), F32),
        ],
        scratch_shapes=[pltpu.VMEM((HG_H, HG_DK, HG_DK), F32), pltpu.VMEM((BLK, D), F32)],
        compiler_params=_cparams(("arbitrary", "arbitrary", "arbitrary")),
        name="hgrn_recurrence",
    )(q, k, lf, v, s0)


def _hg_out_kernel(ofp_ref, obp_ref, ofs_ref, obs_ref, gg_ref, on_ref, w_ref, x_ref, m_ref, o_ref):
    o = jnp.where(pl.program_id(0) < N_TILE_P, ofp_ref[...] + obp_ref[...], ofs_ref[...] + obs_ref[...])
    gain = on_ref[...]
    parts = []
    for h in range(HG_H):
        parts.append(_rms(o[:, h * HG_DK:(h + 1) * HG_DK], gain))
    a = (jnp.concatenate(parts, axis=1) * _silu(gg_ref[...])).astype(BF16)
    o_ref[...] = x_ref[...] + m_ref[...][2:3] * _dot(a, w_ref[...])


def _hg_out(o2_p, o2_s, gg, o_norm, w_o, x, mods_l):
    fixed = lambda i: (0, 0)
    return pl.pallas_call(
        _hg_out_kernel,
        grid=(R // TM,),
        in_specs=[
            pl.BlockSpec((None, TM, D), lambda i: (0, _prompt_blk(i), 0)),
            pl.BlockSpec((None, TM, D), lambda i: (1, _prompt_blk(i), 0)),
            pl.BlockSpec((None, TM, D), lambda i: (0, _sample_blk(i), 0)),
            pl.BlockSpec((None, TM, D), lambda i: (1, _sample_blk(i), 0)),
            pl.BlockSpec((TM, D), lambda i: (i, 0)),
            pl.BlockSpec((1, HG_DK), fixed),
            pl.BlockSpec((D, D), fixed),
            pl.BlockSpec((TM, D), lambda i: (i, 0)),
            pl.BlockSpec((None, N_MOD, D), lambda i: (_cond_idx(i, TM), 0, 0)),
        ],
        out_specs=pl.BlockSpec((TM, D), lambda i: (i, 0)),
        out_shape=jax.ShapeDtypeStruct((R, D), F32),
        compiler_params=_cparams(("arbitrary",)),
        name="hgrn_out",
    )(o2_p, o2_p, o2_s, o2_s, gg, o_norm, w_o, x, mods_l)


def _router_kernel(x_ref, m_ref, g_ref, rt_ref, h8_ref, idx_ref, wt_ref, cnt_ref):
    m = m_ref[...]
    h = _normmod(x_ref[...], g_ref[...], m[4:5], m[3:4])
    tm = h.shape[0]
    for s in range(ROW_TILE):
        h8_ref[pl.ds(s, tm, stride=ROW_TILE), :] = h[:, s * LANES:(s + 1) * LANES]
    h1, h2, h3 = _split3(h)
    r1, r2, r3 = _split3(rt_ref[...])
    lt = (_dot_nt(r1, h1) + _dot_nt(r1, h2) + _dot_nt(r2, h1)
          + _dot_nt(r1, h3) + _dot_nt(r3, h1) + _dot_nt(r2, h2))
    lg = lt[:N_EXP]
    e = jnp.exp(lg - jnp.max(lg, axis=0, keepdims=True))
    p = e / jnp.sum(e, axis=0, keepdims=True)
    io = lax.broadcasted_iota(I32, p.shape, 0)
    m1 = jnp.max(p, axis=0, keepdims=True)
    i1 = jnp.min(jnp.where(p == m1, io, N_EXP), axis=0, keepdims=True)
    p2 = jnp.where(io == i1, -1.0, p)
    m2 = jnp.max(p2, axis=0, keepdims=True)
    i2 = jnp.min(jnp.where(p2 == m2, io, N_EXP), axis=0, keepdims=True)
    den = m1 + m2
    idx_ref[...] = jnp.concatenate([i1, i2], axis=0)
    wt_ref[...] = jnp.concatenate([m1 / den, m2 / den], axis=0)
    chosen = jnp.where(io == i1, 1.0, 0.0) + jnp.where(io == i2, 1.0, 0.0)

    @pl.when(pl.program_id(0) == 0)
    def _():
        cnt_ref[...] = jnp.zeros_like(cnt_ref)

    cnt_ref[...] += jnp.broadcast_to(jnp.sum(chosen, axis=1, keepdims=True), cnt_ref.shape)


def _router(x, mods_l, gain, router_t):
    return pl.pallas_call(
        _router_kernel,
        grid=(R // TM,),
        in_specs=[
            pl.BlockSpec((TM, D), lambda i: (i, 0)),
            pl.BlockSpec((None, N_MOD, D), lambda i: (_cond_idx(i, TM), 0, 0)),
            pl.BlockSpec((1, D), lambda i: (0, 0)),
            pl.BlockSpec((2 * SUB, D), lambda i: (0, 0)),
        ],
        out_specs=[
            pl.BlockSpec((TM * ROW_TILE, LANES), lambda i: (i, 0)),
            pl.BlockSpec((TOP_K, TM), lambda i: (0, i)),
            pl.BlockSpec((TOP_K, TM), lambda i: (0, i)),
            pl.BlockSpec((N_EXP, LANES), lambda i: (0, 0)),
        ],
        out_shape=[
            jax.ShapeDtypeStruct((R * ROW_TILE, LANES), F32),
            jax.ShapeDtypeStruct((TOP_K, R), I32),
            jax.ShapeDtypeStruct((TOP_K, R), F32),
            jax.ShapeDtypeStruct((N_EXP, LANES), F32),
        ],
        compiler_params=_cparams(("arbitrary",)),
        name="moe_router",
    )(x, mods_l, gain, router_t)


IDX_BLOCK = 1024
DMA_UNROLL = 8


def _slot_kernel(e_ref, gs_ref, pos_ref, tri_scr, carry_scr):
    @pl.when(pl.program_id(0) == 0)
    def _():
        r = lax.broadcasted_iota(I32, tri_scr.shape, 0)
        c = lax.broadcasted_iota(I32, tri_scr.shape, 1)
        tri_scr[...] = jnp.where(r <= c, 1.0, 0.0).astype(BF16)
        carry_scr[...] = jnp.zeros_like(carry_scr)

    e = e_ref[...]
    io = lax.broadcasted_iota(I32, (2 * SUB, IDX_BLOCK), 0)
    onehot = jnp.where(io == e, 1.0, 0.0)
    cum = _dot(onehot.astype(BF16), tri_scr[...])
    carry = carry_scr[...]
    slot = cum - 1.0 + carry[:, 0:1] + gs_ref[...][:, 0:1]
    pos_ref[...] = jnp.sum(onehot * slot, axis=0, keepdims=True).astype(I32)
    carry_scr[...] = carry + jnp.broadcast_to(cum[:, IDX_BLOCK - 1:IDX_BLOCK], carry.shape)


def _slots(idx, group_start):
    n_blk = TOP_K * R // IDX_BLOCK
    gs = jnp.broadcast_to(
        jnp.concatenate([group_start, jnp.zeros((2 * SUB - N_EXP,), I32)]).astype(F32)[:, None], (2 * SUB, LANES))
    pos = pl.pallas_call(
        _slot_kernel,
        grid=(n_blk,),
        in_specs=[
            pl.BlockSpec((None, 1, IDX_BLOCK), lambda c: (c, 0, 0)),
            pl.BlockSpec((2 * SUB, LANES), lambda c: (0, 0)),
        ],
        out_specs=pl.BlockSpec((None, 1, IDX_BLOCK), lambda c: (c, 0, 0)),
        out_shape=jax.ShapeDtypeStruct((n_blk, 1, IDX_BLOCK), I32),
        scratch_shapes=[pltpu.VMEM((IDX_BLOCK, IDX_BLOCK), BF16), pltpu.VMEM((2 * SUB, LANES), F32)],
        compiler_params=_cparams(("arbitrary",)),
        name="moe_slots",
    )(idx.reshape(n_blk, 1, IDX_BLOCK), gs)
    return pos.reshape(TOP_K * R)


def _dispatch_kernel(p0_ref, p1_ref, h8_ref, init_ref, hs8_ref, sem):
    del init_ref
    n_tok = h8_ref.shape[0] // ROW_TILE
    for p_ref in (p0_ref, p1_ref):
        def issue(g, carry, p_ref=p_ref):
            for u in range(DMA_UNROLL):
                r = g * DMA_UNROLL + u
                pltpu.make_async_copy(
                    h8_ref.at[pl.ds(pl.multiple_of(r * SUB, SUB), SUB)],
                    hs8_ref.at[pl.ds(pl.multiple_of(p_ref[r] * SUB, SUB), SUB)],
                    sem).start()
            return carry

        lax.fori_loop(0, n_tok // DMA_UNROLL, issue, 0)
    for _ in range(TOP_K):
        pltpu.make_async_copy(h8_ref, hs8_ref.at[pl.ds(0, n_tok * ROW_TILE)], sem).wait()


def _dispatch(pos, h8):
    nt = R // IDX_BLOCK
    zeros = jnp.zeros((N_SLOT * ROW_TILE, LANES), F32)
    return pl.pallas_call(
        _dispatch_kernel,
        grid=(nt,),
        in_specs=[
            pl.BlockSpec((IDX_BLOCK,), lambda i: (i,), memory_space=pltpu.SMEM),
            pl.BlockSpec((IDX_BLOCK,), lambda i: (nt + i,), memory_space=pltpu.SMEM),
            pl.BlockSpec((IDX_BLOCK * ROW_TILE, LANES), lambda i: (i, 0)),
            pl.BlockSpec(memory_space=pl.ANY),
        ],
        out_specs=pl.BlockSpec(memory_space=pl.ANY),
        out_shape=jax.ShapeDtypeStruct((N_SLOT * ROW_TILE, LANES), F32),
        scratch_shapes=[pltpu.SemaphoreType.DMA(())],
        input_output_aliases={3: 0},
        compiler_params=_cparams(("arbitrary",)),
        name="moe_dispatch",
    )(pos, pos, h8, zeros)


EXP_CHUNK = 1792
N_EXP_CHUNK = E_FF // EXP_CHUNK


def _expert_kernel(te_ref, tv_ref, x8_ref, w1_ref, w3_ref, w2_ref, y8_ref, xb_scr, acc_scr):
    i = pl.program_id(0)
    kc = pl.program_id(1)
    valid = tv_ref[i] == 1

    @pl.when(valid & (kc == 0))
    def _():
        for s in range(ROW_TILE):
            xb_scr[:, s * LANES:(s + 1) * LANES] = x8_ref[pl.ds(s, TM_MOE, stride=ROW_TILE), :].astype(BF16)
        acc_scr[...] = jnp.zeros_like(acc_scr)

    @pl.when(valid)
    def _():
        x = xb_scr[...]
        a = _dot(x, w1_ref[...])
        b = _dot(x, w3_ref[...])
        acc_scr[...] += _dot((_silu(a) * b).astype(BF16), w2_ref[...])

    @pl.when(valid & (kc == N_EXP_CHUNK - 1))
    def _():
        for s in range(ROW_TILE):
            y8_ref[pl.ds(s, TM_MOE, stride=ROW_TILE), :] = acc_scr[:, s * LANES:(s + 1) * LANES]

    @pl.when(jnp.logical_not(valid) & (kc == N_EXP_CHUNK - 1))
    def _():
        y8_ref[...] = jnp.zeros_like(y8_ref)


def _experts(tile_expert, tile_valid, hs8, w1, w3, w2, j):
    def kc_eff(kc, tv, i):
        return jnp.where(tv[i] == 1, kc, N_EXP_CHUNK - 1)

    grid_spec = pltpu.PrefetchScalarGridSpec(
        num_scalar_prefetch=2,
        grid=(N_TILE, N_EXP_CHUNK),
        in_specs=[
            pl.BlockSpec((TM_MOE * ROW_TILE, LANES), lambda i, kc, te, tv: (i, 0)),
            pl.BlockSpec((None, None, D, EXP_CHUNK), lambda i, kc, te, tv: (j, te[i], 0, kc_eff(kc, tv, i))),
            pl.BlockSpec((None, None, D, EXP_CHUNK), lambda i, kc, te, tv: (j, te[i], 0, kc_eff(kc, tv, i))),
            pl.BlockSpec((None, None, EXP_CHUNK, D), lambda i, kc, te, tv: (j, te[i], kc_eff(kc, tv, i), 0)),
        ],
        out_specs=pl.BlockSpec((TM_MOE * ROW_TILE, LANES), lambda i, kc, te, tv: (i, 0)),
        scratch_shapes=[pltpu.VMEM((TM_MOE, D), BF16), pltpu.VMEM((TM_MOE, D), F32)],
    )
    return pl.pallas_call(
        _expert_kernel,
        grid_spec=grid_spec,
        out_shape=jax.ShapeDtypeStruct((N_SLOT * ROW_TILE, LANES), F32),
        compiler_params=_cparams(("arbitrary", "arbitrary"), 56),
        name="moe_experts",
    )(tile_expert, tile_valid, hs8, w1, w3, w2)


TILES_PER_IDX_BLOCK = IDX_BLOCK // TM


def _moe_resid_kernel(p0_ref, p1_ref, y8_ref, wc_ref, x_ref, m_ref, fn_ref, o_ref, ya_scr, yb_scr, sems, *, final):
    i = pl.program_id(0)
    nt = pl.num_programs(0)
    tm = x_ref.shape[0]
    bufs = (ya_scr, yb_scr)

    def start(tile, b):
        off = (tile % TILES_PER_IDX_BLOCK) * tm
        for c, p_ref in enumerate((p0_ref, p1_ref)):
            def issue(g, carry, p_ref=p_ref, c=c):
                for u in range(DMA_UNROLL):
                    r = g * DMA_UNROLL + u
                    pltpu.make_async_copy(
                        y8_ref.at[pl.ds(pl.multiple_of(p_ref[off + r] * SUB, SUB), SUB)],
                        bufs[b].at[c, pl.ds(pl.multiple_of(r * SUB, SUB), SUB)],
                        sems.at[b]).start()
                return carry

            lax.fori_loop(0, tm // DMA_UNROLL, issue, 0)

    def finish(b):
        for c in range(TOP_K):
            pltpu.make_async_copy(y8_ref.at[pl.ds(0, tm * ROW_TILE)], bufs[b].at[c], sems.at[b]).wait()

    def rows(ref):
        return jnp.concatenate([ref[pl.ds(s, tm, stride=ROW_TILE), :] for s in range(ROW_TILE)], axis=1)

    def combine(b):
        w = wc_ref[...]
        f = w[:, 0:1] * rows(bufs[b].at[0]) + w[:, 1:2] * rows(bufs[b].at[1])
        xn = x_ref[...] + m_ref[...][5:6] * f
        if final:
            xn = _rms(xn, fn_ref[...])
        o_ref[...] = xn

    @pl.when(i == 0)
    def _():
        start(0, 0)

    for b in range(2):
        @pl.when(i % 2 == b)
        def _(b=b):
            @pl.when(i + 1 < nt)
            def _():
                start(i + 1, 1 - b)

            finish(b)
            combine(b)


def _moe_resid(pos, y8, wcol, x, mods_l, final_gain, final):
    nt = R // TM
    nblk = R // IDX_BLOCK

    def next_blk(i):
        return jnp.minimum(i + 1, nt - 1) // TILES_PER_IDX_BLOCK

    return pl.pallas_call(
        functools.partial(_moe_resid_kernel, final=final),
        grid=(nt,),
        in_specs=[
            pl.BlockSpec((IDX_BLOCK,), lambda i: (next_blk(i),), memory_space=pltpu.SMEM),
            pl.BlockSpec((IDX_BLOCK,), lambda i: (nblk + next_blk(i),), memory_space=pltpu.SMEM),
            pl.BlockSpec(memory_space=pl.ANY),
            pl.BlockSpec((TM, TOP_K), lambda i: (i, 0)),
            pl.BlockSpec((TM, D), lambda i: (i, 0)),
            pl.BlockSpec((None, N_MOD, D), lambda i: (_cond_idx(i, TM), 0, 0)),
            pl.BlockSpec((1, D), lambda i: (0, 0)),
        ],
        out_specs=pl.BlockSpec((TM, D), lambda i: (i, 0)),
        out_shape=jax.ShapeDtypeStruct((R, D), F32),
        scratch_shapes=[pltpu.VMEM((TOP_K, TM * ROW_TILE, LANES), F32), pltpu.VMEM((TOP_K, TM * ROW_TILE, LANES), F32),
                        pltpu.SemaphoreType.DMA((2,))],
        compiler_params=_cparams(("arbitrary",), 48),
        name="moe_resid",
    )(pos, pos, y8, wcol, x, mods_l, final_gain)


def _tile_tables(counts):
    padded = ((counts + TM_MOE - 1) // TM_MOE) * TM_MOE
    gend = jnp.cumsum(padded)
    tile_start = jnp.arange(N_TILE, dtype=I32) * TM_MOE
    te = jnp.sum((tile_start[:, None] >= gend[None, :]).astype(I32), axis=1)
    valid = tile_start < gend[-1]
    te_last = te[gend[-1] // TM_MOE - 1]
    te = jnp.minimum(jnp.where(valid, te, te_last), N_EXP - 1)
    return (gend - padded).astype(I32), te.astype(I32), valid.astype(I32)


def _rot_half(w):
    wa = w.reshape(w.shape[:-1] + (2, 2, ROPE_AXIS // 2))
    return jnp.stack([-wa[..., 1, :], wa[..., 0, :]], axis=-2).reshape(w.shape)


def _rope_tables(tm):
    rows = DEC_SEQ // GRID_W
    r = jnp.repeat(jnp.arange(rows), GRID_W).astype(F32)
    c = jnp.tile(jnp.arange(GRID_W), rows).astype(F32)
    inv = ROPE_THETA ** (-jnp.arange(0, ROPE_AXIS, 2, dtype=F32) / ROPE_AXIS)
    ang_r = r[:, None] * inv
    ang_c = c[:, None] * inv
    ang = jnp.concatenate([ang_r, ang_r, ang_c, ang_c], axis=-1)
    cos, sin = jnp.cos(ang), jnp.sin(ang)
    z32 = jnp.zeros_like(cos)
    one = jnp.ones((tm, QK_ROPE), F32)
    zt = jnp.zeros((tm, QK_ROPE), F32)
    tq_c = jnp.concatenate([jnp.concatenate([cos, cos, z32, z32], 1), jnp.concatenate([one, one, zt, zt], 1)], 0)
    tq_s = jnp.concatenate([jnp.concatenate([sin, sin, z32, z32], 1), jnp.zeros((tm, LANES), F32)], 0)
    tk_c = jnp.concatenate([jnp.concatenate([cos, z32, z32, z32], 1), jnp.concatenate([one, zt, zt, zt], 1)], 0)
    tk_s = jnp.concatenate([jnp.concatenate([sin, z32, z32, z32], 1), jnp.zeros((tm, LANES), F32)], 0)
    return tq_c, tq_s, tk_c, tk_s


def _mla_weights(w_dq, w_uq, w_dkv, w_uk, w_uv, w_o):
    kr_w = w_dkv[:, KV_LORA:]
    w_a = jnp.concatenate(
        [w_dq, w_dkv[:, :KV_LORA], kr_w, _rot_half(kr_w), jnp.zeros((D, LANES - 2 * QK_ROPE), F32)], axis=1)
    uq = w_uq.reshape(Q_LORA, N_PAIR, 2, QK_NOPE + QK_ROPE)
    nope = uq[..., :QK_NOPE].reshape(Q_LORA, N_PAIR, 2 * QK_NOPE)
    rope = uq[..., QK_NOPE:]
    w_uq2 = jnp.concatenate(
        [nope, rope.reshape(Q_LORA, N_PAIR, 2 * QK_ROPE), _rot_half(rope).reshape(Q_LORA, N_PAIR, 2 * QK_ROPE)],
        axis=-1).reshape(Q_LORA, N_PAIR * 2 * LANES)
    uk = w_uk.reshape(KV_LORA, N_PAIR, 2 * QK_NOPE)
    wk = jnp.concatenate([uk, jnp.zeros((KV_LORA, N_PAIR, LANES), F32)], axis=-1).reshape(KV_LORA, N_PAIR * 2 * LANES)
    eye = jnp.eye(LANES, QK_ROPE, dtype=F32)
    pair = jnp.concatenate([jnp.zeros((LANES, LANES), F32), eye, eye, jnp.zeros((LANES, LANES - 2 * QK_ROPE), F32)], 1)
    wkr = jnp.tile(pair, (1, N_PAIR))
    wv = w_uv.reshape(KV_LORA, HEADS * V_HEAD)
    wo = w_o.reshape(HEADS * V_HEAD, D)
    return [w.astype(BF16) for w in (w_a, w_uq2, wk, wkr, wv, wo)]


def kernel(x_prompt, x_sample, cache_ckv, cache_krope, state_hgrn, c, c_ctx, w_ada, b_ada, norm_mix, norm_ffn, mla_w_dq, mla_q_norm, mla_w_uq, mla_w_dkv, mla_kv_norm, mla_w_uk, mla_w_uv, mla_w_o, hg_w_q, hg_w_f, hg_w_i, hg_w_g, hg_lb_logits, hg_o_norm, hg_w_o, ffn_w1, ffn_w3, ffn_w2, moe_router, moe_w1, moe_w3, moe_w2, final_norm):
    x = (x_prompt.reshape(R_P, D), x_sample.reshape(R_S, D))
    cond = jnp.concatenate([c, c_ctx[None], jnp.zeros((COND_PAD - N_COND, D), F32)], axis=0)
    mods = _ada(cond, w_ada, b_ada)
    lb_all = _lower_bounds(hg_lb_logits)
    tq_c, tq_s, tk_c, tk_s = _rope_tables(TM)
    ffn_w = [w.astype(BF16) for w in (ffn_w1, ffn_w3, ffn_w2)]
    moe_w = [w.astype(BF16) for w in (moe_w1, moe_w3, moe_w2)]
    new_ckv, new_krope, new_hgrn = [], [], []

    for l in range(DEPTH):
        j = l // 2
        mods_l = mods[l]
        gain_mix = norm_mix[l][None]
        gain_ffn = norm_ffn[l][None]
        if l % 2 == 0:
            w_a, w_uq2, wk, wkr, wv, wo = _mla_weights(
                mla_w_dq[j], mla_w_uq[j], mla_w_dkv[j], mla_w_uk[j], mla_w_uv[j], mla_w_o[j])
            cq, ckv, kr = _mla_proj(x, mods_l, gain_mix, w_a, mla_q_norm[j][None], mla_kv_norm[j][None], tk_c, tk_s)
            q2 = _q_up(cq, w_uq2, tq_c, tq_s)
            k2_p, v_p = _kv_up(ckv, kr, R_P, wk, wkr, wv)
            ckv_s = jnp.concatenate([cache_ckv[:, j], ckv[R_P:].reshape(DEC_BATCH, DEC_SEQ, KV_LORA)], axis=1)
            kr_cache = jnp.pad(cache_krope[:, j], ((0, 0), (0, 0), (0, LANES - QK_ROPE)))
            kr_s = jnp.concatenate([kr_cache, kr[R_P:].reshape(DEC_BATCH, DEC_SEQ, LANES)], axis=1)
            s_all = PAST + DEC_SEQ
            k2_s, v_s = _kv_up(ckv_s.reshape(DEC_BATCH * s_all, KV_LORA), kr_s.reshape(DEC_BATCH * s_all, LANES),
                               DEC_BATCH * s_all, wk, wkr, wv)
            o_p = _attention(q2, k2_p, v_p, BATCH, SEQ, SEQ, 0, SEQ, 4)
            o_s = _attention(q2, k2_s, v_s, DEC_BATCH, DEC_SEQ, s_all, R_P, 512, 1)
            x = _mm_resid((o_p, o_s), wo, x, mods_l, 2)
            new_ckv.append(ckv[:R_P].reshape(BATCH, SEQ, KV_LORA))
            new_krope.append(kr[:R_P, :QK_ROPE].reshape(BATCH, SEQ, QK_ROPE))
        else:
            w5 = jnp.stack([hg_w_q[j], hg_w_f[j, 0], hg_w_f[j, 1], hg_w_i[j], hg_w_g[j]]).astype(BF16)
            q, k, lf, v, gg = _hg_proj(x, mods_l, gain_mix, w5, lb_all[j])
            s_zero = jnp.zeros((BATCH, 2, HG_H, HG_DK, HG_DK), F32)
            o2_p, st_p = _gla(q, k, lf, v, s_zero, BATCH, SEQ, 0)
            o2_s, _ = _gla(q, k, lf, v, state_hgrn[:, j], DEC_BATCH, DEC_SEQ, R_P)
            x = _hg_out(o2_p, o2_s, gg, hg_o_norm[j][None], hg_w_o[j].astype(BF16), x, mods_l)
            new_hgrn.append(st_p)
        if l % 2 == 0:
            x = _ffn(x, mods_l, gain_ffn, *ffn_w, j)
        else:
            router_t = jnp.concatenate([moe_router[j].T, jnp.zeros((2 * SUB - N_EXP, D), F32)], axis=0)
            h8, idx, wts, cnt = _router(x, mods_l, gain_ffn, router_t)
            group_start, tile_expert, tile_valid = _tile_tables(cnt[:, 0].astype(I32))
            pos = _slots(idx, group_start)
            y8 = _experts(tile_expert, tile_valid, _dispatch(pos, h8), *moe_w, j)
            x = _moe_resid(pos, y8, wts.T, x, mods_l, final_norm[None], final=(l == DEPTH - 1))

    y_prompt = x[:R_P].reshape(BATCH, SEQ, D)
    y_sample = x[R_P:].reshape(DEC_BATCH, DEC_SEQ, D)
    return (y_prompt, y_sample, jnp.stack(new_ckv, axis=1), jnp.stack(new_krope, axis=1),
            jnp.stack(new_hgrn, axis=1))
```

```python
import functools
import math

import jax
import jax.numpy as jnp
from jax import lax
from jax.experimental import pallas as pl
from jax.experimental.pallas import tpu as pltpu

F32 = jnp.float32
BF16 = jnp.bfloat16
I32 = jnp.int32

D = 1024
BATCH, SEQ = 32, 256
DEC_BATCH, DEC_SEQ = 8, 1024
PAST = 256
DEPTH = 4
R_P = BATCH * SEQ
R_S = DEC_BATCH * DEC_SEQ
R = R_P + R_S
N_COND = DEC_BATCH + 1
COND_PAD = 16
N_MOD = 6
EPS = 1e-6
GRID_W = 64
HEADS, QK_NOPE, QK_ROPE, V_HEAD = 16, 64, 32, 64
Q_LORA, KV_LORA = 512, 256
N_PAIR = HEADS // 2
ROPE_AXIS = QK_ROPE // 2
ROPE_THETA = 10000.0
HG_H, HG_DK = 8, 128
BLK = 128
CHUNK = 16
D_FF = 2816
N_EXP, TOP_K, E_FF = 8, 2, 3584
LANES = 128
SUB = 8
ROW_TILE = D // LANES
TM = 512
TM_MOE = 512
N_SLOT = TOP_K * R + N_EXP * TM_MOE
N_TILE = N_SLOT // TM_MOE
MIB = 1024 * 1024
assert DEPTH % 2 == 0


def _cparams(sem, vmem_mib=40):
    return pltpu.CompilerParams(dimension_semantics=sem, vmem_limit_bytes=vmem_mib * MIB)


def _cond_idx(i, tm):
    r0 = i * tm
    return jnp.where(r0 < R_P, DEC_BATCH, (r0 - R_P) // DEC_SEQ)


def _rope_idx(i, tm):
    nb = DEC_SEQ // tm
    r0 = i * tm
    return jnp.where(r0 < R_P, nb, ((r0 - R_P) // tm) % nb)


def _normmod(x, gain, scale, shift):
    ms = jnp.mean(x * x, axis=-1, keepdims=True)
    return (x * lax.rsqrt(ms + EPS)) * gain * (1.0 + scale) + shift


def _rms(x, gain):
    ms = jnp.mean(x * x, axis=-1, keepdims=True)
    return (x * lax.rsqrt(ms + EPS)) * gain


def _silu(x):
    return x / (1.0 + jnp.exp(-x))


def _dot(a, b):
    return jnp.dot(a, b, preferred_element_type=F32)


def _dot_nt(a, b):
    return lax.dot_general(a, b, (((1,), (1,)), ((), ())), preferred_element_type=F32)


def _split3(x):
    hi = x.astype(BF16)
    r1 = x - hi.astype(F32)
    mid = r1.astype(BF16)
    lo = (r1 - mid.astype(F32)).astype(BF16)
    return hi, mid, lo


def _ada_kernel(c_ref, w_ref, b_ref, o_ref):
    o_ref[...] = _dot(_silu(c_ref[...]), w_ref[...]) + b_ref[...]


def _ada(cond, w_ada, b_ada):
    tn = 1536
    out = pl.pallas_call(
        _ada_kernel,
        grid=(DEPTH, N_MOD * D // tn),
        in_specs=[
            pl.BlockSpec((COND_PAD, D), lambda l, j: (0, 0)),
            pl.BlockSpec((None, D, tn), lambda l, j: (l, 0, j)),
            pl.BlockSpec((None, 1, tn), lambda l, j: (l, 0, j)),
        ],
        out_specs=pl.BlockSpec((None, COND_PAD, tn), lambda l, j: (l, 0, j)),
        out_shape=jax.ShapeDtypeStruct((DEPTH, COND_PAD, N_MOD * D), F32),
        compiler_params=_cparams(("arbitrary", "arbitrary")),
        name="ada",
    )(cond, w_ada, b_ada.reshape(DEPTH, 1, N_MOD * D))
    return out[:, :N_COND].reshape(DEPTH, N_COND, N_MOD, D)


def _lb_kernel(x_ref, o_ref):
    x = x_ref[...]
    e = jnp.exp(x - jnp.max(x, axis=0, keepdims=True))
    sm = e / jnp.sum(e, axis=0, keepdims=True)
    n = x.shape[0]
    cum = sm[0]
    o_ref[0] = jnp.zeros_like(cum)
    for i in range(1, n):
        cum_i = cum + sm[i]
        o_ref[i] = cum_i - sm[0]
        cum = cum_i


def _lower_bounds(lb_logits):
    return pl.pallas_call(
        _lb_kernel,
        out_shape=jax.ShapeDtypeStruct(lb_logits.shape, F32),
        name="hgrn_lower_bounds",
    )(lb_logits)


def _two_part_rows(x):
    if isinstance(x, tuple):
        x_p, x_s = x
        s_off = 0
    else:
        x_p = x_s = x
        s_off = N_TILE_P
    w = x_p.shape[1]
    specs = [pl.BlockSpec((TM, w), lambda i: (_prompt_blk(i), 0)),
             pl.BlockSpec((TM, w), lambda i: (s_off + _sample_blk(i), 0))]
    return [x_p, x_s], specs


def _pick_rows(p_ref, s_ref):
    return jnp.where(pl.program_id(0) < N_TILE_P, p_ref[...], s_ref[...])


def _mla_proj_kernel(xp_ref, xs_ref, m_ref, g_ref, w_ref, qn_ref, kvn_ref, tc_ref, ts_ref,
                     cq_ref, ckv_ref, kr_ref):
    m = m_ref[...]
    h = _normmod(_pick_rows(xp_ref, xs_ref), g_ref[...], m[1:2], m[0:1]).astype(BF16)
    y = _dot(h, w_ref[...])
    cq_ref[...] = _rms(y[:, :Q_LORA], qn_ref[...]).astype(BF16)
    ckv_ref[...] = _rms(y[:, Q_LORA:Q_LORA + KV_LORA], kvn_ref[...])
    slab = y[:, Q_LORA + KV_LORA:]
    kr_ref[...] = slab * tc_ref[...] + pltpu.roll(slab, LANES - QK_ROPE, 1) * ts_ref[...]


def _mla_proj(x, mods_l, gain, w_a, q_norm, kv_norm, tk_c, tk_s):
    n_a = w_a.shape[1]
    row = lambda i: (i, 0)
    fixed = lambda i: (0, 0)
    x_args, x_specs = _two_part_rows(x)
    return pl.pallas_call(
        _mla_proj_kernel,
        grid=(R // TM,),
        in_specs=x_specs + [
            pl.BlockSpec((None, N_MOD, D), lambda i: (_cond_idx(i, TM), 0, 0)),
            pl.BlockSpec((1, D), fixed),
            pl.BlockSpec((D, n_a), fixed),
            pl.BlockSpec((1, Q_LORA), fixed),
            pl.BlockSpec((1, KV_LORA), fixed),
            pl.BlockSpec((TM, LANES), lambda i: (_rope_idx(i, TM), 0)),
            pl.BlockSpec((TM, LANES), lambda i: (_rope_idx(i, TM), 0)),
        ],
        out_specs=[
            pl.BlockSpec((TM, Q_LORA), row),
            pl.BlockSpec((TM, KV_LORA), row),
            pl.BlockSpec((TM, LANES), row),
        ],
        out_shape=[
            jax.ShapeDtypeStruct((R, Q_LORA), BF16),
            jax.ShapeDtypeStruct((R, KV_LORA), F32),
            jax.ShapeDtypeStruct((R, LANES), F32),
        ],
        compiler_params=_cparams(("arbitrary",)),
        name="mla_proj",
    )(*x_args, mods_l, gain, w_a, q_norm, kv_norm, tk_c, tk_s)


QK_SCALE_LOG2 = math.log2(math.e) / math.sqrt(QK_NOPE + QK_ROPE)


def _q_up_kernel(cq_ref, w_ref, tc_ref, ts_ref, q_ref):
    y = _dot(cq_ref[...], w_ref[...]) * QK_SCALE_LOG2
    tc = tc_ref[...]
    ts = ts_ref[...]
    for p in range(N_PAIR):
        lo = p * 2 * LANES
        q_ref[:, lo:lo + LANES] = y[:, lo:lo + LANES].astype(BF16)
        hi = y[:, lo + LANES:lo + 2 * LANES]
        q_ref[:, lo + LANES:lo + 2 * LANES] = (hi * tc + pltpu.roll(hi, LANES // 2, 1) * ts).astype(BF16)


def _q_up(cq, w_uq2, tq_c, tq_s):
    nq = w_uq2.shape[1]
    return pl.pallas_call(
        _q_up_kernel,
        grid=(R // TM,),
        in_specs=[
            pl.BlockSpec((TM, Q_LORA), lambda i: (i, 0)),
            pl.BlockSpec((Q_LORA, nq), lambda i: (0, 0)),
            pl.BlockSpec((TM, LANES), lambda i: (_rope_idx(i, TM), 0)),
            pl.BlockSpec((TM, LANES), lambda i: (_rope_idx(i, TM), 0)),
        ],
        out_specs=pl.BlockSpec((TM, nq), lambda i: (i, 0)),
        out_shape=jax.ShapeDtypeStruct((R, nq), BF16),
        compiler_params=_cparams(("arbitrary",)),
        name="mla_q_up",
    )(cq, w_uq2, tq_c, tq_s)


def _kv_up_kernel(c_ref, r_ref, wk_ref, wkr_ref, wv_ref, k_ref, v_ref):
    c = c_ref[...].astype(BF16)
    r = r_ref[...].astype(BF16)
    k_ref[...] = (_dot(c, wk_ref[...]) + _dot(r, wkr_ref[...])).astype(BF16)
    v_ref[...] = _dot(c, wv_ref[...]).astype(BF16)


def _kv_up(ckv, kr, n_rows, wk, wkr, wv):
    nk = wk.shape[1]
    nv = wv.shape[1]
    fixed = lambda i: (0, 0)
    return pl.pallas_call(
        _kv_up_kernel,
        grid=(n_rows // TM,),
        in_specs=[
            pl.BlockSpec((TM, KV_LORA), lambda i: (i, 0)),
            pl.BlockSpec((TM, LANES), lambda i: (i, 0)),
            pl.BlockSpec((KV_LORA, nk), fixed),
            pl.BlockSpec((LANES, nk), fixed),
            pl.BlockSpec((KV_LORA, nv), fixed),
        ],
        out_specs=[
            pl.BlockSpec((TM, nk), lambda i: (i, 0)),
            pl.BlockSpec((TM, nv), lambda i: (i, 0)),
        ],
        out_shape=[
            jax.ShapeDtypeStruct((n_rows, nk), BF16),
            jax.ShapeDtypeStruct((n_rows, nv), BF16),
        ],
        compiler_params=_cparams(("arbitrary",)),
        name="mla_kv_up",
    )(ckv, kr, wk, wkr, wv)


def _attn_kernel(q_ref, k_ref, v_ref, *rest, n_pair):
    o_ref = rest[-1]
    lq = lax.broadcasted_iota(I32, (1, 2 * LANES), 1)
    lv = lax.broadcasted_iota(I32, (1, LANES), 1)
    sel_a = (lq < QK_NOPE) | ((lq >= LANES) & (lq < LANES + QK_ROPE))
    sel_b = ((lq >= QK_NOPE) & (lq < LANES)) | ((lq >= LANES + QK_ROPE) & (lq < LANES + 2 * QK_ROPE))
    for p in range(n_pair):
        q = q_ref[:, p * 2 * LANES:(p + 1) * 2 * LANES]
        k = k_ref[:, p * 2 * LANES:(p + 1) * 2 * LANES]
        v = v_ref[:, p * LANES:(p + 1) * LANES]
        zq = jnp.zeros_like(q)
        zv = jnp.zeros_like(v)
        out = None
        for sel, vsel in ((sel_a, lv < V_HEAD), (sel_b, lv >= V_HEAD)):
            s = _dot_nt(jnp.where(sel, q, zq), k)
            e = jnp.exp2(s - jnp.max(s, axis=-1, keepdims=True))
            den = jnp.sum(e, axis=-1, keepdims=True)
            o = _dot(e.astype(BF16), jnp.where(vsel, v, zv)) / den
            out = o if out is None else out + o
        o_ref[:, p * LANES:(p + 1) * LANES] = out.astype(BF16)


def _attention(q2, k2, v, n_batch, t_len, s_len, q_row0, tq, n_pair):
    nq = t_len // tq
    qb0 = q_row0 // tq
    return pl.pallas_call(
        functools.partial(_attn_kernel, n_pair=n_pair),
        grid=(n_batch, N_PAIR // n_pair, nq),
        in_specs=[
            pl.BlockSpec((tq, n_pair * 2 * LANES), lambda b, p, i: (qb0 + b * nq + i, p)),
            pl.BlockSpec((s_len, n_pair * 2 * LANES), lambda b, p, i: (b, p)),
            pl.BlockSpec((s_len, n_pair * LANES), lambda b, p, i: (b, p)),
        ],
        out_specs=pl.BlockSpec((tq, n_pair * LANES), lambda b, p, i: (b * nq + i, p)),
        out_shape=jax.ShapeDtypeStruct((n_batch * t_len, HEADS * V_HEAD), BF16),
        compiler_params=_cparams(("arbitrary", "arbitrary", "arbitrary")),
        name="mla_attention",
    )(q2, k2, v)


N_TILE_P = R_P // TM


def _prompt_blk(i):
    return jnp.minimum(i, N_TILE_P - 1)


def _sample_blk(i):
    return jnp.maximum(i - N_TILE_P, 0)


def _mm_resid_kernel(ap_ref, as_ref, w_ref, xp_ref, xs_ref, m_ref, o_ref, *, gate_idx):
    gate = m_ref[...][gate_idx:gate_idx + 1]
    o_ref[...] = _pick_rows(xp_ref, xs_ref) + gate * _dot(_pick_rows(ap_ref, as_ref), w_ref[...])


def _mm_resid(a, w, x, mods_l, gate_idx):
    a_args, a_specs = _two_part_rows(a)
    x_args, x_specs = _two_part_rows(x)
    k = a_args[0].shape[1]
    return pl.pallas_call(
        functools.partial(_mm_resid_kernel, gate_idx=gate_idx),
        grid=(R // TM,),
        in_specs=a_specs + [pl.BlockSpec((k, D), lambda i: (0, 0))] + x_specs + [
            pl.BlockSpec((None, N_MOD, D), lambda i: (_cond_idx(i, TM), 0, 0)),
        ],
        out_specs=pl.BlockSpec((TM, D), lambda i: (i, 0)),
        out_shape=jax.ShapeDtypeStruct((R, D), F32),
        compiler_params=_cparams(("arbitrary",)),
        name="mm_resid",
    )(*a_args, w, *x_args, mods_l)


FFN_CHUNK = 1408


def _ffn_kernel(x_ref, m_ref, g_ref, w1_ref, w3_ref, w2_ref, o_ref):
    x = x_ref[...]
    m = m_ref[...]
    h = _normmod(x, g_ref[...], m[4:5], m[3:4]).astype(BF16)
    acc = jnp.zeros(x.shape, F32)
    for c in range(D_FF // FFN_CHUNK):
        sl = slice(c * FFN_CHUNK, (c + 1) * FFN_CHUNK)
        a = _dot(h, w1_ref[:, sl])
        b = _dot(h, w3_ref[:, sl])
        acc = acc + _dot((_silu(a) * b).astype(BF16), w2_ref[sl, :])
    o_ref[...] = x + m[5:6] * acc


def _ffn(x, mods_l, gain, w1, w3, w2, j):
    tm = TM
    fixed = lambda i: (0, 0)
    layer = lambda i: (j, 0, 0)
    once = pl.Buffered(1)
    return pl.pallas_call(
        _ffn_kernel,
        grid=(R // tm,),
        in_specs=[
            pl.BlockSpec((tm, D), lambda i: (i, 0)),
            pl.BlockSpec((None, N_MOD, D), lambda i: (_cond_idx(i, tm), 0, 0)),
            pl.BlockSpec((1, D), fixed),
            pl.BlockSpec((None, D, D_FF), layer, pipeline_mode=once),
            pl.BlockSpec((None, D, D_FF), layer, pipeline_mode=once),
            pl.BlockSpec((None, D_FF, D), layer, pipeline_mode=once),
        ],
        out_specs=pl.BlockSpec((tm, D), lambda i: (i, 0)),
        out_shape=jax.ShapeDtypeStruct((R, D), F32),
        compiler_params=_cparams(("arbitrary",), 56),
        name="dense_swiglu",
    )(x, mods_l, gain, w1, w3, w2)


def _forget_gate(z, lb):
    e = jnp.exp(-jnp.abs(z))
    log_sig = jnp.minimum(z, 0.0) - jnp.log(1.0 + e)
    a = jnp.log(lb)
    b = jnp.log1p(-lb) + log_sig
    log_f = jnp.maximum(a, b) + jnp.log(1.0 + jnp.exp(-jnp.abs(a - b)))
    k = (1.0 - lb) * (jnp.where(z >= 0, e, 1.0) / (1.0 + e))
    return k, log_f


def _hg_proj_kernel(x_ref, m_ref, g_ref, w_ref, lb_ref, q_ref, k_ref, lf_ref, v_ref, gg_ref):
    m = m_ref[...]
    h = _normmod(x_ref[...], g_ref[...], m[1:2], m[0:1]).astype(BF16)
    q_ref[...] = _dot(h, w_ref[0])
    for dr in range(2):
        k, log_f = _forget_gate(_dot(h, w_ref[1 + dr]), lb_ref[dr])
        k_ref[dr] = k
        lf_ref[dr] = log_f
    v_ref[...] = _dot(h, w_ref[3]).astype(BF16)
    gg_ref[...] = _dot(h, w_ref[4])


def _hg_proj(x, mods_l, gain, w5, lb):
    tm = TM
    row = lambda i: (i, 0)
    dirs = lambda i: (0, i, 0)
    return pl.pallas_call(
        _hg_proj_kernel,
        grid=(R // tm,),
        in_specs=[
            pl.BlockSpec((tm, D), row),
            pl.BlockSpec((None, N_MOD, D), lambda i: (_cond_idx(i, tm), 0, 0)),
            pl.BlockSpec((1, D), lambda i: (0, 0)),
            pl.BlockSpec((5, D, D), lambda i: (0, 0, 0), pipeline_mode=pl.Buffered(1)),
            pl.BlockSpec((2, 1, D), lambda i: (0, 0, 0)),
        ],
        out_specs=[
            pl.BlockSpec((tm, D), row),
            pl.BlockSpec((2, tm, D), dirs),
            pl.BlockSpec((2, tm, D), dirs),
            pl.BlockSpec((tm, D), row),
            pl.BlockSpec((tm, D), row),
        ],
        out_shape=[
            jax.ShapeDtypeStruct((R, D), F32),
            jax.ShapeDtypeStruct((2, R, D), F32),
            jax.ShapeDtypeStruct((2, R, D), F32),
            jax.ShapeDtypeStruct((R, D), BF16),
            jax.ShapeDtypeStruct((R, D), F32),
        ],
        compiler_params=_cparams(("arbitrary",), 48),
        name="hgrn_proj",
    )(x, mods_l, gain, w5, lb.reshape(2, 1, D))


LEVEL_HALVES = (64, 32, 16)


def _gla_kernel(q_ref, k_ref, lf_ref, v_ref, s0_ref, *rest, nb):
    o_ref, sn_ref, st_scr, b_scr = rest[-4:]
    d = pl.program_id(1)
    n = pl.program_id(2)

    @pl.when(n == 0)
    def _():
        for h in range(HG_H):
            st_scr[h] = s0_ref[h].T

    row = lax.broadcasted_iota(I32, (BLK, BLK), 0)
    col = lax.broadcasted_iota(I32, (BLK, BLK), 1)
    ut = row + d * (BLK - 1 - 2 * row)
    us = col + d * (BLK - 1 - 2 * col)
    causal = us <= ut
    tri = jnp.where(causal, 1.0, 0.0).astype(BF16)

    hi, mid, lo = _split3(lf_ref[...])
    b_scr[...] = _dot(tri, hi) + _dot(tri, mid) + _dot(tri, lo)

    level_masks = []
    for hs in LEVEL_HALVES:
        sh = int(math.log2(2 * hs))
        same = (ut >> sh) == (us >> sh)
        level_masks.append(same & ((ut & (2 * hs - 1)) >= hs) & ((us & (2 * hs - 1)) < hs))
    base_mask = ((ut >> 4) == (us >> 4)) & causal

    def split_rows(sl, half):
        parts = []
        for jr in range(BLK // (2 * half)):
            r0 = jr * 2 * half + half - 1
            r = jnp.where(d == 0, b_scr[r0:r0 + 1, sl], b_scr[r0 + 1:r0 + 2, sl])
            parts.append(jnp.broadcast_to(r, (2 * half, HG_DK)))
        return parts[0] if len(parts) == 1 else jnp.concatenate(parts, axis=0)

    for h in range(HG_H):
        sl = slice(h * HG_DK, (h + 1) * HG_DK)
        bh = b_scr[:, sl]
        q = q_ref[:, sl]
        k = k_ref[:, sl]
        v = v_ref[:, sl]

        xq = bh - split_rows(sl, CHUNK // 2)
        att = jnp.where(base_mask, _dot_nt((q * jnp.exp(xq)).astype(BF16), (k * jnp.exp(-xq)).astype(BF16)), 0.0)
        for hs, msk in zip(LEVEL_HALVES, level_masks):
            e = jnp.exp(-jnp.abs(bh - split_rows(sl, hs)))
            att = jnp.where(msk, _dot_nt((q * e).astype(BF16), (k * e).astype(BF16)), att)

        b_last = jnp.where(d == 0, b_scr[BLK - 1:BLK, sl], b_scr[0:1, sl])
        q_in = (q * jnp.exp(bh)).astype(BF16)
        k_in = (k * jnp.exp(b_last - bh)).astype(BF16)
        st = st_scr[h]
        o_ref[:, sl] = _dot(att.astype(BF16), v) + _dot_nt(q_in, st.astype(BF16))
        v_t = v.astype(F32).T.astype(BF16)
        st_scr[h] = st * jnp.exp(b_last) + _dot(v_t, k_in)

    @pl.when(n == nb - 1)
    def _():
        for h in range(HG_H):
            sn_ref[h] = st_scr[h].T


def _gla(q, k, lf, v, s0, n_batch, t_len, row0):
    nb = t_len // BLK
    rb0 = row0 // BLK

    def rb(b, d, n):
        return b * nb + n + d * (nb - 1 - 2 * n)

    return pl.pallas_call(
        functools.partial(_gla_kernel, nb=nb),
        grid=(n_batch, 2, nb),
        in_specs=[
            pl.BlockSpec((BLK, D), lambda b, d, n: (rb0 + rb(b, d, n), 0)),
            pl.BlockSpec((None, BLK, D), lambda b, d, n: (d, rb0 + rb(b, d, n), 0)),
            pl.BlockSpec((None, BLK, D), lambda b, d, n: (d, rb0 + rb(b, d, n), 0)),
            pl.BlockSpec((BLK, D), lambda b, d, n: (rb0 + rb(b, d, n), 0)),
            pl.BlockSpec((None, None, HG_H, HG_DK, HG_DK), lambda b, d, n: (b, d, 0, 0, 0)),
        ],
        out_specs=[
            pl.BlockSpec((None, BLK, D), lambda b, d, n: (d, rb(b, d, n), 0)),
            pl.BlockSpec((None, None, HG_H, HG_DK, HG_DK), lambda b, d, n: (b, d, 0, 0, 0)),
        ],
        out_shape=[
            jax.ShapeDtypeStruct((2, n_batch * t_len, D), F32),
            jax.ShapeDtypeStruct((n_batch, 2, HG_H, HG_DK, HG_DK), F32),
        ],
        scratch_shapes=[pltpu.VMEM((HG_H, HG_DK, HG_DK), F32), pltpu.VMEM((BLK, D), F32)],
        compiler_params=_cparams(("arbitrary", "arbitrary", "arbitrary")),
        name="hgrn_recurrence",
    )(q, k, lf, v, s0)


def _hg_out_kernel(ofp_ref, obp_ref, ofs_ref, obs_ref, gg_ref, on_ref, w_ref, x_ref, m_ref, o_ref):
    o = jnp.where(pl.program_id(0) < N_TILE_P, ofp_ref[...] + obp_ref[...], ofs_ref[...] + obs_ref[...])
    gain = on_ref[...]
    parts = []
    for h in range(HG_H):
        parts.append(_rms(o[:, h * HG_DK:(h + 1) * HG_DK], gain))
    a = (jnp.concatenate(parts, axis=1) * _silu(gg_ref[...])).astype(BF16)
    o_ref[...] = x_ref[...] + m_ref[...][2:3] * _dot(a, w_ref[...])


def _hg_out(o2_p, o2_s, gg, o_norm, w_o, x, mods_l):
    fixed = lambda i: (0, 0)
    return pl.pallas_call(
        _hg_out_kernel,
        grid=(R // TM,),
        in_specs=[
            pl.BlockSpec((None, TM, D), lambda i: (0, _prompt_blk(i), 0)),
            pl.BlockSpec((None, TM, D), lambda i: (1, _prompt_blk(i), 0)),
            pl.BlockSpec((None, TM, D), lambda i: (0, _sample_blk(i), 0)),
            pl.BlockSpec((None, TM, D), lambda i: (1, _sample_blk(i), 0)),
            pl.BlockSpec((TM, D), lambda i: (i, 0)),
            pl.BlockSpec((1, HG_DK), fixed),
            pl.BlockSpec((D, D), fixed),
            pl.BlockSpec((TM, D), lambda i: (i, 0)),
            pl.BlockSpec((None, N_MOD, D), lambda i: (_cond_idx(i, TM), 0, 0)),
        ],
        out_specs=pl.BlockSpec((TM, D), lambda i: (i, 0)),
        out_shape=jax.ShapeDtypeStruct((R, D), F32),
        compiler_params=_cparams(("arbitrary",)),
        name="hgrn_out",
    )(o2_p, o2_p, o2_s, o2_s, gg, o_norm, w_o, x, mods_l)


def _router_kernel(x_ref, m_ref, g_ref, rt_ref, h8_ref, idx_ref, wt_ref, cnt_ref):
    m = m_ref[...]
    h = _normmod(x_ref[...], g_ref[...], m[4:5], m[3:4])
    tm = h.shape[0]
    for s in range(ROW_TILE):
        h8_ref[pl.ds(s, tm, stride=ROW_TILE), :] = h[:, s * LANES:(s + 1) * LANES]
    h1, h2, h3 = _split3(h)
    r1, r2, r3 = _split3(rt_ref[...])
    lt = (_dot_nt(r1, h1) + _dot_nt(r1, h2) + _dot_nt(r2, h1)
          + _dot_nt(r1, h3) + _dot_nt(r3, h1) + _dot_nt(r2, h2))
    lg = lt[:N_EXP]
    e = jnp.exp(lg - jnp.max(lg, axis=0, keepdims=True))
    p = e / jnp.sum(e, axis=0, keepdims=True)
    io = lax.broadcasted_iota(I32, p.shape, 0)
    m1 = jnp.max(p, axis=0, keepdims=True)
    i1 = jnp.min(jnp.where(p == m1, io, N_EXP), axis=0, keepdims=True)
    p2 = jnp.where(io == i1, -1.0, p)
    m2 = jnp.max(p2, axis=0, keepdims=True)
    i2 = jnp.min(jnp.where(p2 == m2, io, N_EXP), axis=0, keepdims=True)
    den = m1 + m2
    idx_ref[...] = jnp.concatenate([i1, i2], axis=0)
    wt_ref[...] = jnp.concatenate([m1 / den, m2 / den], axis=0)
    chosen = jnp.where(io == i1, 1.0, 0.0) + jnp.where(io == i2, 1.0, 0.0)

    @pl.when(pl.program_id(0) == 0)
    def _():
        cnt_ref[...] = jnp.zeros_like(cnt_ref)

    cnt_ref[...] += jnp.broadcast_to(jnp.sum(chosen, axis=1, keepdims=True), cnt_ref.shape)


def _router(x, mods_l, gain, router_t):
    return pl.pallas_call(
        _router_kernel,
        grid=(R // TM,),
        in_specs=[
            pl.BlockSpec((TM, D), lambda i: (i, 0)),
            pl.BlockSpec((None, N_MOD, D), lambda i: (_cond_idx(i, TM), 0, 0)),
            pl.BlockSpec((1, D), lambda i: (0, 0)),
            pl.BlockSpec((2 * SUB, D), lambda i: (0, 0)),
        ],
        out_specs=[
            pl.BlockSpec((TM * ROW_TILE, LANES), lambda i: (i, 0)),
            pl.BlockSpec((TOP_K, TM), lambda i: (0, i)),
            pl.BlockSpec((TOP_K, TM), lambda i: (0, i)),
            pl.BlockSpec((N_EXP, LANES), lambda i: (0, 0)),
        ],
        out_shape=[
            jax.ShapeDtypeStruct((R * ROW_TILE, LANES), F32),
            jax.ShapeDtypeStruct((TOP_K, R), I32),
            jax.ShapeDtypeStruct((TOP_K, R), F32),
            jax.ShapeDtypeStruct((N_EXP, LANES), F32),
        ],
        compiler_params=_cparams(("arbitrary",)),
        name="moe_router",
    )(x, mods_l, gain, router_t)


IDX_BLOCK = 1024
DMA_UNROLL = 8


def _slot_kernel(e_ref, gs_ref, pos_ref, tri_scr, carry_scr):
    @pl.when(pl.program_id(0) == 0)
    def _():
        r = lax.broadcasted_iota(I32, tri_scr.shape, 0)
        c = lax.broadcasted_iota(I32, tri_scr.shape, 1)
        tri_scr[...] = jnp.where(r <= c, 1.0, 0.0).astype(BF16)
        carry_scr[...] = jnp.zeros_like(carry_scr)

    e = e_ref[...]
    io = lax.broadcasted_iota(I32, (2 * SUB, IDX_BLOCK), 0)
    onehot = jnp.where(io == e, 1.0, 0.0)
    cum = _dot(onehot.astype(BF16), tri_scr[...])
    carry = carry_scr[...]
    slot = cum - 1.0 + carry[:, 0:1] + gs_ref[...][:, 0:1]
    pos_ref[...] = jnp.sum(onehot * slot, axis=0, keepdims=True).astype(I32)
    carry_scr[...] = carry + jnp.broadcast_to(cum[:, IDX_BLOCK - 1:IDX_BLOCK], carry.shape)


def _slots(idx, group_start):
    n_blk = TOP_K * R // IDX_BLOCK
    gs = jnp.broadcast_to(
        jnp.concatenate([group_start, jnp.zeros((2 * SUB - N_EXP,), I32)]).astype(F32)[:, None], (2 * SUB, LANES))
    pos = pl.pallas_call(
        _slot_kernel,
        grid=(n_blk,),
        in_specs=[
            pl.BlockSpec((None, 1, IDX_BLOCK), lambda c: (c, 0, 0)),
            pl.BlockSpec((2 * SUB, LANES), lambda c: (0, 0)),
        ],
        out_specs=pl.BlockSpec((None, 1, IDX_BLOCK), lambda c: (c, 0, 0)),
        out_shape=jax.ShapeDtypeStruct((n_blk, 1, IDX_BLOCK), I32),
        scratch_shapes=[pltpu.VMEM((IDX_BLOCK, IDX_BLOCK), BF16), pltpu.VMEM((2 * SUB, LANES), F32)],
        compiler_params=_cparams(("arbitrary",)),
        name="moe_slots",
    )(idx.reshape(n_blk, 1, IDX_BLOCK), gs)
    return pos.reshape(TOP_K * R)


N_PAD = N_SLOT - TOP_K * R


def _dispatch_kernel(p0_ref, p1_ref, pad_ref, h8_ref, hs8_ref, zero_scr, sem, zsem):
    n_tok = h8_ref.shape[0] // ROW_TILE

    @pl.when(pl.program_id(0) == 0)
    def _():
        zero_scr[...] = jnp.zeros_like(zero_scr)

        def fill(g, carry):
            for u in range(DMA_UNROLL):
                slot = pad_ref[g * DMA_UNROLL + u]
                pltpu.make_async_copy(
                    zero_scr, hs8_ref.at[pl.ds(pl.multiple_of(slot * SUB, SUB), SUB)], zsem).start(priority=u % 2)
            return carry

        lax.fori_loop(0, N_PAD // DMA_UNROLL, fill, 0)
        for _ in range(N_PAD // n_tok):
            pltpu.make_async_copy(h8_ref, hs8_ref.at[pl.ds(0, n_tok * ROW_TILE)], zsem).wait()

    for p_ref in (p0_ref, p1_ref):
        def issue(g, carry, p_ref=p_ref):
            for u in range(DMA_UNROLL):
                r = g * DMA_UNROLL + u
                pltpu.make_async_copy(
                    h8_ref.at[pl.ds(pl.multiple_of(r * SUB, SUB), SUB)],
                    hs8_ref.at[pl.ds(pl.multiple_of(p_ref[r] * SUB, SUB), SUB)],
                    sem).start(priority=u % 2)
            return carry

        lax.fori_loop(0, n_tok // DMA_UNROLL, issue, 0)
    for _ in range(TOP_K):
        pltpu.make_async_copy(h8_ref, hs8_ref.at[pl.ds(0, n_tok * ROW_TILE)], sem).wait()


def _dispatch(pos, pad_slots, h8):
    nt = R // IDX_BLOCK
    return pl.pallas_call(
        _dispatch_kernel,
        grid=(nt,),
        in_specs=[
            pl.BlockSpec((IDX_BLOCK,), lambda i: (i,), memory_space=pltpu.SMEM),
            pl.BlockSpec((IDX_BLOCK,), lambda i: (nt + i,), memory_space=pltpu.SMEM),
            pl.BlockSpec((N_PAD,), lambda i: (0,), memory_space=pltpu.SMEM),
            pl.BlockSpec((IDX_BLOCK * ROW_TILE, LANES), lambda i: (i, 0)),
        ],
        out_specs=pl.BlockSpec(memory_space=pl.ANY),
        out_shape=jax.ShapeDtypeStruct((N_SLOT * ROW_TILE, LANES), F32),
        scratch_shapes=[pltpu.VMEM((SUB, LANES), F32), pltpu.SemaphoreType.DMA(()), pltpu.SemaphoreType.DMA(())],
        compiler_params=_cparams(("arbitrary",)),
        name="moe_dispatch",
    )(pos, pos, pad_slots, h8)


EXP_CHUNK = 1792
N_EXP_CHUNK = E_FF // EXP_CHUNK


def _expert_kernel(te_ref, tv_ref, x8_ref, w1_ref, w3_ref, w2_ref, y8_ref, xb_scr, acc_scr):
    i = pl.program_id(0)
    kc = pl.program_id(1)
    valid = tv_ref[i] == 1

    @pl.when(valid & (kc == 0))
    def _():
        for s in range(ROW_TILE):
            xb_scr[:, s * LANES:(s + 1) * LANES] = x8_ref[pl.ds(s, TM_MOE, stride=ROW_TILE), :].astype(BF16)
        acc_scr[...] = jnp.zeros_like(acc_scr)

    @pl.when(valid)
    def _():
        x = xb_scr[...]
        a = _dot(x, w1_ref[...])
        b = _dot(x, w3_ref[...])
        acc_scr[...] += _dot((_silu(a) * b).astype(BF16), w2_ref[...])

    @pl.when(valid & (kc == N_EXP_CHUNK - 1))
    def _():
        for s in range(ROW_TILE):
            y8_ref[pl.ds(s, TM_MOE, stride=ROW_TILE), :] = acc_scr[:, s * LANES:(s + 1) * LANES]

    @pl.when(jnp.logical_not(valid) & (kc == N_EXP_CHUNK - 1))
    def _():
        y8_ref[...] = jnp.zeros_like(y8_ref)


def _experts(tile_expert, tile_valid, hs8, w1, w3, w2, j):
    def kc_eff(kc, tv, i):
        return jnp.where(tv[i] == 1, kc, N_EXP_CHUNK - 1)

    grid_spec = pltpu.PrefetchScalarGridSpec(
        num_scalar_prefetch=2,
        grid=(N_TILE, N_EXP_CHUNK),
        in_specs=[
            pl.BlockSpec((TM_MOE * ROW_TILE, LANES), lambda i, kc, te, tv: (i, 0)),
            pl.BlockSpec((None, None, D, EXP_CHUNK), lambda i, kc, te, tv: (j, te[i], 0, kc_eff(kc, tv, i))),
            pl.BlockSpec((None, None, D, EXP_CHUNK), lambda i, kc, te, tv: (j, te[i], 0, kc_eff(kc, tv, i))),
            pl.BlockSpec((None, None, EXP_CHUNK, D), lambda i, kc, te, tv: (j, te[i], kc_eff(kc, tv, i), 0)),
        ],
        out_specs=pl.BlockSpec((TM_MOE * ROW_TILE, LANES), lambda i, kc, te, tv: (i, 0)),
        scratch_shapes=[pltpu.VMEM((TM_MOE, D), BF16), pltpu.VMEM((TM_MOE, D), F32)],
    )
    return pl.pallas_call(
        _expert_kernel,
        grid_spec=grid_spec,
        out_shape=jax.ShapeDtypeStruct((N_SLOT * ROW_TILE, LANES), F32),
        compiler_params=_cparams(("arbitrary", "arbitrary"), 56),
        name="moe_experts",
    )(tile_expert, tile_valid, hs8, w1, w3, w2)


TILES_PER_IDX_BLOCK = IDX_BLOCK // TM


def _moe_resid_kernel(p0_ref, p1_ref, y8_ref, wc_ref, x_ref, m_ref, fn_ref, o_ref, ya_scr, yb_scr, sems, *, final):
    i = pl.program_id(0)
    nt = pl.num_programs(0)
    tm = x_ref.shape[0]
    bufs = (ya_scr, yb_scr)

    def start(tile, b):
        off = (tile % TILES_PER_IDX_BLOCK) * tm
        for c, p_ref in enumerate((p0_ref, p1_ref)):
            def issue(g, carry, p_ref=p_ref, c=c):
                for u in range(DMA_UNROLL):
                    r = g * DMA_UNROLL + u
                    pltpu.make_async_copy(
                        y8_ref.at[pl.ds(pl.multiple_of(p_ref[off + r] * SUB, SUB), SUB)],
                        bufs[b].at[c, pl.ds(pl.multiple_of(r * SUB, SUB), SUB)],
                        sems.at[b]).start(priority=u % 2)
                return carry

            lax.fori_loop(0, tm // DMA_UNROLL, issue, 0)

    def finish(b):
        for c in range(TOP_K):
            pltpu.make_async_copy(y8_ref.at[pl.ds(0, tm * ROW_TILE)], bufs[b].at[c], sems.at[b]).wait()

    def rows(ref):
        return jnp.concatenate([ref[pl.ds(s, tm, stride=ROW_TILE), :] for s in range(ROW_TILE)], axis=1)

    def combine(b):
        w = wc_ref[...]
        f = w[:, 0:1] * rows(bufs[b].at[0]) + w[:, 1:2] * rows(bufs[b].at[1])
        xn = x_ref[...] + m_ref[...][5:6] * f
        if final:
            xn = _rms(xn, fn_ref[...])
        o_ref[...] = xn

    @pl.when(i == 0)
    def _():
        start(0, 0)

    for b in range(2):
        @pl.when(i % 2 == b)
        def _(b=b):
            @pl.when(i + 1 < nt)
            def _():
                start(i + 1, 1 - b)

            finish(b)
            combine(b)


def _moe_resid(pos, y8, wcol, x, mods_l, final_gain, final):
    nt = R // TM
    nblk = R // IDX_BLOCK

    def next_blk(i):
        return jnp.minimum(i + 1, nt - 1) // TILES_PER_IDX_BLOCK

    return pl.pallas_call(
        functools.partial(_moe_resid_kernel, final=final),
        grid=(nt,),
        in_specs=[
            pl.BlockSpec((IDX_BLOCK,), lambda i: (next_blk(i),), memory_space=pltpu.SMEM),
            pl.BlockSpec((IDX_BLOCK,), lambda i: (nblk + next_blk(i),), memory_space=pltpu.SMEM),
            pl.BlockSpec(memory_space=pl.ANY),
            pl.BlockSpec((TM, TOP_K), lambda i: (i, 0)),
            pl.BlockSpec((TM, D), lambda i: (i, 0)),
            pl.BlockSpec((None, N_MOD, D), lambda i: (_cond_idx(i, TM), 0, 0)),
            pl.BlockSpec((1, D), lambda i: (0, 0)),
        ],
        out_specs=pl.BlockSpec((TM, D), lambda i: (i, 0)),
        out_shape=jax.ShapeDtypeStruct((R, D), F32),
        scratch_shapes=[pltpu.VMEM((TOP_K, TM * ROW_TILE, LANES), F32), pltpu.VMEM((TOP_K, TM * ROW_TILE, LANES), F32),
                        pltpu.SemaphoreType.DMA((2,))],
        compiler_params=_cparams(("arbitrary",), 48),
        name="moe_resid",
    )(pos, pos, y8, wcol, x, mods_l, final_gain)


def _tile_tables(counts):
    padded = ((counts + TM_MOE - 1) // TM_MOE) * TM_MOE
    gend = jnp.cumsum(padded)
    tile_start = jnp.arange(N_TILE, dtype=I32) * TM_MOE
    te = jnp.sum((tile_start[:, None] >= gend[None, :]).astype(I32), axis=1)
    valid = tile_start < gend[-1]
    te_last = te[gend[-1] // TM_MOE - 1]
    te = jnp.minimum(jnp.where(valid, te, te_last), N_EXP - 1)
    gstart = gend - padded
    j = jnp.arange(TM_MOE, dtype=I32)[None, :]
    in_group = (j < (padded - counts)[:, None]).reshape(N_PAD)
    group_pad = ((gstart + counts)[:, None] + j).reshape(N_PAD)
    trailing = gend[-1] + jnp.cumsum(jnp.logical_not(in_group).astype(I32)) - 1
    pad_slots = jnp.where(in_group, group_pad, trailing)
    return gstart.astype(I32), te.astype(I32), valid.astype(I32), pad_slots.astype(I32)


def _rot_half(w):
    wa = w.reshape(w.shape[:-1] + (2, 2, ROPE_AXIS // 2))
    return jnp.stack([-wa[..., 1, :], wa[..., 0, :]], axis=-2).reshape(w.shape)


def _rope_tables(tm):
    rows = DEC_SEQ // GRID_W
    r = jnp.repeat(jnp.arange(rows), GRID_W).astype(F32)
    c = jnp.tile(jnp.arange(GRID_W), rows).astype(F32)
    inv = ROPE_THETA ** (-jnp.arange(0, ROPE_AXIS, 2, dtype=F32) / ROPE_AXIS)
    ang_r = r[:, None] * inv
    ang_c = c[:, None] * inv
    ang = jnp.concatenate([ang_r, ang_r, ang_c, ang_c], axis=-1)
    cos, sin = jnp.cos(ang), jnp.sin(ang)
    z32 = jnp.zeros_like(cos)
    one = jnp.ones((tm, QK_ROPE), F32)
    zt = jnp.zeros((tm, QK_ROPE), F32)
    tq_c = jnp.concatenate([jnp.concatenate([cos, cos, z32, z32], 1), jnp.concatenate([one, one, zt, zt], 1)], 0)
    tq_s = jnp.concatenate([jnp.concatenate([sin, sin, z32, z32], 1), jnp.zeros((tm, LANES), F32)], 0)
    tk_c = jnp.concatenate([jnp.concatenate([cos, z32, z32, z32], 1), jnp.concatenate([one, zt, zt, zt], 1)], 0)
    tk_s = jnp.concatenate([jnp.concatenate([sin, z32, z32, z32], 1), jnp.zeros((tm, LANES), F32)], 0)
    return tq_c, tq_s, tk_c, tk_s


def _mla_weights(w_dq, w_uq, w_dkv, w_uk, w_uv, w_o):
    kr_w = w_dkv[:, KV_LORA:]
    w_a = jnp.concatenate(
        [w_dq, w_dkv[:, :KV_LORA], kr_w, _rot_half(kr_w), jnp.zeros((D, LANES - 2 * QK_ROPE), F32)], axis=1)
    uq = w_uq.reshape(Q_LORA, N_PAIR, 2, QK_NOPE + QK_ROPE)
    nope = uq[..., :QK_NOPE].reshape(Q_LORA, N_PAIR, 2 * QK_NOPE)
    rope = uq[..., QK_NOPE:]
    w_uq2 = jnp.concatenate(
        [nope, rope.reshape(Q_LORA, N_PAIR, 2 * QK_ROPE), _rot_half(rope).reshape(Q_LORA, N_PAIR, 2 * QK_ROPE)],
        axis=-1).reshape(Q_LORA, N_PAIR * 2 * LANES)
    uk = w_uk.reshape(KV_LORA, N_PAIR, 2 * QK_NOPE)
    wk = jnp.concatenate([uk, jnp.zeros((KV_LORA, N_PAIR, LANES), F32)], axis=-1).reshape(KV_LORA, N_PAIR * 2 * LANES)
    eye = jnp.eye(LANES, QK_ROPE, dtype=F32)
    pair = jnp.concatenate([jnp.zeros((LANES, LANES), F32), eye, eye, jnp.zeros((LANES, LANES - 2 * QK_ROPE), F32)], 1)
    wkr = jnp.tile(pair, (1, N_PAIR))
    wv = w_uv.reshape(KV_LORA, HEADS * V_HEAD)
    wo = w_o.reshape(HEADS * V_HEAD, D)
    return [w.astype(BF16) for w in (w_a, w_uq2, wk, wkr, wv, wo)]


def kernel(x_prompt, x_sample, cache_ckv, cache_krope, state_hgrn, c, c_ctx, w_ada, b_ada, norm_mix, norm_ffn, mla_w_dq, mla_q_norm, mla_w_uq, mla_w_dkv, mla_kv_norm, mla_w_uk, mla_w_uv, mla_w_o, hg_w_q, hg_w_f, hg_w_i, hg_w_g, hg_lb_logits, hg_o_norm, hg_w_o, ffn_w1, ffn_w3, ffn_w2, moe_router, moe_w1, moe_w3, moe_w2, final_norm):
    x = (x_prompt.reshape(R_P, D), x_sample.reshape(R_S, D))
    cond = jnp.concatenate([c, c_ctx[None], jnp.zeros((COND_PAD - N_COND, D), F32)], axis=0)
    mods = _ada(cond, w_ada, b_ada)
    lb_all = _lower_bounds(hg_lb_logits)
    tq_c, tq_s, tk_c, tk_s = _rope_tables(TM)
    ffn_w = [w.astype(BF16) for w in (ffn_w1, ffn_w3, ffn_w2)]
    moe_w = [w.astype(BF16) for w in (moe_w1, moe_w3, moe_w2)]
    new_ckv, new_krope, new_hgrn = [], [], []

    for l in range(DEPTH):
        j = l // 2
        mods_l = mods[l]
        gain_mix = norm_mix[l][None]
        gain_ffn = norm_ffn[l][None]
        if l % 2 == 0:
            w_a, w_uq2, wk, wkr, wv, wo = _mla_weights(
                mla_w_dq[j], mla_w_uq[j], mla_w_dkv[j], mla_w_uk[j], mla_w_uv[j], mla_w_o[j])
            cq, ckv, kr = _mla_proj(x, mods_l, gain_mix, w_a, mla_q_norm[j][None], mla_kv_norm[j][None], tk_c, tk_s)
            q2 = _q_up(cq, w_uq2, tq_c, tq_s)
            k2_p, v_p = _kv_up(ckv, kr, R_P, wk, wkr, wv)
            ckv_s = jnp.concatenate([cache_ckv[:, j], ckv[R_P:].reshape(DEC_BATCH, DEC_SEQ, KV_LORA)], axis=1)
            kr_cache = jnp.pad(cache_krope[:, j], ((0, 0), (0, 0), (0, LANES - QK_ROPE)))
            kr_s = jnp.concatenate([kr_cache, kr[R_P:].reshape(DEC_BATCH, DEC_SEQ, LANES)], axis=1)
            s_all = PAST + DEC_SEQ
            k2_s, v_s = _kv_up(ckv_s.reshape(DEC_BATCH * s_all, KV_LORA), kr_s.reshape(DEC_BATCH * s_all, LANES),
                               DEC_BATCH * s_all, wk, wkr, wv)
            o_p = _attention(q2, k2_p, v_p, BATCH, SEQ, SEQ, 0, SEQ, 4)
            o_s = _attention(q2, k2_s, v_s, DEC_BATCH, DEC_SEQ, s_all, R_P, 512, 1)
            x = _mm_resid((o_p, o_s), wo, x, mods_l, 2)
            new_ckv.append(ckv[:R_P].reshape(BATCH, SEQ, KV_LORA))
            new_krope.append(kr[:R_P, :QK_ROPE].reshape(BATCH, SEQ, QK_ROPE))
        else:
            w5 = jnp.stack([hg_w_q[j], hg_w_f[j, 0], hg_w_f[j, 1], hg_w_i[j], hg_w_g[j]]).astype(BF16)
            q, k, lf, v, gg = _hg_proj(x, mods_l, gain_mix, w5, lb_all[j])
            s_zero = jnp.zeros((BATCH, 2, HG_H, HG_DK, HG_DK), F32)
            o2_p, st_p = _gla(q, k, lf, v, s_zero, BATCH, SEQ, 0)
            o2_s, _ = _gla(q, k, lf, v, state_hgrn[:, j], DEC_BATCH, DEC_SEQ, R_P)
            x = _hg_out(o2_p, o2_s, gg, hg_o_norm[j][None], hg_w_o[j].astype(BF16), x, mods_l)
            new_hgrn.append(st_p)
        if l % 2 == 0:
            x = _ffn(x, mods_l, gain_ffn, *ffn_w, j)
        else:
            router_t = jnp.concatenate([moe_router[j].T, jnp.zeros((2 * SUB - N_EXP, D), F32)], axis=0)
            h8, idx, wts, cnt = _router(x, mods_l, gain_ffn, router_t)
            group_start, tile_expert, tile_valid, pad_slots = _tile_tables(cnt[:, 0].astype(I32))
            pos = _slots(idx, group_start)
            y8 = _experts(tile_expert, tile_valid, _dispatch(pos, pad_slots, h8), *moe_w, j)
            x = _moe_resid(pos, y8, wts.T, x, mods_l, final_norm[None], final=(l == DEPTH - 1))

    y_prompt = x[:R_P].reshape(BATCH, SEQ, D)
    y_sample = x[R_P:].reshape(DEC_BATCH, DEC_SEQ, D)
    return (y_prompt, y_sample, jnp.stack(new_ckv, axis=1), jnp.stack(new_krope, axis=1),
            jnp.stack(new_hgrn, axis=1))
```

```python
import functools
import math

import jax
import jax.numpy as jnp
from jax import lax
from jax.experimental import pallas as pl
from jax.experimental.pallas import tpu as pltpu

F32 = jnp.float32
BF16 = jnp.bfloat16
I32 = jnp.int32

D = 1024
BATCH, SEQ = 32, 256
DEC_BATCH, DEC_SEQ = 8, 1024
PAST = 256
DEPTH = 4
R_P = BATCH * SEQ
R_S = DEC_BATCH * DEC_SEQ
R = R_P + R_S
N_COND = DEC_BATCH + 1
COND_PAD = 16
N_MOD = 6
EPS = 1e-6
GRID_W = 64
HEADS, QK_NOPE, QK_ROPE, V_HEAD = 16, 64, 32, 64
Q_LORA, KV_LORA = 512, 256
N_PAIR = HEADS // 2
ROPE_AXIS = QK_ROPE // 2
ROPE_THETA = 10000.0
HG_H, HG_DK = 8, 128
BLK = 128
CHUNK = 16
D_FF = 2816
N_EXP, TOP_K, E_FF = 8, 2, 3584
LANES = 128
SUB = 8
ROW_TILE = D // LANES
TM = 512
TM_MOE = 512
N_SLOT = TOP_K * R + N_EXP * TM_MOE
N_TILE = N_SLOT // TM_MOE
MIB = 1024 * 1024
assert DEPTH % 2 == 0


def _cparams(sem, vmem_mib=40):
    return pltpu.CompilerParams(dimension_semantics=sem, vmem_limit_bytes=vmem_mib * MIB)


def _cond_idx(i, tm):
    r0 = i * tm
    return jnp.where(r0 < R_P, DEC_BATCH, (r0 - R_P) // DEC_SEQ)


def _rope_idx(i, tm):
    nb = DEC_SEQ // tm
    r0 = i * tm
    return jnp.where(r0 < R_P, nb, ((r0 - R_P) // tm) % nb)


def _normmod(x, gain, scale, shift):
    ms = jnp.mean(x * x, axis=-1, keepdims=True)
    return (x * lax.rsqrt(ms + EPS)) * gain * (1.0 + scale) + shift


def _rms(x, gain):
    ms = jnp.mean(x * x, axis=-1, keepdims=True)
    return (x * lax.rsqrt(ms + EPS)) * gain


def _silu(x):
    return x / (1.0 + jnp.exp(-x))


def _dot(a, b):
    return jnp.dot(a, b, preferred_element_type=F32)


def _dot_nt(a, b):
    return lax.dot_general(a, b, (((1,), (1,)), ((), ())), preferred_element_type=F32)


def _split3(x):
    hi = x.astype(BF16)
    r1 = x - hi.astype(F32)
    mid = r1.astype(BF16)
    lo = (r1 - mid.astype(F32)).astype(BF16)
    return hi, mid, lo


def _ada_kernel(c_ref, w_ref, b_ref, o_ref):
    o_ref[...] = _dot(_silu(c_ref[...]), w_ref[...]) + b_ref[...]


def _ada(cond, w_ada, b_ada):
    tn = 1536
    out = pl.pallas_call(
        _ada_kernel,
        grid=(DEPTH, N_MOD * D // tn),
        in_specs=[
            pl.BlockSpec((COND_PAD, D), lambda l, j: (0, 0)),
            pl.BlockSpec((None, D, tn), lambda l, j: (l, 0, j)),
            pl.BlockSpec((None, 1, tn), lambda l, j: (l, 0, j)),
        ],
        out_specs=pl.BlockSpec((None, COND_PAD, tn), lambda l, j: (l, 0, j)),
        out_shape=jax.ShapeDtypeStruct((DEPTH, COND_PAD, N_MOD * D), F32),
        compiler_params=_cparams(("arbitrary", "arbitrary")),
        name="ada",
    )(cond, w_ada, b_ada.reshape(DEPTH, 1, N_MOD * D))
    return out[:, :N_COND].reshape(DEPTH, N_COND, N_MOD, D)


def _lb_kernel(x_ref, o_ref):
    x = x_ref[...]
    e = jnp.exp(x - jnp.max(x, axis=0, keepdims=True))
    sm = e / jnp.sum(e, axis=0, keepdims=True)
    n = x.shape[0]
    cum = sm[0]
    o_ref[0] = jnp.zeros_like(cum)
    for i in range(1, n):
        cum_i = cum + sm[i]
        o_ref[i] = cum_i - sm[0]
        cum = cum_i


def _lower_bounds(lb_logits):
    return pl.pallas_call(
        _lb_kernel,
        out_shape=jax.ShapeDtypeStruct(lb_logits.shape, F32),
        name="hgrn_lower_bounds",
    )(lb_logits)


def _two_part_rows(x):
    if isinstance(x, tuple):
        x_p, x_s = x
        s_off = 0
    else:
        x_p = x_s = x
        s_off = N_TILE_P
    w = x_p.shape[1]
    specs = [pl.BlockSpec((TM, w), lambda i: (_prompt_blk(i), 0)),
             pl.BlockSpec((TM, w), lambda i: (s_off + _sample_blk(i), 0))]
    return [x_p, x_s], specs


def _pick_rows(p_ref, s_ref):
    return jnp.where(pl.program_id(0) < N_TILE_P, p_ref[...], s_ref[...])


def _mla_proj_kernel(xp_ref, xs_ref, m_ref, g_ref, w_ref, qn_ref, kvn_ref, tc_ref, ts_ref,
                     cq_ref, ckv_ref, kr_ref):
    m = m_ref[...]
    h = _normmod(_pick_rows(xp_ref, xs_ref), g_ref[...], m[1:2], m[0:1]).astype(BF16)
    y = _dot(h, w_ref[...])
    cq_ref[...] = _rms(y[:, :Q_LORA], qn_ref[...]).astype(BF16)
    ckv_ref[...] = _rms(y[:, Q_LORA:Q_LORA + KV_LORA], kvn_ref[...])
    slab = y[:, Q_LORA + KV_LORA:]
    kr_ref[...] = slab * tc_ref[...] + pltpu.roll(slab, LANES - QK_ROPE, 1) * ts_ref[...]


def _mla_proj(x, mods_l, gain, w_a, q_norm, kv_norm, tk_c, tk_s):
    n_a = w_a.shape[1]
    row = lambda i: (i, 0)
    fixed = lambda i: (0, 0)
    x_args, x_specs = _two_part_rows(x)
    return pl.pallas_call(
        _mla_proj_kernel,
        grid=(R // TM,),
        in_specs=x_specs + [
            pl.BlockSpec((None, N_MOD, D), lambda i: (_cond_idx(i, TM), 0, 0)),
            pl.BlockSpec((1, D), fixed),
            pl.BlockSpec((D, n_a), fixed),
            pl.BlockSpec((1, Q_LORA), fixed),
            pl.BlockSpec((1, KV_LORA), fixed),
            pl.BlockSpec((TM, LANES), lambda i: (_rope_idx(i, TM), 0)),
            pl.BlockSpec((TM, LANES), lambda i: (_rope_idx(i, TM), 0)),
        ],
        out_specs=[
            pl.BlockSpec((TM, Q_LORA), row),
            pl.BlockSpec((TM, KV_LORA), row),
            pl.BlockSpec((TM, LANES), row),
        ],
        out_shape=[
            jax.ShapeDtypeStruct((R, Q_LORA), BF16),
            jax.ShapeDtypeStruct((R, KV_LORA), F32),
            jax.ShapeDtypeStruct((R, LANES), F32),
        ],
        compiler_params=_cparams(("arbitrary",)),
        name="mla_proj",
    )(*x_args, mods_l, gain, w_a, q_norm, kv_norm, tk_c, tk_s)


QK_SCALE_LOG2 = math.log2(math.e) / math.sqrt(QK_NOPE + QK_ROPE)


def _q_up_kernel(cq_ref, w_ref, tc_ref, ts_ref, q_ref):
    y = _dot(cq_ref[...], w_ref[...]) * QK_SCALE_LOG2
    tc = tc_ref[...]
    ts = ts_ref[...]
    for p in range(N_PAIR):
        lo = p * 2 * LANES
        q_ref[:, lo:lo + LANES] = y[:, lo:lo + LANES].astype(BF16)
        hi = y[:, lo + LANES:lo + 2 * LANES]
        q_ref[:, lo + LANES:lo + 2 * LANES] = (hi * tc + pltpu.roll(hi, LANES // 2, 1) * ts).astype(BF16)


def _q_up(cq, w_uq2, tq_c, tq_s):
    nq = w_uq2.shape[1]
    return pl.pallas_call(
        _q_up_kernel,
        grid=(R // TM,),
        in_specs=[
            pl.BlockSpec((TM, Q_LORA), lambda i: (i, 0)),
            pl.BlockSpec((Q_LORA, nq), lambda i: (0, 0)),
            pl.BlockSpec((TM, LANES), lambda i: (_rope_idx(i, TM), 0)),
            pl.BlockSpec((TM, LANES), lambda i: (_rope_idx(i, TM), 0)),
        ],
        out_specs=pl.BlockSpec((TM, nq), lambda i: (i, 0)),
        out_shape=jax.ShapeDtypeStruct((R, nq), BF16),
        compiler_params=_cparams(("arbitrary",)),
        name="mla_q_up",
    )(cq, w_uq2, tq_c, tq_s)


def _kv_up_kernel(c_ref, r_ref, wk_ref, wkr_ref, wv_ref, k_ref, v_ref):
    c = c_ref[...].astype(BF16)
    r = r_ref[...].astype(BF16)
    k_ref[...] = (_dot(c, wk_ref[...]) + _dot(r, wkr_ref[...])).astype(BF16)
    v_ref[...] = _dot(c, wv_ref[...]).astype(BF16)


def _kv_up(ckv, kr, n_rows, wk, wkr, wv):
    nk = wk.shape[1]
    nv = wv.shape[1]
    fixed = lambda i: (0, 0)
    return pl.pallas_call(
        _kv_up_kernel,
        grid=(n_rows // TM,),
        in_specs=[
            pl.BlockSpec((TM, KV_LORA), lambda i: (i, 0)),
            pl.BlockSpec((TM, LANES), lambda i: (i, 0)),
            pl.BlockSpec((KV_LORA, nk), fixed),
            pl.BlockSpec((LANES, nk), fixed),
            pl.BlockSpec((KV_LORA, nv), fixed),
        ],
        out_specs=[
            pl.BlockSpec((TM, nk), lambda i: (i, 0)),
            pl.BlockSpec((TM, nv), lambda i: (i, 0)),
        ],
        out_shape=[
            jax.ShapeDtypeStruct((n_rows, nk), BF16),
            jax.ShapeDtypeStruct((n_rows, nv), BF16),
        ],
        compiler_params=_cparams(("arbitrary",)),
        name="mla_kv_up",
    )(ckv, kr, wk, wkr, wv)


def _attn_kernel(q_ref, k_ref, v_ref, *rest, n_pair):
    o_ref = rest[-1]
    lq = lax.broadcasted_iota(I32, (1, 2 * LANES), 1)
    lv = lax.broadcasted_iota(I32, (1, LANES), 1)
    sel_a = (lq < QK_NOPE) | ((lq >= LANES) & (lq < LANES + QK_ROPE))
    sel_b = ((lq >= QK_NOPE) & (lq < LANES)) | ((lq >= LANES + QK_ROPE) & (lq < LANES + 2 * QK_ROPE))
    for p in range(n_pair):
        q = q_ref[:, p * 2 * LANES:(p + 1) * 2 * LANES]
        k = k_ref[:, p * 2 * LANES:(p + 1) * 2 * LANES]
        v = v_ref[:, p * LANES:(p + 1) * LANES]
        zq = jnp.zeros_like(q)
        zv = jnp.zeros_like(v)
        out = None
        for sel, vsel in ((sel_a, lv < V_HEAD), (sel_b, lv >= V_HEAD)):
            s = _dot_nt(jnp.where(sel, q, zq), k)
            e = jnp.exp2(s - jnp.max(s, axis=-1, keepdims=True))
            den = jnp.sum(e, axis=-1, keepdims=True)
            o = _dot(e.astype(BF16), jnp.where(vsel, v, zv)) / den
            out = o if out is None else out + o
        o_ref[:, p * LANES:(p + 1) * LANES] = out.astype(BF16)


def _attention(q2, k2, v, n_batch, t_len, s_len, q_row0, tq, n_pair):
    nq = t_len // tq
    qb0 = q_row0 // tq
    return pl.pallas_call(
        functools.partial(_attn_kernel, n_pair=n_pair),
        grid=(n_batch, N_PAIR // n_pair, nq),
        in_specs=[
            pl.BlockSpec((tq, n_pair * 2 * LANES), lambda b, p, i: (qb0 + b * nq + i, p)),
            pl.BlockSpec((s_len, n_pair * 2 * LANES), lambda b, p, i: (b, p)),
            pl.BlockSpec((s_len, n_pair * LANES), lambda b, p, i: (b, p)),
        ],
        out_specs=pl.BlockSpec((tq, n_pair * LANES), lambda b, p, i: (b * nq + i, p)),
        out_shape=jax.ShapeDtypeStruct((n_batch * t_len, HEADS * V_HEAD), BF16),
        compiler_params=_cparams(("arbitrary", "arbitrary", "arbitrary"), 56),
        name="mla_attention",
    )(q2, k2, v)


N_TILE_P = R_P // TM


def _prompt_blk(i):
    return jnp.minimum(i, N_TILE_P - 1)


def _sample_blk(i):
    return jnp.maximum(i - N_TILE_P, 0)


def _mm_resid_kernel(ap_ref, as_ref, w_ref, xp_ref, xs_ref, m_ref, o_ref, *, gate_idx):
    gate = m_ref[...][gate_idx:gate_idx + 1]
    o_ref[...] = _pick_rows(xp_ref, xs_ref) + gate * _dot(_pick_rows(ap_ref, as_ref), w_ref[...])


def _mm_resid(a, w, x, mods_l, gate_idx):
    a_args, a_specs = _two_part_rows(a)
    x_args, x_specs = _two_part_rows(x)
    k = a_args[0].shape[1]
    return pl.pallas_call(
        functools.partial(_mm_resid_kernel, gate_idx=gate_idx),
        grid=(R // TM,),
        in_specs=a_specs + [pl.BlockSpec((k, D), lambda i: (0, 0))] + x_specs + [
            pl.BlockSpec((None, N_MOD, D), lambda i: (_cond_idx(i, TM), 0, 0)),
        ],
        out_specs=pl.BlockSpec((TM, D), lambda i: (i, 0)),
        out_shape=jax.ShapeDtypeStruct((R, D), F32),
        compiler_params=_cparams(("arbitrary",)),
        name="mm_resid",
    )(*a_args, w, *x_args, mods_l)


FFN_CHUNK = 1408


def _ffn_kernel(x_ref, m_ref, g_ref, w1_ref, w3_ref, w2_ref, o_ref):
    x = x_ref[...]
    m = m_ref[...]
    h = _normmod(x, g_ref[...], m[4:5], m[3:4]).astype(BF16)
    acc = jnp.zeros(x.shape, F32)
    for c in range(D_FF // FFN_CHUNK):
        sl = slice(c * FFN_CHUNK, (c + 1) * FFN_CHUNK)
        a = _dot(h, w1_ref[:, sl])
        b = _dot(h, w3_ref[:, sl])
        acc = acc + _dot((_silu(a) * b).astype(BF16), w2_ref[sl, :])
    o_ref[...] = x + m[5:6] * acc


def _ffn(x, mods_l, gain, w1, w3, w2, j):
    tm = TM
    fixed = lambda i: (0, 0)
    layer = lambda i: (j, 0, 0)
    once = pl.Buffered(1)
    return pl.pallas_call(
        _ffn_kernel,
        grid=(R // tm,),
        in_specs=[
            pl.BlockSpec((tm, D), lambda i: (i, 0)),
            pl.BlockSpec((None, N_MOD, D), lambda i: (_cond_idx(i, tm), 0, 0)),
            pl.BlockSpec((1, D), fixed),
            pl.BlockSpec((None, D, D_FF), layer, pipeline_mode=once),
            pl.BlockSpec((None, D, D_FF), layer, pipeline_mode=once),
            pl.BlockSpec((None, D_FF, D), layer, pipeline_mode=once),
        ],
        out_specs=pl.BlockSpec((tm, D), lambda i: (i, 0)),
        out_shape=jax.ShapeDtypeStruct((R, D), F32),
        compiler_params=_cparams(("arbitrary",), 56),
        name="dense_swiglu",
    )(x, mods_l, gain, w1, w3, w2)


def _forget_gate(z, lb):
    e = jnp.exp(-jnp.abs(z))
    log_sig = jnp.minimum(z, 0.0) - jnp.log(1.0 + e)
    a = jnp.log(lb)
    b = jnp.log1p(-lb) + log_sig
    log_f = jnp.maximum(a, b) + jnp.log(1.0 + jnp.exp(-jnp.abs(a - b)))
    k = (1.0 - lb) * (jnp.where(z >= 0, e, 1.0) / (1.0 + e))
    return k, log_f


def _hg_proj_kernel(x_ref, m_ref, g_ref, w_ref, lb_ref, q_ref, k_ref, lf_ref, v_ref, gg_ref):
    m = m_ref[...]
    h = _normmod(x_ref[...], g_ref[...], m[1:2], m[0:1]).astype(BF16)
    q_ref[...] = _dot(h, w_ref[0])
    for dr in range(2):
        k, log_f = _forget_gate(_dot(h, w_ref[1 + dr]), lb_ref[dr])
        k_ref[dr] = k
        lf_ref[dr] = log_f
    v_ref[...] = _dot(h, w_ref[3]).astype(BF16)
    gg_ref[...] = _dot(h, w_ref[4])


def _hg_proj(x, mods_l, gain, w5, lb):
    tm = TM
    row = lambda i: (i, 0)
    dirs = lambda i: (0, i, 0)
    return pl.pallas_call(
        _hg_proj_kernel,
        grid=(R // tm,),
        in_specs=[
            pl.BlockSpec((tm, D), row),
            pl.BlockSpec((None, N_MOD, D), lambda i: (_cond_idx(i, tm), 0, 0)),
            pl.BlockSpec((1, D), lambda i: (0, 0)),
            pl.BlockSpec((5, D, D), lambda i: (0, 0, 0), pipeline_mode=pl.Buffered(1)),
            pl.BlockSpec((2, 1, D), lambda i: (0, 0, 0)),
        ],
        out_specs=[
            pl.BlockSpec((tm, D), row),
            pl.BlockSpec((2, tm, D), dirs),
            pl.BlockSpec((2, tm, D), dirs),
            pl.BlockSpec((tm, D), row),
            pl.BlockSpec((tm, D), row),
        ],
        out_shape=[
            jax.ShapeDtypeStruct((R, D), F32),
            jax.ShapeDtypeStruct((2, R, D), F32),
            jax.ShapeDtypeStruct((2, R, D), F32),
            jax.ShapeDtypeStruct((R, D), BF16),
            jax.ShapeDtypeStruct((R, D), F32),
        ],
        compiler_params=_cparams(("arbitrary",), 48),
        name="hgrn_proj",
    )(x, mods_l, gain, w5, lb.reshape(2, 1, D))


LEVEL_HALVES = (64, 32, 16)


def _gla_block(d, q_ref, k_ref, lf_ref, v_ref, o_ref, st_scr, b_scr):
    row = lax.broadcasted_iota(I32, (BLK, BLK), 0)
    col = lax.broadcasted_iota(I32, (BLK, BLK), 1)
    ut, us = (row, col) if d == 0 else (BLK - 1 - row, BLK - 1 - col)
    causal = us <= ut
    tri = jnp.where(causal, 1.0, 0.0).astype(BF16)

    hi, mid, lo = _split3(lf_ref[...])
    b_scr[d] = _dot(tri, hi) + _dot(tri, mid) + _dot(tri, lo)

    level_masks = []
    for hs in LEVEL_HALVES:
        sh = int(math.log2(2 * hs))
        same = (ut >> sh) == (us >> sh)
        level_masks.append(same & ((ut & (2 * hs - 1)) >= hs) & ((us & (2 * hs - 1)) < hs))
    base_mask = ((ut >> 4) == (us >> 4)) & causal

    def split_rows(sl, half):
        parts = []
        for jr in range(BLK // (2 * half)):
            r0 = jr * 2 * half + half - 1 + d
            parts.append(jnp.broadcast_to(b_scr[d, r0:r0 + 1, sl], (2 * half, HG_DK)))
        return parts[0] if len(parts) == 1 else jnp.concatenate(parts, axis=0)

    last = BLK - 1 if d == 0 else 0
    for h in range(HG_H):
        sl = slice(h * HG_DK, (h + 1) * HG_DK)
        bh = b_scr[d, :, sl]
        q = q_ref[:, sl]
        k = k_ref[:, sl]
        v = v_ref[:, sl]

        xq = bh - split_rows(sl, CHUNK // 2)
        att = jnp.where(base_mask, _dot_nt((q * jnp.exp(xq)).astype(BF16), (k * jnp.exp(-xq)).astype(BF16)), 0.0)
        for hs, msk in zip(LEVEL_HALVES, level_masks):
            e = jnp.exp(-jnp.abs(bh - split_rows(sl, hs)))
            att = jnp.where(msk, _dot_nt((q * e).astype(BF16), (k * e).astype(BF16)), att)

        b_last = b_scr[d, last:last + 1, sl]
        q_in = (q * jnp.exp(bh)).astype(BF16)
        k_in = (k * jnp.exp(b_last - bh)).astype(BF16)
        st = st_scr[d, h]
        o_ref[:, sl] = _dot(att.astype(BF16), v) + _dot_nt(q_in, st.astype(BF16))
        v_t = v.astype(F32).T.astype(BF16)
        st_scr[d, h] = st * jnp.exp(b_last) + _dot(v_t, k_in)


def _gla_kernel(qf_ref, kf_ref, lff_ref, vf_ref, qb_ref, kb_ref, lfb_ref, vb_ref, s0_ref,
                of_ref, ob_ref, sn_ref, st_scr, b_scr, *, nb):
    n = pl.program_id(1)

    @pl.when(n == 0)
    def _():
        for d in range(2):
            for h in range(HG_H):
                st_scr[d, h] = s0_ref[d, h].T

    _gla_block(0, qf_ref, kf_ref, lff_ref, vf_ref, of_ref, st_scr, b_scr)
    _gla_block(1, qb_ref, kb_ref, lfb_ref, vb_ref, ob_ref, st_scr, b_scr)

    @pl.when(n == nb - 1)
    def _():
        for d in range(2):
            for h in range(HG_H):
                sn_ref[d, h] = st_scr[d, h].T


def _gla(q, k, lf, v, s0, n_batch, t_len, row0):
    nb = t_len // BLK
    rb0 = row0 // BLK
    fwd = lambda b, n: b * nb + n
    bwd = lambda b, n: b * nb + nb - 1 - n

    def in_specs(blk, d):
        return [
            pl.BlockSpec((BLK, D), lambda b, n: (rb0 + blk(b, n), 0)),
            pl.BlockSpec((None, BLK, D), lambda b, n: (d, rb0 + blk(b, n), 0)),
            pl.BlockSpec((None, BLK, D), lambda b, n: (d, rb0 + blk(b, n), 0)),
            pl.BlockSpec((BLK, D), lambda b, n: (rb0 + blk(b, n), 0)),
        ]

    state_spec = pl.BlockSpec((None, 2, HG_H, HG_DK, HG_DK), lambda b, n: (b, 0, 0, 0, 0))
    return pl.pallas_call(
        functools.partial(_gla_kernel, nb=nb),
        grid=(n_batch, nb),
        in_specs=in_specs(fwd, 0) + in_specs(bwd, 1) + [state_spec],
        out_specs=[
            pl.BlockSpec((BLK, D), lambda b, n: (fwd(b, n), 0)),
            pl.BlockSpec((BLK, D), lambda b, n: (bwd(b, n), 0)),
            state_spec,
        ],
        out_shape=[
            jax.ShapeDtypeStruct((n_batch * t_len, D), F32),
            jax.ShapeDtypeStruct((n_batch * t_len, D), F32),
            jax.ShapeDtypeStruct((n_batch, 2, HG_H, HG_DK, HG_DK), F32),
        ],
        scratch_shapes=[pltpu.VMEM((2, HG_H, HG_DK, HG_DK), F32), pltpu.VMEM((2, BLK, D), F32)],
        compiler_params=_cparams(("arbitrary", "arbitrary")),
        name="hgrn_recurrence",
    )(q, k, lf, v, q, k, lf, v, s0)


def _hg_out_kernel(ofp_ref, obp_ref, ofs_ref, obs_ref, gg_ref, on_ref, w_ref, x_ref, m_ref, o_ref):
    o = jnp.where(pl.program_id(0) < N_TILE_P, ofp_ref[...] + obp_ref[...], ofs_ref[...] + obs_ref[...])
    gain = on_ref[...]
    parts = []
    for h in range(HG_H):
        parts.append(_rms(o[:, h * HG_DK:(h + 1) * HG_DK], gain))
    a = (jnp.concatenate(parts, axis=1) * _silu(gg_ref[...])).astype(BF16)
    o_ref[...] = x_ref[...] + m_ref[...][2:3] * _dot(a, w_ref[...])


def _hg_out(of_p, ob_p, of_s, ob_s, gg, o_norm, w_o, x, mods_l):
    fixed = lambda i: (0, 0)
    return pl.pallas_call(
        _hg_out_kernel,
        grid=(R // TM,),
        in_specs=[
            pl.BlockSpec((TM, D), lambda i: (_prompt_blk(i), 0)),
            pl.BlockSpec((TM, D), lambda i: (_prompt_blk(i), 0)),
            pl.BlockSpec((TM, D), lambda i: (_sample_blk(i), 0)),
            pl.BlockSpec((TM, D), lambda i: (_sample_blk(i), 0)),
            pl.BlockSpec((TM, D), lambda i: (i, 0)),
            pl.BlockSpec((1, HG_DK), fixed),
            pl.BlockSpec((D, D), fixed),
            pl.BlockSpec((TM, D), lambda i: (i, 0)),
            pl.BlockSpec((None, N_MOD, D), lambda i: (_cond_idx(i, TM), 0, 0)),
        ],
        out_specs=pl.BlockSpec((TM, D), lambda i: (i, 0)),
        out_shape=jax.ShapeDtypeStruct((R, D), F32),
        compiler_params=_cparams(("arbitrary",)),
        name="hgrn_out",
    )(of_p, ob_p, of_s, ob_s, gg, o_norm, w_o, x, mods_l)


def _router_kernel(x_ref, m_ref, g_ref, rt_ref, h8_ref, idx_ref, wt_ref, cnt_ref):
    m = m_ref[...]
    h = _normmod(x_ref[...], g_ref[...], m[4:5], m[3:4])
    tm = h.shape[0]
    for s in range(ROW_TILE):
        h8_ref[pl.ds(s, tm, stride=ROW_TILE), :] = h[:, s * LANES:(s + 1) * LANES]
    h1, h2, h3 = _split3(h)
    r1, r2, r3 = _split3(rt_ref[...])
    lt = (_dot_nt(r1, h1) + _dot_nt(r1, h2) + _dot_nt(r2, h1)
          + _dot_nt(r1, h3) + _dot_nt(r3, h1) + _dot_nt(r2, h2))
    lg = lt[:N_EXP]
    e = jnp.exp(lg - jnp.max(lg, axis=0, keepdims=True))
    p = e / jnp.sum(e, axis=0, keepdims=True)
    io = lax.broadcasted_iota(I32, p.shape, 0)
    m1 = jnp.max(p, axis=0, keepdims=True)
    i1 = jnp.min(jnp.where(p == m1, io, N_EXP), axis=0, keepdims=True)
    p2 = jnp.where(io == i1, -1.0, p)
    m2 = jnp.max(p2, axis=0, keepdims=True)
    i2 = jnp.min(jnp.where(p2 == m2, io, N_EXP), axis=0, keepdims=True)
    den = m1 + m2
    idx_ref[...] = jnp.concatenate([i1, i2], axis=0)
    wt_ref[...] = jnp.concatenate([m1 / den, m2 / den], axis=0)
    chosen = jnp.where(io == i1, 1.0, 0.0) + jnp.where(io == i2, 1.0, 0.0)

    @pl.when(pl.program_id(0) == 0)
    def _():
        cnt_ref[...] = jnp.zeros_like(cnt_ref)

    cnt_ref[...] += jnp.broadcast_to(jnp.sum(chosen, axis=1, keepdims=True), cnt_ref.shape)


def _router(x, mods_l, gain, router_t):
    return pl.pallas_call(
        _router_kernel,
        grid=(R // TM,),
        in_specs=[
            pl.BlockSpec((TM, D), lambda i: (i, 0)),
            pl.BlockSpec((None, N_MOD, D), lambda i: (_cond_idx(i, TM), 0, 0)),
            pl.BlockSpec((1, D), lambda i: (0, 0)),
            pl.BlockSpec((2 * SUB, D), lambda i: (0, 0)),
        ],
        out_specs=[
            pl.BlockSpec((TM * ROW_TILE, LANES), lambda i: (i, 0)),
            pl.BlockSpec((TOP_K, TM), lambda i: (0, i)),
            pl.BlockSpec((TOP_K, TM), lambda i: (0, i)),
            pl.BlockSpec((N_EXP, LANES), lambda i: (0, 0)),
        ],
        out_shape=[
            jax.ShapeDtypeStruct((R * ROW_TILE, LANES), F32),
            jax.ShapeDtypeStruct((TOP_K, R), I32),
            jax.ShapeDtypeStruct((TOP_K, R), F32),
            jax.ShapeDtypeStruct((N_EXP, LANES), F32),
        ],
        compiler_params=_cparams(("arbitrary",)),
        name="moe_router",
    )(x, mods_l, gain, router_t)


IDX_BLOCK = 1024
DMA_UNROLL = 8


def _slot_kernel(e_ref, gs_ref, pos_ref, tri_scr, carry_scr):
    @pl.when(pl.program_id(0) == 0)
    def _():
        r = lax.broadcasted_iota(I32, tri_scr.shape, 0)
        c = lax.broadcasted_iota(I32, tri_scr.shape, 1)
        tri_scr[...] = jnp.where(r <= c, 1.0, 0.0).astype(BF16)
        carry_scr[...] = jnp.zeros_like(carry_scr)

    e = e_ref[...]
    io = lax.broadcasted_iota(I32, (2 * SUB, IDX_BLOCK), 0)
    onehot = jnp.where(io == e, 1.0, 0.0)
    cum = _dot(onehot.astype(BF16), tri_scr[...])
    carry = carry_scr[...]
    slot = cum - 1.0 + carry[:, 0:1] + gs_ref[...][:, 0:1]
    pos_ref[...] = jnp.sum(onehot * slot, axis=0, keepdims=True).astype(I32)
    carry_scr[...] = carry + jnp.broadcast_to(cum[:, IDX_BLOCK - 1:IDX_BLOCK], carry.shape)


def _slots(idx, group_start):
    n_blk = TOP_K * R // IDX_BLOCK
    gs = jnp.broadcast_to(
        jnp.concatenate([group_start, jnp.zeros((2 * SUB - N_EXP,), I32)]).astype(F32)[:, None], (2 * SUB, LANES))
    pos = pl.pallas_call(
        _slot_kernel,
        grid=(n_blk,),
        in_specs=[
            pl.BlockSpec((None, 1, IDX_BLOCK), lambda c: (c, 0, 0)),
            pl.BlockSpec((2 * SUB, LANES), lambda c: (0, 0)),
        ],
        out_specs=pl.BlockSpec((None, 1, IDX_BLOCK), lambda c: (c, 0, 0)),
        out_shape=jax.ShapeDtypeStruct((n_blk, 1, IDX_BLOCK), I32),
        scratch_shapes=[pltpu.VMEM((IDX_BLOCK, IDX_BLOCK), BF16), pltpu.VMEM((2 * SUB, LANES), F32)],
        compiler_params=_cparams(("arbitrary",)),
        name="moe_slots",
    )(idx.reshape(n_blk, 1, IDX_BLOCK), gs)
    return pos.reshape(TOP_K * R)


N_PAD = N_SLOT - TOP_K * R


def _dispatch_kernel(p0_ref, p1_ref, pad_ref, h8_ref, hs8_ref, zero_scr, sem, zsem):
    n_tok = h8_ref.shape[0] // ROW_TILE

    @pl.when(pl.program_id(0) == 0)
    def _():
        zero_scr[...] = jnp.zeros_like(zero_scr)

        def fill(g, carry):
            for u in range(DMA_UNROLL):
                slot = pad_ref[g * DMA_UNROLL + u]
                pltpu.make_async_copy(
                    zero_scr, hs8_ref.at[pl.ds(pl.multiple_of(slot * SUB, SUB), SUB)], zsem).start(priority=u % 2)
            return carry

        lax.fori_loop(0, N_PAD // DMA_UNROLL, fill, 0)
        for _ in range(N_PAD // n_tok):
            pltpu.make_async_copy(h8_ref, hs8_ref.at[pl.ds(0, n_tok * ROW_TILE)], zsem).wait()

    for p_ref in (p0_ref, p1_ref):
        def issue(g, carry, p_ref=p_ref):
            for u in range(DMA_UNROLL):
                r = g * DMA_UNROLL + u
                pltpu.make_async_copy(
                    h8_ref.at[pl.ds(pl.multiple_of(r * SUB, SUB), SUB)],
                    hs8_ref.at[pl.ds(pl.multiple_of(p_ref[r] * SUB, SUB), SUB)],
                    sem).start(priority=u % 2)
            return carry

        lax.fori_loop(0, n_tok // DMA_UNROLL, issue, 0)
    for _ in range(TOP_K):
        pltpu.make_async_copy(h8_ref, hs8_ref.at[pl.ds(0, n_tok * ROW_TILE)], sem).wait()


def _dispatch(pos, pad_slots, h8):
    nt = R // IDX_BLOCK
    return pl.pallas_call(
        _dispatch_kernel,
        grid=(nt,),
        in_specs=[
            pl.BlockSpec((IDX_BLOCK,), lambda i: (i,), memory_space=pltpu.SMEM),
            pl.BlockSpec((IDX_BLOCK,), lambda i: (nt + i,), memory_space=pltpu.SMEM),
            pl.BlockSpec((N_PAD,), lambda i: (0,), memory_space=pltpu.SMEM),
            pl.BlockSpec((IDX_BLOCK * ROW_TILE, LANES), lambda i: (i, 0)),
        ],
        out_specs=pl.BlockSpec(memory_space=pl.ANY),
        out_shape=jax.ShapeDtypeStruct((N_SLOT * ROW_TILE, LANES), F32),
        scratch_shapes=[pltpu.VMEM((SUB, LANES), F32), pltpu.SemaphoreType.DMA(()), pltpu.SemaphoreType.DMA(())],
        compiler_params=_cparams(("arbitrary",)),
        name="moe_dispatch",
    )(pos, pos, pad_slots, h8)


EXP_CHUNK = 1792
N_EXP_CHUNK = E_FF // EXP_CHUNK


def _expert_kernel(te_ref, tv_ref, x8_ref, w1_ref, w3_ref, w2_ref, y8_ref, xb_scr, acc_scr):
    i = pl.program_id(0)
    kc = pl.program_id(1)
    valid = tv_ref[i] == 1

    @pl.when(valid & (kc == 0))
    def _():
        for s in range(ROW_TILE):
            xb_scr[:, s * LANES:(s + 1) * LANES] = x8_ref[pl.ds(s, TM_MOE, stride=ROW_TILE), :].astype(BF16)
        acc_scr[...] = jnp.zeros_like(acc_scr)

    @pl.when(valid)
    def _():
        x = xb_scr[...]
        a = _dot(x, w1_ref[...])
        b = _dot(x, w3_ref[...])
        acc_scr[...] += _dot((_silu(a) * b).astype(BF16), w2_ref[...])

    @pl.when(valid & (kc == N_EXP_CHUNK - 1))
    def _():
        for s in range(ROW_TILE):
            y8_ref[pl.ds(s, TM_MOE, stride=ROW_TILE), :] = acc_scr[:, s * LANES:(s + 1) * LANES]

    @pl.when(jnp.logical_not(valid) & (kc == N_EXP_CHUNK - 1))
    def _():
        y8_ref[...] = jnp.zeros_like(y8_ref)


def _experts(tile_expert, tile_valid, hs8, w1, w3, w2, j):
    def kc_eff(kc, tv, i):
        return jnp.where(tv[i] == 1, kc, N_EXP_CHUNK - 1)

    grid_spec = pltpu.PrefetchScalarGridSpec(
        num_scalar_prefetch=2,
        grid=(N_TILE, N_EXP_CHUNK),
        in_specs=[
            pl.BlockSpec((TM_MOE * ROW_TILE, LANES), lambda i, kc, te, tv: (i, 0)),
            pl.BlockSpec((None, None, D, EXP_CHUNK), lambda i, kc, te, tv: (j, te[i], 0, kc_eff(kc, tv, i))),
            pl.BlockSpec((None, None, D, EXP_CHUNK), lambda i, kc, te, tv: (j, te[i], 0, kc_eff(kc, tv, i))),
            pl.BlockSpec((None, None, EXP_CHUNK, D), lambda i, kc, te, tv: (j, te[i], kc_eff(kc, tv, i), 0)),
        ],
        out_specs=pl.BlockSpec((TM_MOE * ROW_TILE, LANES), lambda i, kc, te, tv: (i, 0)),
        scratch_shapes=[pltpu.VMEM((TM_MOE, D), BF16), pltpu.VMEM((TM_MOE, D), F32)],
    )
    return pl.pallas_call(
        _expert_kernel,
        grid_spec=grid_spec,
        out_shape=jax.ShapeDtypeStruct((N_SLOT * ROW_TILE, LANES), F32),
        compiler_params=_cparams(("arbitrary", "arbitrary"), 56),
        name="moe_experts",
    )(tile_expert, tile_valid, hs8, w1, w3, w2)


TILES_PER_IDX_BLOCK = IDX_BLOCK // TM


def _moe_resid_kernel(p0_ref, p1_ref, y8_ref, wc_ref, x_ref, m_ref, fn_ref, o_ref, ya_scr, yb_scr, sems, *, final):
    i = pl.program_id(0)
    nt = pl.num_programs(0)
    tm = x_ref.shape[0]
    bufs = (ya_scr, yb_scr)

    def start(tile, b):
        off = (tile % TILES_PER_IDX_BLOCK) * tm
        for c, p_ref in enumerate((p0_ref, p1_ref)):
            def issue(g, carry, p_ref=p_ref, c=c):
                for u in range(DMA_UNROLL):
                    r = g * DMA_UNROLL + u
                    pltpu.make_async_copy(
                        y8_ref.at[pl.ds(pl.multiple_of(p_ref[off + r] * SUB, SUB), SUB)],
                        bufs[b].at[c, pl.ds(pl.multiple_of(r * SUB, SUB), SUB)],
                        sems.at[b]).start(priority=u % 2)
                return carry

            lax.fori_loop(0, tm // DMA_UNROLL, issue, 0)

    def finish(b):
        for c in range(TOP_K):
            pltpu.make_async_copy(y8_ref.at[pl.ds(0, tm * ROW_TILE)], bufs[b].at[c], sems.at[b]).wait()

    def rows(ref):
        return jnp.concatenate([ref[pl.ds(s, tm, stride=ROW_TILE), :] for s in range(ROW_TILE)], axis=1)

    def combine(b):
        w = wc_ref[...]
        f = w[:, 0:1] * rows(bufs[b].at[0]) + w[:, 1:2] * rows(bufs[b].at[1])
        xn = x_ref[...] + m_ref[...][5:6] * f
        if final:
            xn = _rms(xn, fn_ref[...])
        o_ref[...] = xn

    @pl.when(i == 0)
    def _():
        start(0, 0)

    for b in range(2):
        @pl.when(i % 2 == b)
        def _(b=b):
            @pl.when(i + 1 < nt)
            def _():
                start(i + 1, 1 - b)

            finish(b)
            combine(b)


def _moe_resid(pos, y8, wcol, x, mods_l, final_gain, final):
    nt = R // TM
    nblk = R // IDX_BLOCK

    def next_blk(i):
        return jnp.minimum(i + 1, nt - 1) // TILES_PER_IDX_BLOCK

    return pl.pallas_call(
        functools.partial(_moe_resid_kernel, final=final),
        grid=(nt,),
        in_specs=[
            pl.BlockSpec((IDX_BLOCK,), lambda i: (next_blk(i),), memory_space=pltpu.SMEM),
            pl.BlockSpec((IDX_BLOCK,), lambda i: (nblk + next_blk(i),), memory_space=pltpu.SMEM),
            pl.BlockSpec(memory_space=pl.ANY),
            pl.BlockSpec((TM, TOP_K), lambda i: (i, 0)),
            pl.BlockSpec((TM, D), lambda i: (i, 0)),
            pl.BlockSpec((None, N_MOD, D), lambda i: (_cond_idx(i, TM), 0, 0)),
            pl.BlockSpec((1, D), lambda i: (0, 0)),
        ],
        out_specs=pl.BlockSpec((TM, D), lambda i: (i, 0)),
        out_shape=jax.ShapeDtypeStruct((R, D), F32),
        scratch_shapes=[pltpu.VMEM((TOP_K, TM * ROW_TILE, LANES), F32), pltpu.VMEM((TOP_K, TM * ROW_TILE, LANES), F32),
                        pltpu.SemaphoreType.DMA((2,))],
        compiler_params=_cparams(("arbitrary",), 48),
        name="moe_resid",
    )(pos, pos, y8, wcol, x, mods_l, final_gain)


def _tile_tables(counts):
    padded = ((counts + TM_MOE - 1) // TM_MOE) * TM_MOE
    gend = jnp.cumsum(padded)
    tile_start = jnp.arange(N_TILE, dtype=I32) * TM_MOE
    te = jnp.sum((tile_start[:, None] >= gend[None, :]).astype(I32), axis=1)
    valid = tile_start < gend[-1]
    te_last = te[gend[-1] // TM_MOE - 1]
    te = jnp.minimum(jnp.where(valid, te, te_last), N_EXP - 1)
    gstart = gend - padded
    j = jnp.arange(TM_MOE, dtype=I32)[None, :]
    in_group = (j < (padded - counts)[:, None]).reshape(N_PAD)
    group_pad = ((gstart + counts)[:, None] + j).reshape(N_PAD)
    trailing = gend[-1] + jnp.cumsum(jnp.logical_not(in_group).astype(I32)) - 1
    pad_slots = jnp.where(in_group, group_pad, trailing)
    return gstart.astype(I32), te.astype(I32), valid.astype(I32), pad_slots.astype(I32)


def _rot_half(w):
    wa = w.reshape(w.shape[:-1] + (2, 2, ROPE_AXIS // 2))
    return jnp.stack([-wa[..., 1, :], wa[..., 0, :]], axis=-2).reshape(w.shape)


def _rope_tables(tm):
    rows = DEC_SEQ // GRID_W
    r = jnp.repeat(jnp.arange(rows), GRID_W).astype(F32)
    c = jnp.tile(jnp.arange(GRID_W), rows).astype(F32)
    inv = ROPE_THETA ** (-jnp.arange(0, ROPE_AXIS, 2, dtype=F32) / ROPE_AXIS)
    ang_r = r[:, None] * inv
    ang_c = c[:, None] * inv
    ang = jnp.concatenate([ang_r, ang_r, ang_c, ang_c], axis=-1)
    cos, sin = jnp.cos(ang), jnp.sin(ang)
    z32 = jnp.zeros_like(cos)
    one = jnp.ones((tm, QK_ROPE), F32)
    zt = jnp.zeros((tm, QK_ROPE), F32)
    tq_c = jnp.concatenate([jnp.concatenate([cos, cos, z32, z32], 1), jnp.concatenate([one, one, zt, zt], 1)], 0)
    tq_s = jnp.concatenate([jnp.concatenate([sin, sin, z32, z32], 1), jnp.zeros((tm, LANES), F32)], 0)
    tk_c = jnp.concatenate([jnp.concatenate([cos, z32, z32, z32], 1), jnp.concatenate([one, zt, zt, zt], 1)], 0)
    tk_s = jnp.concatenate([jnp.concatenate([sin, z32, z32, z32], 1), jnp.zeros((tm, LANES), F32)], 0)
    return tq_c, tq_s, tk_c, tk_s


def _mla_weights(w_dq, w_uq, w_dkv, w_uk, w_uv, w_o):
    kr_w = w_dkv[:, KV_LORA:]
    w_a = jnp.concatenate(
        [w_dq, w_dkv[:, :KV_LORA], kr_w, _rot_half(kr_w), jnp.zeros((D, LANES - 2 * QK_ROPE), F32)], axis=1)
    uq = w_uq.reshape(Q_LORA, N_PAIR, 2, QK_NOPE + QK_ROPE)
    nope = uq[..., :QK_NOPE].reshape(Q_LORA, N_PAIR, 2 * QK_NOPE)
    rope = uq[..., QK_NOPE:]
    w_uq2 = jnp.concatenate(
        [nope, rope.reshape(Q_LORA, N_PAIR, 2 * QK_ROPE), _rot_half(rope).reshape(Q_LORA, N_PAIR, 2 * QK_ROPE)],
        axis=-1).reshape(Q_LORA, N_PAIR * 2 * LANES)
    uk = w_uk.reshape(KV_LORA, N_PAIR, 2 * QK_NOPE)
    wk = jnp.concatenate([uk, jnp.zeros((KV_LORA, N_PAIR, LANES), F32)], axis=-1).reshape(KV_LORA, N_PAIR * 2 * LANES)
    eye = jnp.eye(LANES, QK_ROPE, dtype=F32)
    pair = jnp.concatenate([jnp.zeros((LANES, LANES), F32), eye, eye, jnp.zeros((LANES, LANES - 2 * QK_ROPE), F32)], 1)
    wkr = jnp.tile(pair, (1, N_PAIR))
    wv = w_uv.reshape(KV_LORA, HEADS * V_HEAD)
    wo = w_o.reshape(HEADS * V_HEAD, D)
    return [w.astype(BF16) for w in (w_a, w_uq2, wk, wkr, wv, wo)]


def kernel(x_prompt, x_sample, cache_ckv, cache_krope, state_hgrn, c, c_ctx, w_ada, b_ada, norm_mix, norm_ffn, mla_w_dq, mla_q_norm, mla_w_uq, mla_w_dkv, mla_kv_norm, mla_w_uk, mla_w_uv, mla_w_o, hg_w_q, hg_w_f, hg_w_i, hg_w_g, hg_lb_logits, hg_o_norm, hg_w_o, ffn_w1, ffn_w3, ffn_w2, moe_router, moe_w1, moe_w3, moe_w2, final_norm):
    x = (x_prompt.reshape(R_P, D), x_sample.reshape(R_S, D))
    cond = jnp.concatenate([c, c_ctx[None], jnp.zeros((COND_PAD - N_COND, D), F32)], axis=0)
    mods = _ada(cond, w_ada, b_ada)
    lb_all = _lower_bounds(hg_lb_logits)
    tq_c, tq_s, tk_c, tk_s = _rope_tables(TM)
    ffn_w = [w.astype(BF16) for w in (ffn_w1, ffn_w3, ffn_w2)]
    moe_w = [w.astype(BF16) for w in (moe_w1, moe_w3, moe_w2)]
    new_ckv, new_krope, new_hgrn = [], [], []

    for l in range(DEPTH):
        j = l // 2
        mods_l = mods[l]
        gain_mix = norm_mix[l][None]
        gain_ffn = norm_ffn[l][None]
        if l % 2 == 0:
            w_a, w_uq2, wk, wkr, wv, wo = _mla_weights(
                mla_w_dq[j], mla_w_uq[j], mla_w_dkv[j], mla_w_uk[j], mla_w_uv[j], mla_w_o[j])
            cq, ckv, kr = _mla_proj(x, mods_l, gain_mix, w_a, mla_q_norm[j][None], mla_kv_norm[j][None], tk_c, tk_s)
            q2 = _q_up(cq, w_uq2, tq_c, tq_s)
            k2_p, v_p = _kv_up(ckv, kr, R_P, wk, wkr, wv)
            ckv_s = jnp.concatenate([cache_ckv[:, j], ckv[R_P:].reshape(DEC_BATCH, DEC_SEQ, KV_LORA)], axis=1)
            kr_cache = jnp.pad(cache_krope[:, j], ((0, 0), (0, 0), (0, LANES - QK_ROPE)))
            kr_s = jnp.concatenate([kr_cache, kr[R_P:].reshape(DEC_BATCH, DEC_SEQ, LANES)], axis=1)
            s_all = PAST + DEC_SEQ
            k2_s, v_s = _kv_up(ckv_s.reshape(DEC_BATCH * s_all, KV_LORA), kr_s.reshape(DEC_BATCH * s_all, LANES),
                               DEC_BATCH * s_all, wk, wkr, wv)
            o_p = _attention(q2, k2_p, v_p, BATCH, SEQ, SEQ, 0, SEQ, N_PAIR)
            o_s = _attention(q2, k2_s, v_s, DEC_BATCH, DEC_SEQ, s_all, R_P, 512, N_PAIR)
            x = _mm_resid((o_p, o_s), wo, x, mods_l, 2)
            new_ckv.append(ckv[:R_P].reshape(BATCH, SEQ, KV_LORA))
            new_krope.append(kr[:R_P, :QK_ROPE].reshape(BATCH, SEQ, QK_ROPE))
        else:
            w5 = jnp.stack([hg_w_q[j], hg_w_f[j, 0], hg_w_f[j, 1], hg_w_i[j], hg_w_g[j]]).astype(BF16)
            q, k, lf, v, gg = _hg_proj(x, mods_l, gain_mix, w5, lb_all[j])
            s_zero = jnp.zeros((BATCH, 2, HG_H, HG_DK, HG_DK), F32)
            of_p, ob_p, st_p = _gla(q, k, lf, v, s_zero, BATCH, SEQ, 0)
            of_s, ob_s, _ = _gla(q, k, lf, v, state_hgrn[:, j], DEC_BATCH, DEC_SEQ, R_P)
            x = _hg_out(of_p, ob_p, of_s, ob_s, gg, hg_o_norm[j][None], hg_w_o[j].astype(BF16), x, mods_l)
            new_hgrn.append(st_p)
        if l % 2 == 0:
            x = _ffn(x, mods_l, gain_ffn, *ffn_w, j)
        else:
            router_t = jnp.concatenate([moe_router[j].T, jnp.zeros((2 * SUB - N_EXP, D), F32)], axis=0)
            h8, idx, wts, cnt = _router(x, mods_l, gain_ffn, router_t)
            group_start, tile_expert, tile_valid, pad_slots = _tile_tables(cnt[:, 0].astype(I32))
            pos = _slots(idx, group_start)
            y8 = _experts(tile_expert, tile_valid, _dispatch(pos, pad_slots, h8), *moe_w, j)
            x = _moe_resid(pos, y8, wts.T, x, mods_l, final_norm[None], final=(l == DEPTH - 1))

    y_prompt = x[:R_P].reshape(BATCH, SEQ, D)
    y_sample = x[R_P:].reshape(DEC_BATCH, DEC_SEQ, D)
    return (y_prompt, y_sample, jnp.stack(new_ckv, axis=1), jnp.stack(new_krope, axis=1),
            jnp.stack(new_hgrn, axis=1))
```

```python
import functools
import math

import jax
import jax.numpy as jnp
from jax import lax
from jax.experimental import pallas as pl
from jax.experimental.pallas import tpu as pltpu

F32 = jnp.float32
BF16 = jnp.bfloat16
I32 = jnp.int32

D = 1024
BATCH, SEQ = 32, 256
DEC_BATCH, DEC_SEQ = 8, 1024
PAST = 256
DEPTH = 4
R_P = BATCH * SEQ
R_S = DEC_BATCH * DEC_SEQ
R = R_P + R_S
N_COND = DEC_BATCH + 1
COND_PAD = 16
N_MOD = 6
EPS = 1e-6
GRID_W = 64
HEADS, QK_NOPE, QK_ROPE, V_HEAD = 16, 64, 32, 64
Q_LORA, KV_LORA = 512, 256
N_PAIR = HEADS // 2
ROPE_AXIS = QK_ROPE // 2
ROPE_THETA = 10000.0
HG_H, HG_DK = 8, 128
BLK = 128
CHUNK = 16
D_FF = 2816
N_EXP, TOP_K, E_FF = 8, 2, 3584
LANES = 128
SUB = 8
ROW_TILE = D // LANES
TM = 512
TM_MOE = 512
N_SLOT = TOP_K * R + N_EXP * TM_MOE
N_TILE = N_SLOT // TM_MOE
MIB = 1024 * 1024
assert DEPTH % 2 == 0


def _cparams(sem, vmem_mib=40):
    return pltpu.CompilerParams(dimension_semantics=sem, vmem_limit_bytes=vmem_mib * MIB)


def _cond_idx(i, tm):
    r0 = i * tm
    return jnp.where(r0 < R_P, DEC_BATCH, (r0 - R_P) // DEC_SEQ)


def _rope_idx(i, tm):
    nb = DEC_SEQ // tm
    r0 = i * tm
    return jnp.where(r0 < R_P, nb, ((r0 - R_P) // tm) % nb)


def _normmod(x, gain, scale, shift):
    ms = jnp.mean(x * x, axis=-1, keepdims=True)
    return (x * lax.rsqrt(ms + EPS)) * gain * (1.0 + scale) + shift


def _rms(x, gain):
    ms = jnp.mean(x * x, axis=-1, keepdims=True)
    return (x * lax.rsqrt(ms + EPS)) * gain


def _silu(x):
    return x / (1.0 + jnp.exp(-x))


def _dot(a, b):
    return jnp.dot(a, b, preferred_element_type=F32)


def _dot_nt(a, b):
    return lax.dot_general(a, b, (((1,), (1,)), ((), ())), preferred_element_type=F32)


def _split3(x):
    hi = x.astype(BF16)
    r1 = x - hi.astype(F32)
    mid = r1.astype(BF16)
    lo = (r1 - mid.astype(F32)).astype(BF16)
    return hi, mid, lo


def _ada_kernel(c_ref, w_ref, b_ref, o_ref):
    o_ref[...] = _dot(_silu(c_ref[...]), w_ref[...]) + b_ref[...]


def _ada(cond, w_ada, b_ada):
    tn = 1536
    out = pl.pallas_call(
        _ada_kernel,
        grid=(DEPTH, N_MOD * D // tn),
        in_specs=[
            pl.BlockSpec((COND_PAD, D), lambda l, j: (0, 0)),
            pl.BlockSpec((None, D, tn), lambda l, j: (l, 0, j)),
            pl.BlockSpec((None, 1, tn), lambda l, j: (l, 0, j)),
        ],
        out_specs=pl.BlockSpec((None, COND_PAD, tn), lambda l, j: (l, 0, j)),
        out_shape=jax.ShapeDtypeStruct((DEPTH, COND_PAD, N_MOD * D), F32),
        compiler_params=_cparams(("arbitrary", "arbitrary")),
        name="ada",
    )(cond, w_ada, b_ada.reshape(DEPTH, 1, N_MOD * D))
    return out[:, :N_COND].reshape(DEPTH, N_COND, N_MOD, D)


def _lb_kernel(x_ref, o_ref):
    x = x_ref[...]
    e = jnp.exp(x - jnp.max(x, axis=0, keepdims=True))
    sm = e / jnp.sum(e, axis=0, keepdims=True)
    n = x.shape[0]
    cum = sm[0]
    o_ref[0] = jnp.zeros_like(cum)
    for i in range(1, n):
        cum_i = cum + sm[i]
        o_ref[i] = cum_i - sm[0]
        cum = cum_i


def _lower_bounds(lb_logits):
    return pl.pallas_call(
        _lb_kernel,
        out_shape=jax.ShapeDtypeStruct(lb_logits.shape, F32),
        name="hgrn_lower_bounds",
    )(lb_logits)


def _two_part_rows(x):
    if isinstance(x, tuple):
        x_p, x_s = x
        s_off = 0
    else:
        x_p = x_s = x
        s_off = N_TILE_P
    w = x_p.shape[1]
    specs = [pl.BlockSpec((TM, w), lambda i: (_prompt_blk(i), 0)),
             pl.BlockSpec((TM, w), lambda i: (s_off + _sample_blk(i), 0))]
    return [x_p, x_s], specs


def _pick_rows(p_ref, s_ref):
    return jnp.where(pl.program_id(0) < N_TILE_P, p_ref[...], s_ref[...])


def _mla_proj_kernel(xp_ref, xs_ref, m_ref, g_ref, w_ref, qn_ref, kvn_ref, tc_ref, ts_ref,
                     cq_ref, ckv_ref, kr_ref):
    m = m_ref[...]
    h = _normmod(_pick_rows(xp_ref, xs_ref), g_ref[...], m[1:2], m[0:1]).astype(BF16)
    y = _dot(h, w_ref[...])
    cq_ref[...] = _rms(y[:, :Q_LORA], qn_ref[...]).astype(BF16)
    ckv_ref[...] = _rms(y[:, Q_LORA:Q_LORA + KV_LORA], kvn_ref[...])
    slab = y[:, Q_LORA + KV_LORA:]
    kr_ref[...] = slab * tc_ref[...] + pltpu.roll(slab, LANES - QK_ROPE, 1) * ts_ref[...]


def _mla_proj(x, mods_l, gain, w_a, q_norm, kv_norm, tk_c, tk_s):
    n_a = w_a.shape[1]
    row = lambda i: (i, 0)
    fixed = lambda i: (0, 0)
    x_args, x_specs = _two_part_rows(x)
    return pl.pallas_call(
        _mla_proj_kernel,
        grid=(R // TM,),
        in_specs=x_specs + [
            pl.BlockSpec((None, N_MOD, D), lambda i: (_cond_idx(i, TM), 0, 0)),
            pl.BlockSpec((1, D), fixed),
            pl.BlockSpec((D, n_a), fixed),
            pl.BlockSpec((1, Q_LORA), fixed),
            pl.BlockSpec((1, KV_LORA), fixed),
            pl.BlockSpec((TM, LANES), lambda i: (_rope_idx(i, TM), 0)),
            pl.BlockSpec((TM, LANES), lambda i: (_rope_idx(i, TM), 0)),
        ],
        out_specs=[
            pl.BlockSpec((TM, Q_LORA), row),
            pl.BlockSpec((TM, KV_LORA), row),
            pl.BlockSpec((TM, LANES), row),
        ],
        out_shape=[
            jax.ShapeDtypeStruct((R, Q_LORA), BF16),
            jax.ShapeDtypeStruct((R, KV_LORA), F32),
            jax.ShapeDtypeStruct((R, LANES), F32),
        ],
        compiler_params=_cparams(("arbitrary",)),
        name="mla_proj",
    )(*x_args, mods_l, gain, w_a, q_norm, kv_norm, tk_c, tk_s)


QK_SCALE_LOG2 = math.log2(math.e) / math.sqrt(QK_NOPE + QK_ROPE)


def _q_up_kernel(cq_ref, w_ref, tc_ref, ts_ref, q_ref):
    y = _dot(cq_ref[...], w_ref[...]) * QK_SCALE_LOG2
    tc = tc_ref[...]
    ts = ts_ref[...]
    for p in range(N_PAIR):
        lo = p * 2 * LANES
        q_ref[:, lo:lo + LANES] = y[:, lo:lo + LANES].astype(BF16)
        hi = y[:, lo + LANES:lo + 2 * LANES]
        q_ref[:, lo + LANES:lo + 2 * LANES] = (hi * tc + pltpu.roll(hi, LANES // 2, 1) * ts).astype(BF16)


def _q_up(cq, w_uq2, tq_c, tq_s):
    nq = w_uq2.shape[1]
    return pl.pallas_call(
        _q_up_kernel,
        grid=(R // TM,),
        in_specs=[
            pl.BlockSpec((TM, Q_LORA), lambda i: (i, 0)),
            pl.BlockSpec((Q_LORA, nq), lambda i: (0, 0)),
            pl.BlockSpec((TM, LANES), lambda i: (_rope_idx(i, TM), 0)),
            pl.BlockSpec((TM, LANES), lambda i: (_rope_idx(i, TM), 0)),
        ],
        out_specs=pl.BlockSpec((TM, nq), lambda i: (i, 0)),
        out_shape=jax.ShapeDtypeStruct((R, nq), BF16),
        compiler_params=_cparams(("arbitrary",)),
        name="mla_q_up",
    )(cq, w_uq2, tq_c, tq_s)


def _kv_up_kernel(c_ref, r_ref, wk_ref, wkr_ref, wv_ref, k_ref, v_ref):
    c = c_ref[...].astype(BF16)
    r = r_ref[...].astype(BF16)
    k_ref[...] = (_dot(c, wk_ref[...]) + _dot(r, wkr_ref[...])).astype(BF16)
    v_ref[...] = _dot(c, wv_ref[...]).astype(BF16)


def _kv_up(ckv, kr, n_rows, wk, wkr, wv):
    nk = wk.shape[1]
    nv = wv.shape[1]
    fixed = lambda i: (0, 0)
    return pl.pallas_call(
        _kv_up_kernel,
        grid=(n_rows // TM,),
        in_specs=[
            pl.BlockSpec((TM, KV_LORA), lambda i: (i, 0)),
            pl.BlockSpec((TM, LANES), lambda i: (i, 0)),
            pl.BlockSpec((KV_LORA, nk), fixed),
            pl.BlockSpec((LANES, nk), fixed),
            pl.BlockSpec((KV_LORA, nv), fixed),
        ],
        out_specs=[
            pl.BlockSpec((TM, nk), lambda i: (i, 0)),
            pl.BlockSpec((TM, nv), lambda i: (i, 0)),
        ],
        out_shape=[
            jax.ShapeDtypeStruct((n_rows, nk), BF16),
            jax.ShapeDtypeStruct((n_rows, nv), BF16),
        ],
        compiler_params=_cparams(("arbitrary",)),
        name="mla_kv_up",
    )(ckv, kr, wk, wkr, wv)


def _attn_kernel(q_ref, k_ref, v_ref, *rest, n_pair):
    o_ref = rest[-1]
    lq = lax.broadcasted_iota(I32, (1, 2 * LANES), 1)
    lv = lax.broadcasted_iota(I32, (1, LANES), 1)
    sel_a = (lq < QK_NOPE) | ((lq >= LANES) & (lq < LANES + QK_ROPE))
    sel_b = ((lq >= QK_NOPE) & (lq < LANES)) | ((lq >= LANES + QK_ROPE) & (lq < LANES + 2 * QK_ROPE))
    for p in range(n_pair):
        q = q_ref[:, p * 2 * LANES:(p + 1) * 2 * LANES]
        k = k_ref[:, p * 2 * LANES:(p + 1) * 2 * LANES]
        v = v_ref[:, p * LANES:(p + 1) * LANES]
        zq = jnp.zeros_like(q)
        zv = jnp.zeros_like(v)
        out = None
        for sel, vsel in ((sel_a, lv < V_HEAD), (sel_b, lv >= V_HEAD)):
            s = _dot_nt(jnp.where(sel, q, zq), k)
            e = jnp.exp2(s - jnp.max(s, axis=-1, keepdims=True))
            den = jnp.sum(e, axis=-1, keepdims=True)
            o = _dot(e.astype(BF16), jnp.where(vsel, v, zv)) / den
            out = o if out is None else out + o
        o_ref[:, p * LANES:(p + 1) * LANES] = out.astype(BF16)


def _attention(q2, k2, v, n_batch, t_len, s_len, q_row0, tq, n_pair):
    nq = t_len // tq
    qb0 = q_row0 // tq
    return pl.pallas_call(
        functools.partial(_attn_kernel, n_pair=n_pair),
        grid=(n_batch, N_PAIR // n_pair, nq),
        in_specs=[
            pl.BlockSpec((tq, n_pair * 2 * LANES), lambda b, p, i: (qb0 + b * nq + i, p)),
            pl.BlockSpec((s_len, n_pair * 2 * LANES), lambda b, p, i: (b, p)),
            pl.BlockSpec((s_len, n_pair * LANES), lambda b, p, i: (b, p)),
        ],
        out_specs=pl.BlockSpec((tq, n_pair * LANES), lambda b, p, i: (b * nq + i, p)),
        out_shape=jax.ShapeDtypeStruct((n_batch * t_len, HEADS * V_HEAD), BF16),
        compiler_params=_cparams(("arbitrary", "arbitrary", "arbitrary"), 56),
        name="mla_attention",
    )(q2, k2, v)


N_TILE_P = R_P // TM


def _prompt_blk(i):
    return jnp.minimum(i, N_TILE_P - 1)


def _sample_blk(i):
    return jnp.maximum(i - N_TILE_P, 0)


def _mm_resid_kernel(ap_ref, as_ref, w_ref, xp_ref, xs_ref, m_ref, o_ref, *, gate_idx):
    gate = m_ref[...][gate_idx:gate_idx + 1]
    o_ref[...] = _pick_rows(xp_ref, xs_ref) + gate * _dot(_pick_rows(ap_ref, as_ref), w_ref[...])


def _mm_resid(a, w, x, mods_l, gate_idx):
    a_args, a_specs = _two_part_rows(a)
    x_args, x_specs = _two_part_rows(x)
    k = a_args[0].shape[1]
    return pl.pallas_call(
        functools.partial(_mm_resid_kernel, gate_idx=gate_idx),
        grid=(R // TM,),
        in_specs=a_specs + [pl.BlockSpec((k, D), lambda i: (0, 0))] + x_specs + [
            pl.BlockSpec((None, N_MOD, D), lambda i: (_cond_idx(i, TM), 0, 0)),
        ],
        out_specs=pl.BlockSpec((TM, D), lambda i: (i, 0)),
        out_shape=jax.ShapeDtypeStruct((R, D), F32),
        compiler_params=_cparams(("arbitrary",)),
        name="mm_resid",
    )(*a_args, w, *x_args, mods_l)


FFN_CHUNK = 1408


def _ffn_kernel(x_ref, m_ref, g_ref, w1_ref, w3_ref, w2_ref, o_ref):
    x = x_ref[...]
    m = m_ref[...]
    h = _normmod(x, g_ref[...], m[4:5], m[3:4]).astype(BF16)
    acc = jnp.zeros(x.shape, F32)
    for c in range(D_FF // FFN_CHUNK):
        sl = slice(c * FFN_CHUNK, (c + 1) * FFN_CHUNK)
        a = _dot(h, w1_ref[:, sl])
        b = _dot(h, w3_ref[:, sl])
        acc = acc + _dot((_silu(a) * b).astype(BF16), w2_ref[sl, :])
    o_ref[...] = x + m[5:6] * acc


def _ffn(x, mods_l, gain, w1, w3, w2, j):
    tm = TM
    fixed = lambda i: (0, 0)
    layer = lambda i: (j, 0, 0)
    once = pl.Buffered(1)
    return pl.pallas_call(
        _ffn_kernel,
        grid=(R // tm,),
        in_specs=[
            pl.BlockSpec((tm, D), lambda i: (i, 0)),
            pl.BlockSpec((None, N_MOD, D), lambda i: (_cond_idx(i, tm), 0, 0)),
            pl.BlockSpec((1, D), fixed),
            pl.BlockSpec((None, D, D_FF), layer, pipeline_mode=once),
            pl.BlockSpec((None, D, D_FF), layer, pipeline_mode=once),
            pl.BlockSpec((None, D_FF, D), layer, pipeline_mode=once),
        ],
        out_specs=pl.BlockSpec((tm, D), lambda i: (i, 0)),
        out_shape=jax.ShapeDtypeStruct((R, D), F32),
        compiler_params=_cparams(("arbitrary",), 56),
        name="dense_swiglu",
    )(x, mods_l, gain, w1, w3, w2)


def _forget_gate(z, lb):
    e = jnp.exp(-jnp.abs(z))
    log_sig = jnp.minimum(z, 0.0) - jnp.log(1.0 + e)
    a = jnp.log(lb)
    b = jnp.log1p(-lb) + log_sig
    log_f = jnp.maximum(a, b) + jnp.log(1.0 + jnp.exp(-jnp.abs(a - b)))
    k = (1.0 - lb) * (jnp.where(z >= 0, e, 1.0) / (1.0 + e))
    return k, log_f


def _hg_proj_kernel(x_ref, m_ref, g_ref, w_ref, lb_ref, q_ref, k_ref, lf_ref, v_ref, gg_ref):
    m = m_ref[...]
    h = _normmod(x_ref[...], g_ref[...], m[1:2], m[0:1]).astype(BF16)
    q_ref[...] = _dot(h, w_ref[0])
    for dr in range(2):
        k, log_f = _forget_gate(_dot(h, w_ref[1 + dr]), lb_ref[dr])
        k_ref[dr] = k
        lf_ref[dr] = log_f
    v_ref[...] = _dot(h, w_ref[3]).astype(BF16)
    gg_ref[...] = _dot(h, w_ref[4])


def _hg_proj(x, mods_l, gain, w5, lb):
    tm = TM
    row = lambda i: (i, 0)
    dirs = lambda i: (0, i, 0)
    return pl.pallas_call(
        _hg_proj_kernel,
        grid=(R // tm,),
        in_specs=[
            pl.BlockSpec((tm, D), row),
            pl.BlockSpec((None, N_MOD, D), lambda i: (_cond_idx(i, tm), 0, 0)),
            pl.BlockSpec((1, D), lambda i: (0, 0)),
            pl.BlockSpec((5, D, D), lambda i: (0, 0, 0), pipeline_mode=pl.Buffered(1)),
            pl.BlockSpec((2, 1, D), lambda i: (0, 0, 0)),
        ],
        out_specs=[
            pl.BlockSpec((tm, D), row),
            pl.BlockSpec((2, tm, D), dirs),
            pl.BlockSpec((2, tm, D), dirs),
            pl.BlockSpec((tm, D), row),
            pl.BlockSpec((tm, D), row),
        ],
        out_shape=[
            jax.ShapeDtypeStruct((R, D), F32),
            jax.ShapeDtypeStruct((2, R, D), F32),
            jax.ShapeDtypeStruct((2, R, D), F32),
            jax.ShapeDtypeStruct((R, D), BF16),
            jax.ShapeDtypeStruct((R, D), F32),
        ],
        compiler_params=_cparams(("arbitrary",), 48),
        name="hgrn_proj",
    )(x, mods_l, gain, w5, lb.reshape(2, 1, D))


LEVEL_HALVES = (64, 32, 16)


def _gla_block(d, q_ref, k_ref, lf_ref, v_ref, o_ref, st_scr, b_scr):
    row = lax.broadcasted_iota(I32, (BLK, BLK), 0)
    col = lax.broadcasted_iota(I32, (BLK, BLK), 1)
    ut, us = (row, col) if d == 0 else (BLK - 1 - row, BLK - 1 - col)
    causal = us <= ut
    tri = jnp.where(causal, 1.0, 0.0).astype(BF16)

    hi, mid, lo = _split3(lf_ref[...])
    b_scr[d] = _dot(tri, hi) + _dot(tri, mid) + _dot(tri, lo)

    level_masks = []
    for hs in LEVEL_HALVES:
        sh = int(math.log2(2 * hs))
        same = (ut >> sh) == (us >> sh)
        level_masks.append(same & ((ut & (2 * hs - 1)) >= hs) & ((us & (2 * hs - 1)) < hs))
    base_mask = ((ut >> 4) == (us >> 4)) & causal

    def split_rows(sl, half):
        parts = []
        for jr in range(BLK // (2 * half)):
            r0 = jr * 2 * half + half - 1 + d
            parts.append(jnp.broadcast_to(b_scr[d, r0:r0 + 1, sl], (2 * half, HG_DK)))
        return parts[0] if len(parts) == 1 else jnp.concatenate(parts, axis=0)

    last = BLK - 1 if d == 0 else 0
    for h in range(HG_H):
        sl = slice(h * HG_DK, (h + 1) * HG_DK)
        bh = b_scr[d, :, sl]
        q = q_ref[:, sl]
        k = k_ref[:, sl]
        v = v_ref[:, sl]

        xq = bh - split_rows(sl, CHUNK // 2)
        att = jnp.where(base_mask, _dot_nt((q * jnp.exp(xq)).astype(BF16), (k * jnp.exp(-xq)).astype(BF16)), 0.0)
        for hs, msk in zip(LEVEL_HALVES, level_masks):
            e = jnp.exp(-jnp.abs(bh - split_rows(sl, hs)))
            att = jnp.where(msk, _dot_nt((q * e).astype(BF16), (k * e).astype(BF16)), att)

        b_last = b_scr[d, last:last + 1, sl]
        q_in = (q * jnp.exp(bh)).astype(BF16)
        k_in = (k * jnp.exp(b_last - bh)).astype(BF16)
        st = st_scr[d, h]
        o_ref[:, sl] = _dot(att.astype(BF16), v) + _dot_nt(q_in, st.astype(BF16))
        v_t = v.astype(F32).T.astype(BF16)
        st_scr[d, h] = st * jnp.exp(b_last) + _dot(v_t, k_in)


def _gla_kernel(qf_ref, kf_ref, lff_ref, vf_ref, qb_ref, kb_ref, lfb_ref, vb_ref, s0_ref,
                of_ref, ob_ref, sn_ref, st_scr, b_scr, *, nb):
    n = pl.program_id(1)

    @pl.when(n == 0)
    def _():
        for s in range(GLA_SEQ):
            for d in range(2):
                for h in range(HG_H):
                    st_scr[s, d, h] = s0_ref[s, d, h].T

    for s in range(GLA_SEQ):
        _gla_block(0, qf_ref.at[s], kf_ref.at[s], lff_ref.at[s], vf_ref.at[s], of_ref.at[s], st_scr.at[s], b_scr.at[s])
        _gla_block(1, qb_ref.at[s], kb_ref.at[s], lfb_ref.at[s], vb_ref.at[s], ob_ref.at[s], st_scr.at[s], b_scr.at[s])

    @pl.when(n == nb - 1)
    def _():
        for s in range(GLA_SEQ):
            for d in range(2):
                for h in range(HG_H):
                    sn_ref[s, d, h] = st_scr[s, d, h].T


GLA_SEQ = 2


def _gla(q, k, lf, v, s0, n_batch, t_len, row0):
    nb = t_len // BLK
    g0 = row0 // t_len // GLA_SEQ
    n_all = R // t_len
    fwd = lambda n: n
    bwd = lambda n: nb - 1 - n

    def in_specs(blk, d):
        return [
            pl.BlockSpec((GLA_SEQ, BLK, D), lambda g, n: (g0 + g, blk(n), 0)),
            pl.BlockSpec((None, GLA_SEQ, BLK, D), lambda g, n: (d, g0 + g, blk(n), 0)),
            pl.BlockSpec((None, GLA_SEQ, BLK, D), lambda g, n: (d, g0 + g, blk(n), 0)),
            pl.BlockSpec((GLA_SEQ, BLK, D), lambda g, n: (g0 + g, blk(n), 0)),
        ]

    state_spec = pl.BlockSpec((GLA_SEQ, 2, HG_H, HG_DK, HG_DK), lambda g, n: (g, 0, 0, 0, 0))
    q3, v3 = (a.reshape(n_all, t_len, D) for a in (q, v))
    k4, lf4 = (a.reshape(2, n_all, t_len, D) for a in (k, lf))
    o_f, o_b, s_n = pl.pallas_call(
        functools.partial(_gla_kernel, nb=nb),
        grid=(n_batch // GLA_SEQ, nb),
        in_specs=in_specs(fwd, 0) + in_specs(bwd, 1) + [state_spec],
        out_specs=[
            pl.BlockSpec((GLA_SEQ, BLK, D), lambda g, n: (g, fwd(n), 0)),
            pl.BlockSpec((GLA_SEQ, BLK, D), lambda g, n: (g, bwd(n), 0)),
            state_spec,
        ],
        out_shape=[
            jax.ShapeDtypeStruct((n_batch, t_len, D), F32),
            jax.ShapeDtypeStruct((n_batch, t_len, D), F32),
            jax.ShapeDtypeStruct((n_batch, 2, HG_H, HG_DK, HG_DK), F32),
        ],
        scratch_shapes=[pltpu.VMEM((GLA_SEQ, 2, HG_H, HG_DK, HG_DK), F32), pltpu.VMEM((GLA_SEQ, 2, BLK, D), F32)],
        compiler_params=_cparams(("arbitrary", "arbitrary")),
        name="hgrn_recurrence",
    )(q3, k4, lf4, v3, q3, k4, lf4, v3, s0)
    return o_f.reshape(n_batch * t_len, D), o_b.reshape(n_batch * t_len, D), s_n


def _hg_out_kernel(ofp_ref, obp_ref, ofs_ref, obs_ref, gg_ref, on_ref, w_ref, x_ref, m_ref, o_ref):
    o = jnp.where(pl.program_id(0) < N_TILE_P, ofp_ref[...] + obp_ref[...], ofs_ref[...] + obs_ref[...])
    gain = on_ref[...]
    parts = []
    for h in range(HG_H):
        parts.append(_rms(o[:, h * HG_DK:(h + 1) * HG_DK], gain))
    a = (jnp.concatenate(parts, axis=1) * _silu(gg_ref[...])).astype(BF16)
    o_ref[...] = x_ref[...] + m_ref[...][2:3] * _dot(a, w_ref[...])


def _hg_out(of_p, ob_p, of_s, ob_s, gg, o_norm, w_o, x, mods_l):
    fixed = lambda i: (0, 0)
    return pl.pallas_call(
        _hg_out_kernel,
        grid=(R // TM,),
        in_specs=[
            pl.BlockSpec((TM, D), lambda i: (_prompt_blk(i), 0)),
            pl.BlockSpec((TM, D), lambda i: (_prompt_blk(i), 0)),
            pl.BlockSpec((TM, D), lambda i: (_sample_blk(i), 0)),
            pl.BlockSpec((TM, D), lambda i: (_sample_blk(i), 0)),
            pl.BlockSpec((TM, D), lambda i: (i, 0)),
            pl.BlockSpec((1, HG_DK), fixed),
            pl.BlockSpec((D, D), fixed),
            pl.BlockSpec((TM, D), lambda i: (i, 0)),
            pl.BlockSpec((None, N_MOD, D), lambda i: (_cond_idx(i, TM), 0, 0)),
        ],
        out_specs=pl.BlockSpec((TM, D), lambda i: (i, 0)),
        out_shape=jax.ShapeDtypeStruct((R, D), F32),
        compiler_params=_cparams(("arbitrary",)),
        name="hgrn_out",
    )(of_p, ob_p, of_s, ob_s, gg, o_norm, w_o, x, mods_l)


def _router_kernel(x_ref, m_ref, g_ref, rt_ref, h8_ref, idx_ref, wt_ref, cnt_ref):
    m = m_ref[...]
    h = _normmod(x_ref[...], g_ref[...], m[4:5], m[3:4])
    tm = h.shape[0]
    for s in range(ROW_TILE):
        h8_ref[pl.ds(s, tm, stride=ROW_TILE), :] = h[:, s * LANES:(s + 1) * LANES]
    h1, h2, h3 = _split3(h)
    r1, r2, r3 = _split3(rt_ref[...])
    lt = (_dot_nt(r1, h1) + _dot_nt(r1, h2) + _dot_nt(r2, h1)
          + _dot_nt(r1, h3) + _dot_nt(r3, h1) + _dot_nt(r2, h2))
    lg = lt[:N_EXP]
    e = jnp.exp(lg - jnp.max(lg, axis=0, keepdims=True))
    p = e / jnp.sum(e, axis=0, keepdims=True)
    io = lax.broadcasted_iota(I32, p.shape, 0)
    m1 = jnp.max(p, axis=0, keepdims=True)
    i1 = jnp.min(jnp.where(p == m1, io, N_EXP), axis=0, keepdims=True)
    p2 = jnp.where(io == i1, -1.0, p)
    m2 = jnp.max(p2, axis=0, keepdims=True)
    i2 = jnp.min(jnp.where(p2 == m2, io, N_EXP), axis=0, keepdims=True)
    den = m1 + m2
    idx_ref[...] = jnp.concatenate([i1, i2], axis=0)
    wt_ref[...] = jnp.concatenate([m1 / den, m2 / den], axis=0)
    chosen = jnp.where(io == i1, 1.0, 0.0) + jnp.where(io == i2, 1.0, 0.0)

    @pl.when(pl.program_id(0) == 0)
    def _():
        cnt_ref[...] = jnp.zeros_like(cnt_ref)

    cnt_ref[...] += jnp.broadcast_to(jnp.sum(chosen, axis=1, keepdims=True), cnt_ref.shape)


def _router(x, mods_l, gain, router_t):
    return pl.pallas_call(
        _router_kernel,
        grid=(R // TM,),
        in_specs=[
            pl.BlockSpec((TM, D), lambda i: (i, 0)),
            pl.BlockSpec((None, N_MOD, D), lambda i: (_cond_idx(i, TM), 0, 0)),
            pl.BlockSpec((1, D), lambda i: (0, 0)),
            pl.BlockSpec((2 * SUB, D), lambda i: (0, 0)),
        ],
        out_specs=[
            pl.BlockSpec((TM * ROW_TILE, LANES), lambda i: (i, 0)),
            pl.BlockSpec((TOP_K, TM), lambda i: (0, i)),
            pl.BlockSpec((TOP_K, TM), lambda i: (0, i)),
            pl.BlockSpec((N_EXP, LANES), lambda i: (0, 0)),
        ],
        out_shape=[
            jax.ShapeDtypeStruct((R * ROW_TILE, LANES), F32),
            jax.ShapeDtypeStruct((TOP_K, R), I32),
            jax.ShapeDtypeStruct((TOP_K, R), F32),
            jax.ShapeDtypeStruct((N_EXP, LANES), F32),
        ],
        compiler_params=_cparams(("arbitrary",)),
        name="moe_router",
    )(x, mods_l, gain, router_t)


IDX_BLOCK = 1024
DMA_UNROLL = 8


def _slot_kernel(e_ref, gs_ref, pos_ref, tri_scr, carry_scr):
    @pl.when(pl.program_id(0) == 0)
    def _():
        r = lax.broadcasted_iota(I32, tri_scr.shape, 0)
        c = lax.broadcasted_iota(I32, tri_scr.shape, 1)
        tri_scr[...] = jnp.where(r <= c, 1.0, 0.0).astype(BF16)
        carry_scr[...] = jnp.zeros_like(carry_scr)

    e = e_ref[...]
    io = lax.broadcasted_iota(I32, (2 * SUB, IDX_BLOCK), 0)
    onehot = jnp.where(io == e, 1.0, 0.0)
    cum = _dot(onehot.astype(BF16), tri_scr[...])
    carry = carry_scr[...]
    slot = cum - 1.0 + carry[:, 0:1] + gs_ref[...][:, 0:1]
    pos_ref[...] = jnp.sum(onehot * slot, axis=0, keepdims=True).astype(I32)
    carry_scr[...] = carry + jnp.broadcast_to(cum[:, IDX_BLOCK - 1:IDX_BLOCK], carry.shape)


def _slots(idx, group_start):
    n_blk = TOP_K * R // IDX_BLOCK
    gs = jnp.broadcast_to(
        jnp.concatenate([group_start, jnp.zeros((2 * SUB - N_EXP,), I32)]).astype(F32)[:, None], (2 * SUB, LANES))
    pos = pl.pallas_call(
        _slot_kernel,
        grid=(n_blk,),
        in_specs=[
            pl.BlockSpec((None, 1, IDX_BLOCK), lambda c: (c, 0, 0)),
            pl.BlockSpec((2 * SUB, LANES), lambda c: (0, 0)),
        ],
        out_specs=pl.BlockSpec((None, 1, IDX_BLOCK), lambda c: (c, 0, 0)),
        out_shape=jax.ShapeDtypeStruct((n_blk, 1, IDX_BLOCK), I32),
        scratch_shapes=[pltpu.VMEM((IDX_BLOCK, IDX_BLOCK), BF16), pltpu.VMEM((2 * SUB, LANES), F32)],
        compiler_params=_cparams(("arbitrary",)),
        name="moe_slots",
    )(idx.reshape(n_blk, 1, IDX_BLOCK), gs)
    return pos.reshape(TOP_K * R)


N_PAD = N_SLOT - TOP_K * R


def _dispatch_kernel(p0_ref, p1_ref, pad_ref, h8_ref, hs8_ref, zero_scr, sem, zsem):
    n_tok = h8_ref.shape[0] // ROW_TILE

    @pl.when(pl.program_id(0) == 0)
    def _():
        zero_scr[...] = jnp.zeros_like(zero_scr)

        def fill(g, carry):
            for u in range(DMA_UNROLL):
                slot = pad_ref[g * DMA_UNROLL + u]
                pltpu.make_async_copy(
                    zero_scr, hs8_ref.at[pl.ds(pl.multiple_of(slot * SUB, SUB), SUB)], zsem).start(priority=u % 2)
            return carry

        lax.fori_loop(0, N_PAD // DMA_UNROLL, fill, 0)
        for _ in range(N_PAD // n_tok):
            pltpu.make_async_copy(h8_ref, hs8_ref.at[pl.ds(0, n_tok * ROW_TILE)], zsem).wait()

    for p_ref in (p0_ref, p1_ref):
        def issue(g, carry, p_ref=p_ref):
            for u in range(DMA_UNROLL):
                r = g * DMA_UNROLL + u
                pltpu.make_async_copy(
                    h8_ref.at[pl.ds(pl.multiple_of(r * SUB, SUB), SUB)],
                    hs8_ref.at[pl.ds(pl.multiple_of(p_ref[r] * SUB, SUB), SUB)],
                    sem).start(priority=u % 2)
            return carry

        lax.fori_loop(0, n_tok // DMA_UNROLL, issue, 0)
    for _ in range(TOP_K):
        pltpu.make_async_copy(h8_ref, hs8_ref.at[pl.ds(0, n_tok * ROW_TILE)], sem).wait()


def _dispatch(pos, pad_slots, h8):
    nt = R // IDX_BLOCK
    return pl.pallas_call(
        _dispatch_kernel,
        grid=(nt,),
        in_specs=[
            pl.BlockSpec((IDX_BLOCK,), lambda i: (i,), memory_space=pltpu.SMEM),
            pl.BlockSpec((IDX_BLOCK,), lambda i: (nt + i,), memory_space=pltpu.SMEM),
            pl.BlockSpec((N_PAD,), lambda i: (0,), memory_space=pltpu.SMEM),
            pl.BlockSpec((IDX_BLOCK * ROW_TILE, LANES), lambda i: (i, 0)),
        ],
        out_specs=pl.BlockSpec(memory_space=pl.ANY),
        out_shape=jax.ShapeDtypeStruct((N_SLOT * ROW_TILE, LANES), F32),
        scratch_shapes=[pltpu.VMEM((SUB, LANES), F32), pltpu.SemaphoreType.DMA(()), pltpu.SemaphoreType.DMA(())],
        compiler_params=_cparams(("arbitrary",)),
        name="moe_dispatch",
    )(pos, pos, pad_slots, h8)


EXP_CHUNK = 1792
N_EXP_CHUNK = E_FF // EXP_CHUNK


def _expert_kernel(te_ref, tv_ref, x8_ref, w1_ref, w3_ref, w2_ref, y8_ref, xb_scr, acc_scr):
    i = pl.program_id(0)
    kc = pl.program_id(1)
    valid = tv_ref[i] == 1

    @pl.when(valid & (kc == 0))
    def _():
        for s in range(ROW_TILE):
            xb_scr[:, s * LANES:(s + 1) * LANES] = x8_ref[pl.ds(s, TM_MOE, stride=ROW_TILE), :].astype(BF16)
        acc_scr[...] = jnp.zeros_like(acc_scr)

    @pl.when(valid)
    def _():
        x = xb_scr[...]
        a = _dot(x, w1_ref[...])
        b = _dot(x, w3_ref[...])
        acc_scr[...] += _dot((_silu(a) * b).astype(BF16), w2_ref[...])

    @pl.when(valid & (kc == N_EXP_CHUNK - 1))
    def _():
        for s in range(ROW_TILE):
            y8_ref[pl.ds(s, TM_MOE, stride=ROW_TILE), :] = acc_scr[:, s * LANES:(s + 1) * LANES]

    @pl.when(jnp.logical_not(valid) & (kc == N_EXP_CHUNK - 1))
    def _():
        y8_ref[...] = jnp.zeros_like(y8_ref)


def _experts(tile_expert, tile_valid, hs8, w1, w3, w2, j):
    def kc_eff(kc, tv, i):
        return jnp.where(tv[i] == 1, kc, N_EXP_CHUNK - 1)

    grid_spec = pltpu.PrefetchScalarGridSpec(
        num_scalar_prefetch=2,
        grid=(N_TILE, N_EXP_CHUNK),
        in_specs=[
            pl.BlockSpec((TM_MOE * ROW_TILE, LANES), lambda i, kc, te, tv: (i, 0)),
            pl.BlockSpec((None, None, D, EXP_CHUNK), lambda i, kc, te, tv: (j, te[i], 0, kc_eff(kc, tv, i))),
            pl.BlockSpec((None, None, D, EXP_CHUNK), lambda i, kc, te, tv: (j, te[i], 0, kc_eff(kc, tv, i))),
            pl.BlockSpec((None, None, EXP_CHUNK, D), lambda i, kc, te, tv: (j, te[i], kc_eff(kc, tv, i), 0)),
        ],
        out_specs=pl.BlockSpec((TM_MOE * ROW_TILE, LANES), lambda i, kc, te, tv: (i, 0)),
        scratch_shapes=[pltpu.VMEM((TM_MOE, D), BF16), pltpu.VMEM((TM_MOE, D), F32)],
    )
    return pl.pallas_call(
        _expert_kernel,
        grid_spec=grid_spec,
        out_shape=jax.ShapeDtypeStruct((N_SLOT * ROW_TILE, LANES), F32),
        compiler_params=_cparams(("arbitrary", "arbitrary"), 56),
        name="moe_experts",
    )(tile_expert, tile_valid, hs8, w1, w3, w2)


TILES_PER_IDX_BLOCK = IDX_BLOCK // TM


def _moe_resid_kernel(p0_ref, p1_ref, y8_ref, wc_ref, x_ref, m_ref, fn_ref, o_ref, ya_scr, yb_scr, sems, *, final):
    i = pl.program_id(0)
    nt = pl.num_programs(0)
    tm = x_ref.shape[0]
    bufs = (ya_scr, yb_scr)

    def start(tile, b):
        off = (tile % TILES_PER_IDX_BLOCK) * tm
        for c, p_ref in enumerate((p0_ref, p1_ref)):
            def issue(g, carry, p_ref=p_ref, c=c):
                for u in range(DMA_UNROLL):
                    r = g * DMA_UNROLL + u
                    pltpu.make_async_copy(
                        y8_ref.at[pl.ds(pl.multiple_of(p_ref[off + r] * SUB, SUB), SUB)],
                        bufs[b].at[c, pl.ds(pl.multiple_of(r * SUB, SUB), SUB)],
                        sems.at[b]).start(priority=u % 2)
                return carry

            lax.fori_loop(0, tm // DMA_UNROLL, issue, 0)

    def finish(b):
        for c in range(TOP_K):
            pltpu.make_async_copy(y8_ref.at[pl.ds(0, tm * ROW_TILE)], bufs[b].at[c], sems.at[b]).wait()

    def rows(ref):
        return jnp.concatenate([ref[pl.ds(s, tm, stride=ROW_TILE), :] for s in range(ROW_TILE)], axis=1)

    def combine(b):
        w = wc_ref[...]
        f = w[:, 0:1] * rows(bufs[b].at[0]) + w[:, 1:2] * rows(bufs[b].at[1])
        xn = x_ref[...] + m_ref[...][5:6] * f
        if final:
            xn = _rms(xn, fn_ref[...])
        o_ref[...] = xn

    @pl.when(i == 0)
    def _():
        start(0, 0)

    for b in range(2):
        @pl.when(i % 2 == b)
        def _(b=b):
            @pl.when(i + 1 < nt)
            def _():
                start(i + 1, 1 - b)

            finish(b)
            combine(b)


def _moe_resid(pos, y8, wcol, x, mods_l, final_gain, final):
    nt = R // TM
    nblk = R // IDX_BLOCK

    def next_blk(i):
        return jnp.minimum(i + 1, nt - 1) // TILES_PER_IDX_BLOCK

    return pl.pallas_call(
        functools.partial(_moe_resid_kernel, final=final),
        grid=(nt,),
        in_specs=[
            pl.BlockSpec((IDX_BLOCK,), lambda i: (next_blk(i),), memory_space=pltpu.SMEM),
            pl.BlockSpec((IDX_BLOCK,), lambda i: (nblk + next_blk(i),), memory_space=pltpu.SMEM),
            pl.BlockSpec(memory_space=pl.ANY),
            pl.BlockSpec((TM, TOP_K), lambda i: (i, 0)),
            pl.BlockSpec((TM, D), lambda i: (i, 0)),
            pl.BlockSpec((None, N_MOD, D), lambda i: (_cond_idx(i, TM), 0, 0)),
            pl.BlockSpec((1, D), lambda i: (0, 0)),
        ],
        out_specs=pl.BlockSpec((TM, D), lambda i: (i, 0)),
        out_shape=jax.ShapeDtypeStruct((R, D), F32),
        scratch_shapes=[pltpu.VMEM((TOP_K, TM * ROW_TILE, LANES), F32), pltpu.VMEM((TOP_K, TM * ROW_TILE, LANES), F32),
                        pltpu.SemaphoreType.DMA((2,))],
        compiler_params=_cparams(("arbitrary",), 48),
        name="moe_resid",
    )(pos, pos, y8, wcol, x, mods_l, final_gain)


def _tile_tables(counts):
    padded = ((counts + TM_MOE - 1) // TM_MOE) * TM_MOE
    gend = jnp.cumsum(padded)
    tile_start = jnp.arange(N_TILE, dtype=I32) * TM_MOE
    te = jnp.sum((tile_start[:, None] >= gend[None, :]).astype(I32), axis=1)
    valid = tile_start < gend[-1]
    te_last = te[gend[-1] // TM_MOE - 1]
    te = jnp.minimum(jnp.where(valid, te, te_last), N_EXP - 1)
    gstart = gend - padded
    j = jnp.arange(TM_MOE, dtype=I32)[None, :]
    in_group = (j < (padded - counts)[:, None]).reshape(N_PAD)
    group_pad = ((gstart + counts)[:, None] + j).reshape(N_PAD)
    trailing = gend[-1] + jnp.cumsum(jnp.logical_not(in_group).astype(I32)) - 1
    pad_slots = jnp.where(in_group, group_pad, trailing)
    return gstart.astype(I32), te.astype(I32), valid.astype(I32), pad_slots.astype(I32)


def _rot_half(w):
    wa = w.reshape(w.shape[:-1] + (2, 2, ROPE_AXIS // 2))
    return jnp.stack([-wa[..., 1, :], wa[..., 0, :]], axis=-2).reshape(w.shape)


def _rope_tables(tm):
    rows = DEC_SEQ // GRID_W
    r = jnp.repeat(jnp.arange(rows), GRID_W).astype(F32)
    c = jnp.tile(jnp.arange(GRID_W), rows).astype(F32)
    inv = ROPE_THETA ** (-jnp.arange(0, ROPE_AXIS, 2, dtype=F32) / ROPE_AXIS)
    ang_r = r[:, None] * inv
    ang_c = c[:, None] * inv
    ang = jnp.concatenate([ang_r, ang_r, ang_c, ang_c], axis=-1)
    cos, sin = jnp.cos(ang), jnp.sin(ang)
    z32 = jnp.zeros_like(cos)
    one = jnp.ones((tm, QK_ROPE), F32)
    zt = jnp.zeros((tm, QK_ROPE), F32)
    tq_c = jnp.concatenate([jnp.concatenate([cos, cos, z32, z32], 1), jnp.concatenate([one, one, zt, zt], 1)], 0)
    tq_s = jnp.concatenate([jnp.concatenate([sin, sin, z32, z32], 1), jnp.zeros((tm, LANES), F32)], 0)
    tk_c = jnp.concatenate([jnp.concatenate([cos, z32, z32, z32], 1), jnp.concatenate([one, zt, zt, zt], 1)], 0)
    tk_s = jnp.concatenate([jnp.concatenate([sin, z32, z32, z32], 1), jnp.zeros((tm, LANES), F32)], 0)
    return tq_c, tq_s, tk_c, tk_s


def _mla_weights(w_dq, w_uq, w_dkv, w_uk, w_uv, w_o):
    kr_w = w_dkv[:, KV_LORA:]
    w_a = jnp.concatenate(
        [w_dq, w_dkv[:, :KV_LORA], kr_w, _rot_half(kr_w), jnp.zeros((D, LANES - 2 * QK_ROPE), F32)], axis=1)
    uq = w_uq.reshape(Q_LORA, N_PAIR, 2, QK_NOPE + QK_ROPE)
    nope = uq[..., :QK_NOPE].reshape(Q_LORA, N_PAIR, 2 * QK_NOPE)
    rope = uq[..., QK_NOPE:]
    w_uq2 = jnp.concatenate(
        [nope, rope.reshape(Q_LORA, N_PAIR, 2 * QK_ROPE), _rot_half(rope).reshape(Q_LORA, N_PAIR, 2 * QK_ROPE)],
        axis=-1).reshape(Q_LORA, N_PAIR * 2 * LANES)
    uk = w_uk.reshape(KV_LORA, N_PAIR, 2 * QK_NOPE)
    wk = jnp.concatenate([uk, jnp.zeros((KV_LORA, N_PAIR, LANES), F32)], axis=-1).reshape(KV_LORA, N_PAIR * 2 * LANES)
    eye = jnp.eye(LANES, QK_ROPE, dtype=F32)
    pair = jnp.concatenate([jnp.zeros((LANES, LANES), F32), eye, eye, jnp.zeros((LANES, LANES - 2 * QK_ROPE), F32)], 1)
    wkr = jnp.tile(pair, (1, N_PAIR))
    wv = w_uv.reshape(KV_LORA, HEADS * V_HEAD)
    wo = w_o.reshape(HEADS * V_HEAD, D)
    return [w.astype(BF16) for w in (w_a, w_uq2, wk, wkr, wv, wo)]


def kernel(x_prompt, x_sample, cache_ckv, cache_krope, state_hgrn, c, c_ctx, w_ada, b_ada, norm_mix, norm_ffn, mla_w_dq, mla_q_norm, mla_w_uq, mla_w_dkv, mla_kv_norm, mla_w_uk, mla_w_uv, mla_w_o, hg_w_q, hg_w_f, hg_w_i, hg_w_g, hg_lb_logits, hg_o_norm, hg_w_o, ffn_w1, ffn_w3, ffn_w2, moe_router, moe_w1, moe_w3, moe_w2, final_norm):
    x = (x_prompt.reshape(R_P, D), x_sample.reshape(R_S, D))
    cond = jnp.concatenate([c, c_ctx[None], jnp.zeros((COND_PAD - N_COND, D), F32)], axis=0)
    mods = _ada(cond, w_ada, b_ada)
    lb_all = _lower_bounds(hg_lb_logits)
    tq_c, tq_s, tk_c, tk_s = _rope_tables(TM)
    ffn_w = [w.astype(BF16) for w in (ffn_w1, ffn_w3, ffn_w2)]
    moe_w = [w.astype(BF16) for w in (moe_w1, moe_w3, moe_w2)]
    new_ckv, new_krope, new_hgrn = [], [], []

    for l in range(DEPTH):
        j = l // 2
        mods_l = mods[l]
        gain_mix = norm_mix[l][None]
        gain_ffn = norm_ffn[l][None]
        if l % 2 == 0:
            w_a, w_uq2, wk, wkr, wv, wo = _mla_weights(
                mla_w_dq[j], mla_w_uq[j], mla_w_dkv[j], mla_w_uk[j], mla_w_uv[j], mla_w_o[j])
            cq, ckv, kr = _mla_proj(x, mods_l, gain_mix, w_a, mla_q_norm[j][None], mla_kv_norm[j][None], tk_c, tk_s)
            q2 = _q_up(cq, w_uq2, tq_c, tq_s)
            k2_p, v_p = _kv_up(ckv, kr, R_P, wk, wkr, wv)
            ckv_s = jnp.concatenate([cache_ckv[:, j], ckv[R_P:].reshape(DEC_BATCH, DEC_SEQ, KV_LORA)], axis=1)
            kr_cache = jnp.pad(cache_krope[:, j], ((0, 0), (0, 0), (0, LANES - QK_ROPE)))
            kr_s = jnp.concatenate([kr_cache, kr[R_P:].reshape(DEC_BATCH, DEC_SEQ, LANES)], axis=1)
            s_all = PAST + DEC_SEQ
            k2_s, v_s = _kv_up(ckv_s.reshape(DEC_BATCH * s_all, KV_LORA), kr_s.reshape(DEC_BATCH * s_all, LANES),
                               DEC_BATCH * s_all, wk, wkr, wv)
            o_p = _attention(q2, k2_p, v_p, BATCH, SEQ, SEQ, 0, SEQ, N_PAIR)
            o_s = _attention(q2, k2_s, v_s, DEC_BATCH, DEC_SEQ, s_all, R_P, 512, N_PAIR)
            x = _mm_resid((o_p, o_s), wo, x, mods_l, 2)
            new_ckv.append(ckv[:R_P].reshape(BATCH, SEQ, KV_LORA))
            new_krope.append(kr[:R_P, :QK_ROPE].reshape(BATCH, SEQ, QK_ROPE))
        else:
            w5 = jnp.stack([hg_w_q[j], hg_w_f[j, 0], hg_w_f[j, 1], hg_w_i[j], hg_w_g[j]]).astype(BF16)
            q, k, lf, v, gg = _hg_proj(x, mods_l, gain_mix, w5, lb_all[j])
            s_zero = jnp.zeros((BATCH, 2, HG_H, HG_DK, HG_DK), F32)
            of_p, ob_p, st_p = _gla(q, k, lf, v, s_zero, BATCH, SEQ, 0)
            of_s, ob_s, _ = _gla(q, k, lf, v, state_hgrn[:, j], DEC_BATCH, DEC_SEQ, R_P)
            x = _hg_out(of_p, ob_p, of_s, ob_s, gg, hg_o_norm[j][None], hg_w_o[j].astype(BF16), x, mods_l)
            new_hgrn.append(st_p)
        if l % 2 == 0:
            x = _ffn(x, mods_l, gain_ffn, *ffn_w, j)
        else:
            router_t = jnp.concatenate([moe_router[j].T, jnp.zeros((2 * SUB - N_EXP, D), F32)], axis=0)
            h8, idx, wts, cnt = _router(x, mods_l, gain_ffn, router_t)
            group_start, tile_expert, tile_valid, pad_slots = _tile_tables(cnt[:, 0].astype(I32))
            pos = _slots(idx, group_start)
            y8 = _experts(tile_expert, tile_valid, _dispatch(pos, pad_slots, h8), *moe_w, j)
            x = _moe_resid(pos, y8, wts.T, x, mods_l, final_norm[None], final=(l == DEPTH - 1))

    y_prompt = x[:R_P].reshape(BATCH, SEQ, D)
    y_sample = x[R_P:].reshape(DEC_BATCH, DEC_SEQ, D)
    return (y_prompt, y_sample, jnp.stack(new_ckv, axis=1), jnp.stack(new_krope, axis=1),
            jnp.stack(new_hgrn, axis=1))
```

```python
import functools
import math

import jax
import jax.numpy as jnp
from jax import lax
from jax.experimental import pallas as pl
from jax.experimental.pallas import tpu as pltpu

F32 = jnp.float32
BF16 = jnp.bfloat16
I32 = jnp.int32

D = 1024
BATCH, SEQ = 32, 256
DEC_BATCH, DEC_SEQ = 8, 1024
PAST = 256
DEPTH = 4
R_P = BATCH * SEQ
R_S = DEC_BATCH * DEC_SEQ
R = R_P + R_S
N_COND = DEC_BATCH + 1
COND_PAD = 16
N_MOD = 6
EPS = 1e-6
GRID_W = 64
HEADS, QK_NOPE, QK_ROPE, V_HEAD = 16, 64, 32, 64
Q_LORA, KV_LORA = 512, 256
N_PAIR = HEADS // 2
ROPE_AXIS = QK_ROPE // 2
ROPE_THETA = 10000.0
HG_H, HG_DK = 8, 128
BLK = 128
CHUNK = 16
D_FF = 2816
N_EXP, TOP_K, E_FF = 8, 2, 3584
LANES = 128
SUB = 8
ROW_TILE = D // LANES
TM = 512
TM_MOE = 512
N_SLOT = TOP_K * R + N_EXP * TM_MOE
N_TILE = N_SLOT // TM_MOE
MIB = 1024 * 1024
assert DEPTH % 2 == 0


def _cparams(sem, vmem_mib=40):
    return pltpu.CompilerParams(dimension_semantics=sem, vmem_limit_bytes=vmem_mib * MIB)


def _cond_idx(i, tm):
    r0 = i * tm
    return jnp.where(r0 < R_P, DEC_BATCH, (r0 - R_P) // DEC_SEQ)


def _rope_idx(i, tm):
    nb = DEC_SEQ // tm
    r0 = i * tm
    return jnp.where(r0 < R_P, nb, ((r0 - R_P) // tm) % nb)


def _normmod(x, gain, scale, shift):
    ms = jnp.mean(x * x, axis=-1, keepdims=True)
    return (x * lax.rsqrt(ms + EPS)) * gain * (1.0 + scale) + shift


def _rms(x, gain):
    ms = jnp.mean(x * x, axis=-1, keepdims=True)
    return (x * lax.rsqrt(ms + EPS)) * gain


def _silu(x):
    return x / (1.0 + jnp.exp(-x))


def _dot(a, b):
    return jnp.dot(a, b, preferred_element_type=F32)


def _dot_nt(a, b):
    return lax.dot_general(a, b, (((1,), (1,)), ((), ())), preferred_element_type=F32)


def _split3(x):
    hi = x.astype(BF16)
    r1 = x - hi.astype(F32)
    mid = r1.astype(BF16)
    lo = (r1 - mid.astype(F32)).astype(BF16)
    return hi, mid, lo


def _ada_kernel(c_ref, w_ref, b_ref, o_ref):
    o_ref[...] = _dot(_silu(c_ref[...]), w_ref[...]) + b_ref[...]


def _ada(cond, w_ada, b_ada):
    tn = 1536
    out = pl.pallas_call(
        _ada_kernel,
        grid=(DEPTH, N_MOD * D // tn),
        in_specs=[
            pl.BlockSpec((COND_PAD, D), lambda l, j: (0, 0)),
            pl.BlockSpec((None, D, tn), lambda l, j: (l, 0, j)),
            pl.BlockSpec((None, 1, tn), lambda l, j: (l, 0, j)),
        ],
        out_specs=pl.BlockSpec((None, COND_PAD, tn), lambda l, j: (l, 0, j)),
        out_shape=jax.ShapeDtypeStruct((DEPTH, COND_PAD, N_MOD * D), F32),
        compiler_params=_cparams(("arbitrary", "arbitrary")),
        name="ada",
    )(cond, w_ada, b_ada.reshape(DEPTH, 1, N_MOD * D))
    return out[:, :N_COND].reshape(DEPTH, N_COND, N_MOD, D)


def _lb_kernel(x_ref, o_ref):
    x = x_ref[...]
    e = jnp.exp(x - jnp.max(x, axis=0, keepdims=True))
    sm = e / jnp.sum(e, axis=0, keepdims=True)
    n = x.shape[0]
    cum = sm[0]
    o_ref[0] = jnp.zeros_like(cum)
    for i in range(1, n):
        cum_i = cum + sm[i]
        o_ref[i] = cum_i - sm[0]
        cum = cum_i


def _lower_bounds(lb_logits):
    return pl.pallas_call(
        _lb_kernel,
        out_shape=jax.ShapeDtypeStruct(lb_logits.shape, F32),
        name="hgrn_lower_bounds",
    )(lb_logits)


def _two_part_rows(x):
    if isinstance(x, tuple):
        x_p, x_s = x
        s_off = 0
    else:
        x_p = x_s = x
        s_off = N_TILE_P
    w = x_p.shape[1]
    specs = [pl.BlockSpec((TM, w), lambda i: (_prompt_blk(i), 0)),
             pl.BlockSpec((TM, w), lambda i: (s_off + _sample_blk(i), 0))]
    return [x_p, x_s], specs


def _pick_rows(p_ref, s_ref):
    return jnp.where(pl.program_id(0) < N_TILE_P, p_ref[...], s_ref[...])


def _mla_proj_kernel(xp_ref, xs_ref, m_ref, g_ref, w_ref, qn_ref, kvn_ref, tc_ref, ts_ref,
                     cq_ref, ckv_ref, kr_ref):
    m = m_ref[...]
    h = _normmod(_pick_rows(xp_ref, xs_ref), g_ref[...], m[1:2], m[0:1]).astype(BF16)
    y = _dot(h, w_ref[...])
    cq_ref[...] = _rms(y[:, :Q_LORA], qn_ref[...]).astype(BF16)
    ckv_ref[...] = _rms(y[:, Q_LORA:Q_LORA + KV_LORA], kvn_ref[...])
    slab = y[:, Q_LORA + KV_LORA:]
    kr_ref[...] = slab * tc_ref[...] + pltpu.roll(slab, LANES - QK_ROPE, 1) * ts_ref[...]


def _mla_proj(x, mods_l, gain, w_a, q_norm, kv_norm, tk_c, tk_s):
    n_a = w_a.shape[1]
    row = lambda i: (i, 0)
    fixed = lambda i: (0, 0)
    x_args, x_specs = _two_part_rows(x)
    return pl.pallas_call(
        _mla_proj_kernel,
        grid=(R // TM,),
        in_specs=x_specs + [
            pl.BlockSpec((None, N_MOD, D), lambda i: (_cond_idx(i, TM), 0, 0)),
            pl.BlockSpec((1, D), fixed),
            pl.BlockSpec((D, n_a), fixed),
            pl.BlockSpec((1, Q_LORA), fixed),
            pl.BlockSpec((1, KV_LORA), fixed),
            pl.BlockSpec((TM, LANES), lambda i: (_rope_idx(i, TM), 0)),
            pl.BlockSpec((TM, LANES), lambda i: (_rope_idx(i, TM), 0)),
        ],
        out_specs=[
            pl.BlockSpec((TM, Q_LORA), row),
            pl.BlockSpec((TM, KV_LORA), row),
            pl.BlockSpec((TM, LANES), row),
        ],
        out_shape=[
            jax.ShapeDtypeStruct((R, Q_LORA), BF16),
            jax.ShapeDtypeStruct((R, KV_LORA), F32),
            jax.ShapeDtypeStruct((R, LANES), F32),
        ],
        compiler_params=_cparams(("arbitrary",)),
        name="mla_proj",
    )(*x_args, mods_l, gain, w_a, q_norm, kv_norm, tk_c, tk_s)


QK_SCALE_LOG2 = math.log2(math.e) / math.sqrt(QK_NOPE + QK_ROPE)


def _q_up_kernel(cq_ref, w_ref, tc_ref, ts_ref, q_ref):
    y = _dot(cq_ref[...], w_ref[...]) * QK_SCALE_LOG2
    tc = tc_ref[...]
    ts = ts_ref[...]
    for p in range(N_PAIR):
        lo = p * 2 * LANES
        q_ref[:, lo:lo + LANES] = y[:, lo:lo + LANES].astype(BF16)
        hi = y[:, lo + LANES:lo + 2 * LANES]
        q_ref[:, lo + LANES:lo + 2 * LANES] = (hi * tc + pltpu.roll(hi, LANES // 2, 1) * ts).astype(BF16)


def _q_up(cq, w_uq2, tq_c, tq_s):
    nq = w_uq2.shape[1]
    return pl.pallas_call(
        _q_up_kernel,
        grid=(R // TM,),
        in_specs=[
            pl.BlockSpec((TM, Q_LORA), lambda i: (i, 0)),
            pl.BlockSpec((Q_LORA, nq), lambda i: (0, 0)),
            pl.BlockSpec((TM, LANES), lambda i: (_rope_idx(i, TM), 0)),
            pl.BlockSpec((TM, LANES), lambda i: (_rope_idx(i, TM), 0)),
        ],
        out_specs=pl.BlockSpec((TM, nq), lambda i: (i, 0)),
        out_shape=jax.ShapeDtypeStruct((R, nq), BF16),
        compiler_params=_cparams(("arbitrary",)),
        name="mla_q_up",
    )(cq, w_uq2, tq_c, tq_s)


def _kv_up_kernel(c_ref, r_ref, wk_ref, wkr_ref, wv_ref, k_ref, v_ref):
    c = c_ref[...].astype(BF16)
    r = r_ref[...].astype(BF16)
    k_ref[...] = (_dot(c, wk_ref[...]) + _dot(r, wkr_ref[...])).astype(BF16)
    v_ref[...] = _dot(c, wv_ref[...]).astype(BF16)


def _kv_up(ckv, kr, n_rows, wk, wkr, wv):
    nk = wk.shape[1]
    nv = wv.shape[1]
    fixed = lambda i: (0, 0)
    return pl.pallas_call(
        _kv_up_kernel,
        grid=(n_rows // TM,),
        in_specs=[
            pl.BlockSpec((TM, KV_LORA), lambda i: (i, 0)),
            pl.BlockSpec((TM, LANES), lambda i: (i, 0)),
            pl.BlockSpec((KV_LORA, nk), fixed),
            pl.BlockSpec((LANES, nk), fixed),
            pl.BlockSpec((KV_LORA, nv), fixed),
        ],
        out_specs=[
            pl.BlockSpec((TM, nk), lambda i: (i, 0)),
            pl.BlockSpec((TM, nv), lambda i: (i, 0)),
        ],
        out_shape=[
            jax.ShapeDtypeStruct((n_rows, nk), BF16),
            jax.ShapeDtypeStruct((n_rows, nv), BF16),
        ],
        compiler_params=_cparams(("arbitrary",)),
        name="mla_kv_up",
    )(ckv, kr, wk, wkr, wv)


def _attn_kernel(q_ref, k_ref, v_ref, *rest, n_pair):
    o_ref = rest[-1]
    lq = lax.broadcasted_iota(I32, (1, 2 * LANES), 1)
    lv = lax.broadcasted_iota(I32, (1, LANES), 1)
    sel_a = (lq < QK_NOPE) | ((lq >= LANES) & (lq < LANES + QK_ROPE))
    sel_b = ((lq >= QK_NOPE) & (lq < LANES)) | ((lq >= LANES + QK_ROPE) & (lq < LANES + 2 * QK_ROPE))
    for p in range(n_pair):
        q = q_ref[:, p * 2 * LANES:(p + 1) * 2 * LANES]
        k = k_ref[:, p * 2 * LANES:(p + 1) * 2 * LANES]
        v = v_ref[:, p * LANES:(p + 1) * LANES]
        zq = jnp.zeros_like(q)
        zv = jnp.zeros_like(v)
        out = None
        for sel, vsel in ((sel_a, lv < V_HEAD), (sel_b, lv >= V_HEAD)):
            s = _dot_nt(jnp.where(sel, q, zq), k)
            e = jnp.exp2(s - jnp.max(s, axis=-1, keepdims=True))
            den = jnp.sum(e, axis=-1, keepdims=True)
            o = _dot(e.astype(BF16), jnp.where(vsel, v, zv)) / den
            out = o if out is None else out + o
        o_ref[:, p * LANES:(p + 1) * LANES] = out.astype(BF16)


def _attention(q2, k2, v, n_batch, t_len, s_len, q_row0, tq, n_pair):
    nq = t_len // tq
    qb0 = q_row0 // tq
    return pl.pallas_call(
        functools.partial(_attn_kernel, n_pair=n_pair),
        grid=(n_batch, N_PAIR // n_pair, nq),
        in_specs=[
            pl.BlockSpec((tq, n_pair * 2 * LANES), lambda b, p, i: (qb0 + b * nq + i, p)),
            pl.BlockSpec((s_len, n_pair * 2 * LANES), lambda b, p, i: (b, p)),
            pl.BlockSpec((s_len, n_pair * LANES), lambda b, p, i: (b, p)),
        ],
        out_specs=pl.BlockSpec((tq, n_pair * LANES), lambda b, p, i: (b * nq + i, p)),
        out_shape=jax.ShapeDtypeStruct((n_batch * t_len, HEADS * V_HEAD), BF16),
        compiler_params=_cparams(("arbitrary", "arbitrary", "arbitrary"), 56),
        name="mla_attention",
    )(q2, k2, v)


N_TILE_P = R_P // TM


def _prompt_blk(i):
    return jnp.minimum(i, N_TILE_P - 1)


def _sample_blk(i):
    return jnp.maximum(i - N_TILE_P, 0)


def _mm_resid_kernel(ap_ref, as_ref, w_ref, xp_ref, xs_ref, m_ref, o_ref, *, gate_idx):
    gate = m_ref[...][gate_idx:gate_idx + 1]
    o_ref[...] = _pick_rows(xp_ref, xs_ref) + gate * _dot(_pick_rows(ap_ref, as_ref), w_ref[...])


def _mm_resid(a, w, x, mods_l, gate_idx):
    a_args, a_specs = _two_part_rows(a)
    x_args, x_specs = _two_part_rows(x)
    k = a_args[0].shape[1]
    return pl.pallas_call(
        functools.partial(_mm_resid_kernel, gate_idx=gate_idx),
        grid=(R // TM,),
        in_specs=a_specs + [pl.BlockSpec((k, D), lambda i: (0, 0))] + x_specs + [
            pl.BlockSpec((None, N_MOD, D), lambda i: (_cond_idx(i, TM), 0, 0)),
        ],
        out_specs=pl.BlockSpec((TM, D), lambda i: (i, 0)),
        out_shape=jax.ShapeDtypeStruct((R, D), F32),
        compiler_params=_cparams(("arbitrary",)),
        name="mm_resid",
    )(*a_args, w, *x_args, mods_l)


FFN_CHUNK = 2816


def _ffn_kernel(x_ref, m_ref, g_ref, w1_ref, w3_ref, w2_ref, o_ref):
    x = x_ref[...]
    m = m_ref[...]
    h = _normmod(x, g_ref[...], m[4:5], m[3:4]).astype(BF16)
    acc = jnp.zeros(x.shape, F32)
    for c in range(D_FF // FFN_CHUNK):
        sl = slice(c * FFN_CHUNK, (c + 1) * FFN_CHUNK)
        a = _dot(h, w1_ref[:, sl])
        b = _dot(h, w3_ref[:, sl])
        acc = acc + _dot((_silu(a) * b).astype(BF16), w2_ref[sl, :])
    o_ref[...] = x + m[5:6] * acc


def _ffn(x, mods_l, gain, w1, w3, w2, j):
    tm = TM
    fixed = lambda i: (0, 0)
    layer = lambda i: (j, 0, 0)
    once = pl.Buffered(1)
    return pl.pallas_call(
        _ffn_kernel,
        grid=(R // tm,),
        in_specs=[
            pl.BlockSpec((tm, D), lambda i: (i, 0)),
            pl.BlockSpec((None, N_MOD, D), lambda i: (_cond_idx(i, tm), 0, 0)),
            pl.BlockSpec((1, D), fixed),
            pl.BlockSpec((None, D, D_FF), layer, pipeline_mode=once),
            pl.BlockSpec((None, D, D_FF), layer, pipeline_mode=once),
            pl.BlockSpec((None, D_FF, D), layer, pipeline_mode=once),
        ],
        out_specs=pl.BlockSpec((tm, D), lambda i: (i, 0)),
        out_shape=jax.ShapeDtypeStruct((R, D), F32),
        compiler_params=_cparams(("arbitrary",), 56),
        name="dense_swiglu",
    )(x, mods_l, gain, w1, w3, w2)


def _forget_gate(z, lb):
    e = jnp.exp(-jnp.abs(z))
    log_sig = jnp.minimum(z, 0.0) - jnp.log(1.0 + e)
    a = jnp.log(lb)
    b = jnp.log1p(-lb) + log_sig
    log_f = jnp.maximum(a, b) + jnp.log(1.0 + jnp.exp(-jnp.abs(a - b)))
    k = (1.0 - lb) * (jnp.where(z >= 0, e, 1.0) / (1.0 + e))
    return k, log_f


def _hg_proj_kernel(x_ref, m_ref, g_ref, w_ref, lb_ref, q_ref, k_ref, lf_ref, v_ref, gg_ref):
    m = m_ref[...]
    h = _normmod(x_ref[...], g_ref[...], m[1:2], m[0:1]).astype(BF16)
    q_ref[...] = _dot(h, w_ref[0])
    for dr in range(2):
        k, log_f = _forget_gate(_dot(h, w_ref[1 + dr]), lb_ref[dr])
        k_ref[dr] = k
        lf_ref[dr] = log_f
    v_ref[...] = _dot(h, w_ref[3]).astype(BF16)
    gg_ref[...] = _dot(h, w_ref[4])


def _hg_proj(x, mods_l, gain, w5, lb):
    tm = TM
    row = lambda i: (i, 0)
    dirs = lambda i: (0, i, 0)
    return pl.pallas_call(
        _hg_proj_kernel,
        grid=(R // tm,),
        in_specs=[
            pl.BlockSpec((tm, D), row),
            pl.BlockSpec((None, N_MOD, D), lambda i: (_cond_idx(i, tm), 0, 0)),
            pl.BlockSpec((1, D), lambda i: (0, 0)),
            pl.BlockSpec((5, D, D), lambda i: (0, 0, 0), pipeline_mode=pl.Buffered(1)),
            pl.BlockSpec((2, 1, D), lambda i: (0, 0, 0)),
        ],
        out_specs=[
            pl.BlockSpec((tm, D), row),
            pl.BlockSpec((2, tm, D), dirs),
            pl.BlockSpec((2, tm, D), dirs),
            pl.BlockSpec((tm, D), row),
            pl.BlockSpec((tm, D), row),
        ],
        out_shape=[
            jax.ShapeDtypeStruct((R, D), F32),
            jax.ShapeDtypeStruct((2, R, D), F32),
            jax.ShapeDtypeStruct((2, R, D), F32),
            jax.ShapeDtypeStruct((R, D), BF16),
            jax.ShapeDtypeStruct((R, D), F32),
        ],
        compiler_params=_cparams(("arbitrary",), 48),
        name="hgrn_proj",
    )(x, mods_l, gain, w5, lb.reshape(2, 1, D))


LEVEL_HALVES = (64, 32, 16)


def _gla_block(d, q_ref, k_ref, lf_ref, v_ref, o_ref, st_scr, b_scr):
    row = lax.broadcasted_iota(I32, (BLK, BLK), 0)
    col = lax.broadcasted_iota(I32, (BLK, BLK), 1)
    ut, us = (row, col) if d == 0 else (BLK - 1 - row, BLK - 1 - col)
    causal = us <= ut
    tri = jnp.where(causal, 1.0, 0.0).astype(BF16)

    hi, mid, lo = _split3(lf_ref[...])
    b_scr[d] = _dot(tri, hi) + _dot(tri, mid) + _dot(tri, lo)

    level_masks = []
    for hs in LEVEL_HALVES:
        sh = int(math.log2(2 * hs))
        same = (ut >> sh) == (us >> sh)
        level_masks.append(same & ((ut & (2 * hs - 1)) >= hs) & ((us & (2 * hs - 1)) < hs))
    base_mask = ((ut >> 4) == (us >> 4)) & causal

    def split_rows(sl, half):
        parts = []
        for jr in range(BLK // (2 * half)):
            r0 = jr * 2 * half + half - 1 + d
            parts.append(jnp.broadcast_to(b_scr[d, r0:r0 + 1, sl], (2 * half, HG_DK)))
        return parts[0] if len(parts) == 1 else jnp.concatenate(parts, axis=0)

    last = BLK - 1 if d == 0 else 0
    for h in range(HG_H):
        sl = slice(h * HG_DK, (h + 1) * HG_DK)
        bh = b_scr[d, :, sl]
        q = q_ref[:, sl]
        k = k_ref[:, sl]
        v = v_ref[:, sl]

        xq = bh - split_rows(sl, CHUNK // 2)
        att = jnp.where(base_mask, _dot_nt((q * jnp.exp(xq)).astype(BF16), (k * jnp.exp(-xq)).astype(BF16)), 0.0)
        for hs, msk in zip(LEVEL_HALVES, level_masks):
            e = jnp.exp(-jnp.abs(bh - split_rows(sl, hs)))
            att = jnp.where(msk, _dot_nt((q * e).astype(BF16), (k * e).astype(BF16)), att)

        b_last = b_scr[d, last:last + 1, sl]
        q_in = (q * jnp.exp(bh)).astype(BF16)
        k_in = (k * jnp.exp(b_last - bh)).astype(BF16)
        st = st_scr[d, h]
        o_ref[:, sl] = _dot(att.astype(BF16), v) + _dot_nt(q_in, st.astype(BF16))
        v_t = v.astype(F32).T.astype(BF16)
        st_scr[d, h] = st * jnp.exp(b_last) + _dot(v_t, k_in)


def _gla_kernel(qf_ref, kf_ref, lff_ref, vf_ref, qb_ref, kb_ref, lfb_ref, vb_ref, s0_ref,
                of_ref, ob_ref, sn_ref, st_scr, b_scr, *, nb):
    n = pl.program_id(1)

    @pl.when(n == 0)
    def _():
        for s in range(GLA_SEQ):
            for d in range(2):
                for h in range(HG_H):
                    st_scr[s, d, h] = s0_ref[s, d, h].T

    for s in range(GLA_SEQ):
        _gla_block(0, qf_ref.at[s], kf_ref.at[s], lff_ref.at[s], vf_ref.at[s], of_ref.at[s], st_scr.at[s], b_scr.at[s])
        _gla_block(1, qb_ref.at[s], kb_ref.at[s], lfb_ref.at[s], vb_ref.at[s], ob_ref.at[s], st_scr.at[s], b_scr.at[s])

    @pl.when(n == nb - 1)
    def _():
        for s in range(GLA_SEQ):
            for d in range(2):
                for h in range(HG_H):
                    sn_ref[s, d, h] = st_scr[s, d, h].T


GLA_SEQ = 2


def _gla(q, k, lf, v, s0, n_batch, t_len, row0):
    nb = t_len // BLK
    g0 = row0 // t_len // GLA_SEQ
    n_all = R // t_len
    fwd = lambda n: n
    bwd = lambda n: nb - 1 - n

    def in_specs(blk, d):
        return [
            pl.BlockSpec((GLA_SEQ, BLK, D), lambda g, n: (g0 + g, blk(n), 0)),
            pl.BlockSpec((None, GLA_SEQ, BLK, D), lambda g, n: (d, g0 + g, blk(n), 0)),
            pl.BlockSpec((None, GLA_SEQ, BLK, D), lambda g, n: (d, g0 + g, blk(n), 0)),
            pl.BlockSpec((GLA_SEQ, BLK, D), lambda g, n: (g0 + g, blk(n), 0)),
        ]

    state_spec = pl.BlockSpec((GLA_SEQ, 2, HG_H, HG_DK, HG_DK), lambda g, n: (g, 0, 0, 0, 0))
    q3, v3 = (a.reshape(n_all, t_len, D) for a in (q, v))
    k4, lf4 = (a.reshape(2, n_all, t_len, D) for a in (k, lf))
    o_f, o_b, s_n = pl.pallas_call(
        functools.partial(_gla_kernel, nb=nb),
        grid=(n_batch // GLA_SEQ, nb),
        in_specs=in_specs(fwd, 0) + in_specs(bwd, 1) + [state_spec],
        out_specs=[
            pl.BlockSpec((GLA_SEQ, BLK, D), lambda g, n: (g, fwd(n), 0)),
            pl.BlockSpec((GLA_SEQ, BLK, D), lambda g, n: (g, bwd(n), 0)),
            state_spec,
        ],
        out_shape=[
            jax.ShapeDtypeStruct((n_batch, t_len, D), F32),
            jax.ShapeDtypeStruct((n_batch, t_len, D), F32),
            jax.ShapeDtypeStruct((n_batch, 2, HG_H, HG_DK, HG_DK), F32),
        ],
        scratch_shapes=[pltpu.VMEM((GLA_SEQ, 2, HG_H, HG_DK, HG_DK), F32), pltpu.VMEM((GLA_SEQ, 2, BLK, D), F32)],
        compiler_params=_cparams(("arbitrary", "arbitrary")),
        name="hgrn_recurrence",
    )(q3, k4, lf4, v3, q3, k4, lf4, v3, s0)
    return o_f.reshape(n_batch * t_len, D), o_b.reshape(n_batch * t_len, D), s_n


def _hg_out_kernel(ofp_ref, obp_ref, ofs_ref, obs_ref, gg_ref, on_ref, w_ref, x_ref, m_ref, o_ref):
    o = jnp.where(pl.program_id(0) < N_TILE_P, ofp_ref[...] + obp_ref[...], ofs_ref[...] + obs_ref[...])
    gain = on_ref[...]
    parts = []
    for h in range(HG_H):
        parts.append(_rms(o[:, h * HG_DK:(h + 1) * HG_DK], gain))
    a = (jnp.concatenate(parts, axis=1) * _silu(gg_ref[...])).astype(BF16)
    o_ref[...] = x_ref[...] + m_ref[...][2:3] * _dot(a, w_ref[...])


def _hg_out(of_p, ob_p, of_s, ob_s, gg, o_norm, w_o, x, mods_l):
    fixed = lambda i: (0, 0)
    return pl.pallas_call(
        _hg_out_kernel,
        grid=(R // TM,),
        in_specs=[
            pl.BlockSpec((TM, D), lambda i: (_prompt_blk(i), 0)),
            pl.BlockSpec((TM, D), lambda i: (_prompt_blk(i), 0)),
            pl.BlockSpec((TM, D), lambda i: (_sample_blk(i), 0)),
            pl.BlockSpec((TM, D), lambda i: (_sample_blk(i), 0)),
            pl.BlockSpec((TM, D), lambda i: (i, 0)),
            pl.BlockSpec((1, HG_DK), fixed),
            pl.BlockSpec((D, D), fixed),
            pl.BlockSpec((TM, D), lambda i: (i, 0)),
            pl.BlockSpec((None, N_MOD, D), lambda i: (_cond_idx(i, TM), 0, 0)),
        ],
        out_specs=pl.BlockSpec((TM, D), lambda i: (i, 0)),
        out_shape=jax.ShapeDtypeStruct((R, D), F32),
        compiler_params=_cparams(("arbitrary",)),
        name="hgrn_out",
    )(of_p, ob_p, of_s, ob_s, gg, o_norm, w_o, x, mods_l)


def _router_kernel(x_ref, m_ref, g_ref, rt_ref, h8_ref, idx_ref, wt_ref, cnt_ref):
    m = m_ref[...]
    h = _normmod(x_ref[...], g_ref[...], m[4:5], m[3:4])
    tm = h.shape[0]
    for s in range(ROW_TILE):
        h8_ref[pl.ds(s, tm, stride=ROW_TILE), :] = h[:, s * LANES:(s + 1) * LANES]
    h1, h2, h3 = _split3(h)
    r1, r2, r3 = _split3(rt_ref[...])
    lt = (_dot_nt(r1, h1) + _dot_nt(r1, h2) + _dot_nt(r2, h1)
          + _dot_nt(r1, h3) + _dot_nt(r3, h1) + _dot_nt(r2, h2))
    lg = lt[:N_EXP]
    e = jnp.exp(lg - jnp.max(lg, axis=0, keepdims=True))
    p = e / jnp.sum(e, axis=0, keepdims=True)
    io = lax.broadcasted_iota(I32, p.shape, 0)
    m1 = jnp.max(p, axis=0, keepdims=True)
    i1 = jnp.min(jnp.where(p == m1, io, N_EXP), axis=0, keepdims=True)
    p2 = jnp.where(io == i1, -1.0, p)
    m2 = jnp.max(p2, axis=0, keepdims=True)
    i2 = jnp.min(jnp.where(p2 == m2, io, N_EXP), axis=0, keepdims=True)
    den = m1 + m2
    idx_ref[...] = jnp.concatenate([i1, i2], axis=0)
    wt_ref[...] = jnp.concatenate([m1 / den, m2 / den], axis=0)
    chosen = jnp.where(io == i1, 1.0, 0.0) + jnp.where(io == i2, 1.0, 0.0)

    @pl.when(pl.program_id(0) == 0)
    def _():
        cnt_ref[...] = jnp.zeros_like(cnt_ref)

    cnt_ref[...] += jnp.broadcast_to(jnp.sum(chosen, axis=1, keepdims=True), cnt_ref.shape)


def _router(x, mods_l, gain, router_t):
    return pl.pallas_call(
        _router_kernel,
        grid=(R // TM,),
        in_specs=[
            pl.BlockSpec((TM, D), lambda i: (i, 0)),
            pl.BlockSpec((None, N_MOD, D), lambda i: (_cond_idx(i, TM), 0, 0)),
            pl.BlockSpec((1, D), lambda i: (0, 0)),
            pl.BlockSpec((2 * SUB, D), lambda i: (0, 0)),
        ],
        out_specs=[
            pl.BlockSpec((TM * ROW_TILE, LANES), lambda i: (i, 0)),
            pl.BlockSpec((TOP_K, TM), lambda i: (0, i)),
            pl.BlockSpec((TOP_K, TM), lambda i: (0, i)),
            pl.BlockSpec((N_EXP, LANES), lambda i: (0, 0)),
        ],
        out_shape=[
            jax.ShapeDtypeStruct((R * ROW_TILE, LANES), F32),
            jax.ShapeDtypeStruct((TOP_K, R), I32),
            jax.ShapeDtypeStruct((TOP_K, R), F32),
            jax.ShapeDtypeStruct((N_EXP, LANES), F32),
        ],
        compiler_params=_cparams(("arbitrary",)),
        name="moe_router",
    )(x, mods_l, gain, router_t)


IDX_BLOCK = 1024
DMA_UNROLL = 16


def _slot_kernel(e_ref, gs_ref, pos_ref, tri_scr, carry_scr):
    @pl.when(pl.program_id(0) == 0)
    def _():
        r = lax.broadcasted_iota(I32, tri_scr.shape, 0)
        c = lax.broadcasted_iota(I32, tri_scr.shape, 1)
        tri_scr[...] = jnp.where(r <= c, 1.0, 0.0).astype(BF16)
        carry_scr[...] = jnp.zeros_like(carry_scr)

    e = e_ref[...]
    io = lax.broadcasted_iota(I32, (2 * SUB, IDX_BLOCK), 0)
    onehot = jnp.where(io == e, 1.0, 0.0)
    cum = _dot(onehot.astype(BF16), tri_scr[...])
    carry = carry_scr[...]
    slot = cum - 1.0 + carry[:, 0:1] + gs_ref[...][:, 0:1]
    pos_ref[...] = jnp.sum(onehot * slot, axis=0, keepdims=True).astype(I32)
    carry_scr[...] = carry + jnp.broadcast_to(cum[:, IDX_BLOCK - 1:IDX_BLOCK], carry.shape)


def _slots(idx, group_start):
    n_blk = TOP_K * R // IDX_BLOCK
    gs = jnp.broadcast_to(
        jnp.concatenate([group_start, jnp.zeros((2 * SUB - N_EXP,), I32)]).astype(F32)[:, None], (2 * SUB, LANES))
    pos = pl.pallas_call(
        _slot_kernel,
        grid=(n_blk,),
        in_specs=[
            pl.BlockSpec((None, 1, IDX_BLOCK), lambda c: (c, 0, 0)),
            pl.BlockSpec((2 * SUB, LANES), lambda c: (0, 0)),
        ],
        out_specs=pl.BlockSpec((None, 1, IDX_BLOCK), lambda c: (c, 0, 0)),
        out_shape=jax.ShapeDtypeStruct((n_blk, 1, IDX_BLOCK), I32),
        scratch_shapes=[pltpu.VMEM((IDX_BLOCK, IDX_BLOCK), BF16), pltpu.VMEM((2 * SUB, LANES), F32)],
        compiler_params=_cparams(("arbitrary",)),
        name="moe_slots",
    )(idx.reshape(n_blk, 1, IDX_BLOCK), gs)
    return pos.reshape(TOP_K * R)


N_PAD = N_SLOT - TOP_K * R


def _dispatch_kernel(p0_ref, p1_ref, pad_ref, h8_ref, hs8_ref, zero_scr, sem, zsem):
    n_tok = h8_ref.shape[0] // ROW_TILE

    @pl.when(pl.program_id(0) == 0)
    def _():
        zero_scr[...] = jnp.zeros_like(zero_scr)

        def fill(g, carry):
            for u in range(DMA_UNROLL):
                slot = pad_ref[g * DMA_UNROLL + u]
                pltpu.make_async_copy(
                    zero_scr, hs8_ref.at[pl.ds(pl.multiple_of(slot * SUB, SUB), SUB)], zsem).start(priority=u % 2)
            return carry

        lax.fori_loop(0, N_PAD // DMA_UNROLL, fill, 0)
        for _ in range(N_PAD // n_tok):
            pltpu.make_async_copy(h8_ref, hs8_ref.at[pl.ds(0, n_tok * ROW_TILE)], zsem).wait()

    for p_ref in (p0_ref, p1_ref):
        def issue(g, carry, p_ref=p_ref):
            for u in range(DMA_UNROLL):
                r = g * DMA_UNROLL + u
                pltpu.make_async_copy(
                    h8_ref.at[pl.ds(pl.multiple_of(r * SUB, SUB), SUB)],
                    hs8_ref.at[pl.ds(pl.multiple_of(p_ref[r] * SUB, SUB), SUB)],
                    sem).start(priority=u % 2)
            return carry

        lax.fori_loop(0, n_tok // DMA_UNROLL, issue, 0)
    for _ in range(TOP_K):
        pltpu.make_async_copy(h8_ref, hs8_ref.at[pl.ds(0, n_tok * ROW_TILE)], sem).wait()


def _dispatch(pos, pad_slots, h8):
    nt = R // IDX_BLOCK
    return pl.pallas_call(
        _dispatch_kernel,
        grid=(nt,),
        in_specs=[
            pl.BlockSpec((IDX_BLOCK,), lambda i: (i,), memory_space=pltpu.SMEM),
            pl.BlockSpec((IDX_BLOCK,), lambda i: (nt + i,), memory_space=pltpu.SMEM),
            pl.BlockSpec((N_PAD,), lambda i: (0,), memory_space=pltpu.SMEM),
            pl.BlockSpec((IDX_BLOCK * ROW_TILE, LANES), lambda i: (i, 0)),
        ],
        out_specs=pl.BlockSpec(memory_space=pl.ANY),
        out_shape=jax.ShapeDtypeStruct((N_SLOT * ROW_TILE, LANES), F32),
        scratch_shapes=[pltpu.VMEM((SUB, LANES), F32), pltpu.SemaphoreType.DMA(()), pltpu.SemaphoreType.DMA(())],
        compiler_params=_cparams(("arbitrary",)),
        name="moe_dispatch",
    )(pos, pos, pad_slots, h8)


EXP_CHUNK = 1792
N_EXP_CHUNK = E_FF // EXP_CHUNK


def _expert_kernel(te_ref, tv_ref, x8_ref, w1_ref, w3_ref, w2_ref, y8_ref, xb_scr, acc_scr):
    i = pl.program_id(0)
    kc = pl.program_id(1)
    valid = tv_ref[i] == 1

    @pl.when(valid & (kc == 0))
    def _():
        for s in range(ROW_TILE):
            xb_scr[:, s * LANES:(s + 1) * LANES] = x8_ref[pl.ds(s, TM_MOE, stride=ROW_TILE), :].astype(BF16)
        acc_scr[...] = jnp.zeros_like(acc_scr)

    @pl.when(valid)
    def _():
        x = xb_scr[...]
        a = _dot(x, w1_ref[...])
        b = _dot(x, w3_ref[...])
        acc_scr[...] += _dot((_silu(a) * b).astype(BF16), w2_ref[...])

    @pl.when(valid & (kc == N_EXP_CHUNK - 1))
    def _():
        for s in range(ROW_TILE):
            y8_ref[pl.ds(s, TM_MOE, stride=ROW_TILE), :] = acc_scr[:, s * LANES:(s + 1) * LANES]

    @pl.when(jnp.logical_not(valid) & (kc == N_EXP_CHUNK - 1))
    def _():
        y8_ref[...] = jnp.zeros_like(y8_ref)


def _experts(tile_expert, tile_valid, hs8, w1, w3, w2, j):
    def kc_eff(kc, tv, i):
        return jnp.where(tv[i] == 1, kc, N_EXP_CHUNK - 1)

    grid_spec = pltpu.PrefetchScalarGridSpec(
        num_scalar_prefetch=2,
        grid=(N_TILE, N_EXP_CHUNK),
        in_specs=[
            pl.BlockSpec((TM_MOE * ROW_TILE, LANES), lambda i, kc, te, tv: (i, 0)),
            pl.BlockSpec((None, None, D, EXP_CHUNK), lambda i, kc, te, tv: (j, te[i], 0, kc_eff(kc, tv, i))),
            pl.BlockSpec((None, None, D, EXP_CHUNK), lambda i, kc, te, tv: (j, te[i], 0, kc_eff(kc, tv, i))),
            pl.BlockSpec((None, None, EXP_CHUNK, D), lambda i, kc, te, tv: (j, te[i], kc_eff(kc, tv, i), 0)),
        ],
        out_specs=pl.BlockSpec((TM_MOE * ROW_TILE, LANES), lambda i, kc, te, tv: (i, 0)),
        scratch_shapes=[pltpu.VMEM((TM_MOE, D), BF16), pltpu.VMEM((TM_MOE, D), F32)],
    )
    return pl.pallas_call(
        _expert_kernel,
        grid_spec=grid_spec,
        out_shape=jax.ShapeDtypeStruct((N_SLOT * ROW_TILE, LANES), F32),
        compiler_params=_cparams(("arbitrary", "arbitrary"), 56),
        name="moe_experts",
    )(tile_expert, tile_valid, hs8, w1, w3, w2)


TILES_PER_IDX_BLOCK = IDX_BLOCK // TM


def _moe_resid_kernel(p0_ref, p1_ref, y8_ref, wc_ref, x_ref, m_ref, fn_ref, o_ref, ya_scr, yb_scr, sems, *, final):
    i = pl.program_id(0)
    nt = pl.num_programs(0)
    tm = x_ref.shape[0]
    bufs = (ya_scr, yb_scr)

    def start(tile, b):
        off = (tile % TILES_PER_IDX_BLOCK) * tm
        for c, p_ref in enumerate((p0_ref, p1_ref)):
            def issue(g, carry, p_ref=p_ref, c=c):
                for u in range(DMA_UNROLL):
                    r = g * DMA_UNROLL + u
                    pltpu.make_async_copy(
                        y8_ref.at[pl.ds(pl.multiple_of(p_ref[off + r] * SUB, SUB), SUB)],
                        bufs[b].at[c, pl.ds(pl.multiple_of(r * SUB, SUB), SUB)],
                        sems.at[b]).start(priority=u % 2)
                return carry

            lax.fori_loop(0, tm // DMA_UNROLL, issue, 0)

    def finish(b):
        for c in range(TOP_K):
            pltpu.make_async_copy(y8_ref.at[pl.ds(0, tm * ROW_TILE)], bufs[b].at[c], sems.at[b]).wait()

    def rows(ref):
        return jnp.concatenate([ref[pl.ds(s, tm, stride=ROW_TILE), :] for s in range(ROW_TILE)], axis=1)

    def combine(b):
        w = wc_ref[...]
        f = w[:, 0:1] * rows(bufs[b].at[0]) + w[:, 1:2] * rows(bufs[b].at[1])
        xn = x_ref[...] + m_ref[...][5:6] * f
        if final:
            xn = _rms(xn, fn_ref[...])
        o_ref[...] = xn

    @pl.when(i == 0)
    def _():
        start(0, 0)

    for b in range(2):
        @pl.when(i % 2 == b)
        def _(b=b):
            @pl.when(i + 1 < nt)
            def _():
                start(i + 1, 1 - b)

            finish(b)
            combine(b)


def _moe_resid(pos, y8, wcol, x, mods_l, final_gain, final):
    nt = R // TM
    nblk = R // IDX_BLOCK

    def next_blk(i):
        return jnp.minimum(i + 1, nt - 1) // TILES_PER_IDX_BLOCK

    return pl.pallas_call(
        functools.partial(_moe_resid_kernel, final=final),
        grid=(nt,),
        in_specs=[
            pl.BlockSpec((IDX_BLOCK,), lambda i: (next_blk(i),), memory_space=pltpu.SMEM),
            pl.BlockSpec((IDX_BLOCK,), lambda i: (nblk + next_blk(i),), memory_space=pltpu.SMEM),
            pl.BlockSpec(memory_space=pl.ANY),
            pl.BlockSpec((TM, TOP_K), lambda i: (i, 0)),
            pl.BlockSpec((TM, D), lambda i: (i, 0)),
            pl.BlockSpec((None, N_MOD, D), lambda i: (_cond_idx(i, TM), 0, 0)),
            pl.BlockSpec((1, D), lambda i: (0, 0)),
        ],
        out_specs=pl.BlockSpec((TM, D), lambda i: (i, 0)),
        out_shape=jax.ShapeDtypeStruct((R, D), F32),
        scratch_shapes=[pltpu.VMEM((TOP_K, TM * ROW_TILE, LANES), F32), pltpu.VMEM((TOP_K, TM * ROW_TILE, LANES), F32),
                        pltpu.SemaphoreType.DMA((2,))],
        compiler_params=_cparams(("arbitrary",), 48),
        name="moe_resid",
    )(pos, pos, y8, wcol, x, mods_l, final_gain)


def _tile_tables(counts):
    padded = ((counts + TM_MOE - 1) // TM_MOE) * TM_MOE
    gend = jnp.cumsum(padded)
    tile_start = jnp.arange(N_TILE, dtype=I32) * TM_MOE
    te = jnp.sum((tile_start[:, None] >= gend[None, :]).astype(I32), axis=1)
    valid = tile_start < gend[-1]
    te_last = te[gend[-1] // TM_MOE - 1]
    te = jnp.minimum(jnp.where(valid, te, te_last), N_EXP - 1)
    gstart = gend - padded
    j = jnp.arange(TM_MOE, dtype=I32)[None, :]
    in_group = (j < (padded - counts)[:, None]).reshape(N_PAD)
    group_pad = ((gstart + counts)[:, None] + j).reshape(N_PAD)
    trailing = gend[-1] + jnp.cumsum(jnp.logical_not(in_group).astype(I32)) - 1
    pad_slots = jnp.where(in_group, group_pad, trailing)
    return gstart.astype(I32), te.astype(I32), valid.astype(I32), pad_slots.astype(I32)


def _rot_half(w):
    wa = w.reshape(w.shape[:-1] + (2, 2, ROPE_AXIS // 2))
    return jnp.stack([-wa[..., 1, :], wa[..., 0, :]], axis=-2).reshape(w.shape)


def _rope_tables(tm):
    rows = DEC_SEQ // GRID_W
    r = jnp.repeat(jnp.arange(rows), GRID_W).astype(F32)
    c = jnp.tile(jnp.arange(GRID_W), rows).astype(F32)
    inv = ROPE_THETA ** (-jnp.arange(0, ROPE_AXIS, 2, dtype=F32) / ROPE_AXIS)
    ang_r = r[:, None] * inv
    ang_c = c[:, None] * inv
    ang = jnp.concatenate([ang_r, ang_r, ang_c, ang_c], axis=-1)
    cos, sin = jnp.cos(ang), jnp.sin(ang)
    z32 = jnp.zeros_like(cos)
    one = jnp.ones((tm, QK_ROPE), F32)
    zt = jnp.zeros((tm, QK_ROPE), F32)
    tq_c = jnp.concatenate([jnp.concatenate([cos, cos, z32, z32], 1), jnp.concatenate([one, one, zt, zt], 1)], 0)
    tq_s = jnp.concatenate([jnp.concatenate([sin, sin, z32, z32], 1), jnp.zeros((tm, LANES), F32)], 0)
    tk_c = jnp.concatenate([jnp.concatenate([cos, z32, z32, z32], 1), jnp.concatenate([one, zt, zt, zt], 1)], 0)
    tk_s = jnp.concatenate([jnp.concatenate([sin, z32, z32, z32], 1), jnp.zeros((tm, LANES), F32)], 0)
    return tq_c, tq_s, tk_c, tk_s


def _mla_weights(w_dq, w_uq, w_dkv, w_uk, w_uv, w_o):
    kr_w = w_dkv[:, KV_LORA:]
    w_a = jnp.concatenate(
        [w_dq, w_dkv[:, :KV_LORA], kr_w, _rot_half(kr_w), jnp.zeros((D, LANES - 2 * QK_ROPE), F32)], axis=1)
    uq = w_uq.reshape(Q_LORA, N_PAIR, 2, QK_NOPE + QK_ROPE)
    nope = uq[..., :QK_NOPE].reshape(Q_LORA, N_PAIR, 2 * QK_NOPE)
    rope = uq[..., QK_NOPE:]
    w_uq2 = jnp.concatenate(
        [nope, rope.reshape(Q_LORA, N_PAIR, 2 * QK_ROPE), _rot_half(rope).reshape(Q_LORA, N_PAIR, 2 * QK_ROPE)],
        axis=-1).reshape(Q_LORA, N_PAIR * 2 * LANES)
    uk = w_uk.reshape(KV_LORA, N_PAIR, 2 * QK_NOPE)
    wk = jnp.concatenate([uk, jnp.zeros((KV_LORA, N_PAIR, LANES), F32)], axis=-1).reshape(KV_LORA, N_PAIR * 2 * LANES)
    eye = jnp.eye(LANES, QK_ROPE, dtype=F32)
    pair = jnp.concatenate([jnp.zeros((LANES, LANES), F32), eye, eye, jnp.zeros((LANES, LANES - 2 * QK_ROPE), F32)], 1)
    wkr = jnp.tile(pair, (1, N_PAIR))
    wv = w_uv.reshape(KV_LORA, HEADS * V_HEAD)
    wo = w_o.reshape(HEADS * V_HEAD, D)
    return [w.astype(BF16) for w in (w_a, w_uq2, wk, wkr, wv, wo)]


def kernel(x_prompt, x_sample, cache_ckv, cache_krope, state_hgrn, c, c_ctx, w_ada, b_ada, norm_mix, norm_ffn, mla_w_dq, mla_q_norm, mla_w_uq, mla_w_dkv, mla_kv_norm, mla_w_uk, mla_w_uv, mla_w_o, hg_w_q, hg_w_f, hg_w_i, hg_w_g, hg_lb_logits, hg_o_norm, hg_w_o, ffn_w1, ffn_w3, ffn_w2, moe_router, moe_w1, moe_w3, moe_w2, final_norm):
    x = (x_prompt.reshape(R_P, D), x_sample.reshape(R_S, D))
    cond = jnp.concatenate([c, c_ctx[None], jnp.zeros((COND_PAD - N_COND, D), F32)], axis=0)
    mods = _ada(cond, w_ada, b_ada)
    lb_all = _lower_bounds(hg_lb_logits)
    tq_c, tq_s, tk_c, tk_s = _rope_tables(TM)
    ffn_w = [w.astype(BF16) for w in (ffn_w1, ffn_w3, ffn_w2)]
    moe_w = [w.astype(BF16) for w in (moe_w1, moe_w3, moe_w2)]
    new_ckv, new_krope, new_hgrn = [], [], []

    for l in range(DEPTH):
        j = l // 2
        mods_l = mods[l]
        gain_mix = norm_mix[l][None]
        gain_ffn = norm_ffn[l][None]
        if l % 2 == 0:
            w_a, w_uq2, wk, wkr, wv, wo = _mla_weights(
                mla_w_dq[j], mla_w_uq[j], mla_w_dkv[j], mla_w_uk[j], mla_w_uv[j], mla_w_o[j])
            cq, ckv, kr = _mla_proj(x, mods_l, gain_mix, w_a, mla_q_norm[j][None], mla_kv_norm[j][None], tk_c, tk_s)
            q2 = _q_up(cq, w_uq2, tq_c, tq_s)
            k2_p, v_p = _kv_up(ckv, kr, R_P, wk, wkr, wv)
            ckv_s = jnp.concatenate([cache_ckv[:, j], ckv[R_P:].reshape(DEC_BATCH, DEC_SEQ, KV_LORA)], axis=1)
            kr_cache = jnp.pad(cache_krope[:, j], ((0, 0), (0, 0), (0, LANES - QK_ROPE)))
            kr_s = jnp.concatenate([kr_cache, kr[R_P:].reshape(DEC_BATCH, DEC_SEQ, LANES)], axis=1)
            s_all = PAST + DEC_SEQ
            k2_s, v_s = _kv_up(ckv_s.reshape(DEC_BATCH * s_all, KV_LORA), kr_s.reshape(DEC_BATCH * s_all, LANES),
                               DEC_BATCH * s_all, wk, wkr, wv)
            o_p = _attention(q2, k2_p, v_p, BATCH, SEQ, SEQ, 0, SEQ, N_PAIR)
            o_s = _attention(q2, k2_s, v_s, DEC_BATCH, DEC_SEQ, s_all, R_P, 512, N_PAIR)
            x = _mm_resid((o_p, o_s), wo, x, mods_l, 2)
            new_ckv.append(ckv[:R_P].reshape(BATCH, SEQ, KV_LORA))
            new_krope.append(kr[:R_P, :QK_ROPE].reshape(BATCH, SEQ, QK_ROPE))
        else:
            w5 = jnp.stack([hg_w_q[j], hg_w_f[j, 0], hg_w_f[j, 1], hg_w_i[j], hg_w_g[j]]).astype(BF16)
            q, k, lf, v, gg = _hg_proj(x, mods_l, gain_mix, w5, lb_all[j])
            s_zero = jnp.zeros((BATCH, 2, HG_H, HG_DK, HG_DK), F32)
            of_p, ob_p, st_p = _gla(q, k, lf, v, s_zero, BATCH, SEQ, 0)
            of_s, ob_s, _ = _gla(q, k, lf, v, state_hgrn[:, j], DEC_BATCH, DEC_SEQ, R_P)
            x = _hg_out(of_p, ob_p, of_s, ob_s, gg, hg_o_norm[j][None], hg_w_o[j].astype(BF16), x, mods_l)
            new_hgrn.append(st_p)
        if l % 2 == 0:
            x = _ffn(x, mods_l, gain_ffn, *ffn_w, j)
        else:
            router_t = jnp.concatenate([moe_router[j].T, jnp.zeros((2 * SUB - N_EXP, D), F32)], axis=0)
            h8, idx, wts, cnt = _router(x, mods_l, gain_ffn, router_t)
            group_start, tile_expert, tile_valid, pad_slots = _tile_tables(cnt[:, 0].astype(I32))
            pos = _slots(idx, group_start)
            y8 = _experts(tile_expert, tile_valid, _dispatch(pos, pad_slots, h8), *moe_w, j)
            x = _moe_resid(pos, y8, wts.T, x, mods_l, final_norm[None], final=(l == DEPTH - 1))

    y_prompt = x[:R_P].reshape(BATCH, SEQ, D)
    y_sample = x[R_P:].reshape(DEC_BATCH, DEC_SEQ, D)
    return (y_prompt, y_sample, jnp.stack(new_ckv, axis=1), jnp.stack(new_krope, axis=1),
            jnp.stack(new_hgrn, axis=1))
```
